```python
import math
import jax
import jax.numpy as jnp
from jax import lax
import numpy as np

D_MODEL = 4096
BATCH = 2
SEQ = 4096
DEPTH = 4

GRID_W = 64
CTX_LEN = 256

N_BRANCH = 3
BRANCH_WIDTH = D_MODEL // 4

HEAD_DIM = 128
N_Q_HEADS = BRANCH_WIDTH // HEAD_DIM
N_KV_HEADS = N_Q_HEADS // 4
Q_PER_KV = N_Q_HEADS // N_KV_HEADS
KV_WIDTH = N_KV_HEADS * HEAD_DIM
N_ROPE_FREQ = HEAD_DIM // 4
ROPE_THETA = 10000.0
Q_BLOCK = 128

RNN_WIDTH = BRANCH_WIDTH
RNN_BLOCKS = 8
RNN_BLOCK_DIM = RNN_WIDTH // RNN_BLOCKS
LRU_C = 8.0
CONV_W = 4

SSD_WIDTH = BRANCH_WIDTH
SSD_HEAD_DIM = 64
SSD_HEADS = SSD_WIDTH // SSD_HEAD_DIM
SSD_GROUPS = 2
SSD_STATE = 128
SSD_CHUNK = 128
SSD_BC = SSD_GROUPS * SSD_STATE
SSD_XBC = SSD_WIDTH + 2 * SSD_BC

Q_OFF = 0
K_OFF = Q_OFF + BRANCH_WIDTH
V_OFF = K_OFF + KV_WIDTH
RX_OFF = V_OFF + KV_WIDTH
RG_OFF = RX_OFF + RNN_WIDTH
SZ_OFF = RG_OFF + RNN_WIDTH
SX_OFF = SZ_OFF + SSD_WIDTH
SDT_OFF = SX_OFF + SSD_XBC
MIX_COLS = SDT_OFF + 2 * SSD_HEADS
IN_COLS = MIX_COLS + N_BRANCH * D_MODEL

MOD_RANK = 256
N_MOD = 6

N_EXPERTS = 16
N_EXPERT_GROUPS = 4
EXPERTS_PER_GROUP = N_EXPERTS // N_EXPERT_GROUPS
TOP_K = 2
EXPERT_FF = D_MODEL // 8

EPS = 1e-6

kernel_name = 'hybrid_rglru_gqa_ssd_moe_prefix_dit'


def rmsnorm(x, g):
    xf = x.astype(jnp.float32)
    y = xf * lax.rsqrt(jnp.mean(xf * xf, axis=-1, keepdims=True) + EPS)
    return (y * g.astype(jnp.float32)).astype(x.dtype)


def adaln(cond, w_a, w_b, b):
    m = (jax.nn.silu(cond) @ w_a) @ w_b + b
    return m.reshape(cond.shape[0], 1, N_MOD, D_MODEL)


def modulate(h, mod, k):
    return h * (1 + mod[..., k + 1, :]) + mod[..., k, :]


def axial_rope_tables(n_tokens, dtype):
    rows = n_tokens // GRID_W
    row = jnp.repeat(jnp.arange(rows, dtype=jnp.float32), GRID_W)
    col = jnp.tile(jnp.arange(GRID_W, dtype=jnp.float32), rows)
    inv = ROPE_THETA ** (-jnp.arange(N_ROPE_FREQ, dtype=jnp.float32) / N_ROPE_FREQ)
    ang = jnp.stack([row[:, None] * inv, col[:, None] * inv], axis=1)
    return jnp.cos(ang).astype(dtype), jnp.sin(ang).astype(dtype)


def apply_rope(t, cos, sin):
    shp = t.shape
    t = t.reshape(shp[:-1] + (2, 2, N_ROPE_FREQ))
    t1, t2 = t[..., 0, :], t[..., 1, :]
    cs, sn = cos[:, None], sin[:, None]
    out = jnp.stack([t1 * cs - t2 * sn, t2 * cs + t1 * sn], axis=-2)
    return out.reshape(shp)


def centred_dwconv(x, w, b):
    lo = (CONV_W - 1) // 2
    y = lax.conv_general_dilated(x, w[:, None, :].astype(x.dtype), window_strides=(1,),
                                 padding=[(lo, CONV_W - 1 - lo)],
                                 dimension_numbers=('NWC', 'WIO', 'NWC'),
                                 feature_group_count=x.shape[-1])
    return y + b


def linear_scan(a, u, h0, reverse):
    if h0 is not None:
        first = -1 if reverse else 0
        u = u.at[:, first].add(a[:, first] * h0)

    def comb(e1, e2):
        return e1[0] * e2[0], e2[0] * e1[1] + e2[1]

    _, h = lax.associative_scan(comb, (a, u), axis=1, reverse=reverse)
    return h


def attend(q, k, v):
    s = jnp.einsum('btkgd,bskd->bkgts', q, k, preferred_element_type=jnp.float32) * (HEAD_DIM ** -0.5)
    p = jax.nn.softmax(s, axis=-1).astype(v.dtype)
    return jnp.einsum('bkgts,bskd->btkgd', p, v)


def gqa_branch(pl, pc, q_norm, k_norm, cos, sin, ctx_out):
    def heads(p, off, n):
        return p[..., off:off + n * HEAD_DIM].reshape(p.shape[:2] + (n, HEAD_DIM))

    ql = apply_rope(rmsnorm(heads(pl, Q_OFF, N_Q_HEADS), q_norm), cos, sin)
    kl = apply_rope(rmsnorm(heads(pl, K_OFF, N_KV_HEADS), k_norm), cos, sin)
    vl = heads(pl, V_OFF, N_KV_HEADS)
    qc = rmsnorm(heads(pc, Q_OFF, N_Q_HEADS), q_norm)
    kc = rmsnorm(heads(pc, K_OFF, N_KV_HEADS), k_norm)
    vc = heads(pc, V_OFF, N_KV_HEADS)
    k_all = jnp.concatenate([kc, kl], axis=1)
    v_all = jnp.concatenate([vc, vl], axis=1)
    b, s = pl.shape[:2]
    qb = ql.reshape(b, s // Q_BLOCK, Q_BLOCK, N_KV_HEADS, Q_PER_KV, HEAD_DIM).swapaxes(0, 1)
    yl = lax.map(lambda q: attend(q, k_all, v_all), qb)
    yl = yl.swapaxes(0, 1).reshape(b, s, BRANCH_WIDTH)
    yc = None
    if ctx_out:
        n_ctx = pc.shape[1]
        yc = attend(qc.reshape(b, n_ctx, N_KV_HEADS, Q_PER_KV, HEAD_DIM), kc, vc).reshape(b, n_ctx, BRANCH_WIDTH)
    return yl, yc


def rglru_direction(xc, xl, lam, w_r, b_r, w_i, b_i, reverse, ctx_out):
    def gates_and_input(x):
        xf = x.astype(jnp.float32)
        xb = xf.reshape(x.shape[:2] + (RNN_BLOCKS, RNN_BLOCK_DIM))
        r = jax.nn.sigmoid(jnp.einsum('btnd,nde->btne', xb, w_r).reshape(x.shape) + b_r)
        i = jax.nn.sigmoid(jnp.einsum('btnd,nde->btne', xb, w_i).reshape(x.shape) + b_i)
        log_a = -LRU_C * r * jax.nn.softplus(-lam)
        return jnp.exp(log_a), jnp.sqrt(-jnp.expm1(2 * log_a)) * (i * xf)

    a_c, u_c = gates_and_input(xc)
    h_c = linear_scan(a_c, u_c, None, reverse)
    h_c_final = h_c[:, 0] if reverse else h_c[:, -1]
    a_l, u_l = gates_and_input(xl)
    h_l = linear_scan(a_l, u_l, h_c_final, reverse)
    return h_l, (h_c if ctx_out else None)


def rglru_branch(pl, pc, conv_w, conv_b, lam, w_r, b_r, w_i, b_i, ctx_out):
    xl = centred_dwconv(pl[..., RX_OFF:RX_OFF + RNN_WIDTH], conv_w, conv_b)
    xc = centred_dwconv(pc[..., RX_OFF:RX_OFF + RNN_WIDTH], conv_w, conv_b)
    hf_l, hf_c = rglru_direction(xc, xl, lam[0], w_r[0], b_r[0], w_i[0], b_i[0], False, ctx_out)
    hb_l, hb_c = rglru_direction(xc, xl, lam[1], w_r[1], b_r[1], w_i[1], b_i[1], True, ctx_out)
    yl = (jax.nn.gelu(pl[..., RG_OFF:RG_OFF + RNN_WIDTH]) * (hf_l + hb_l)).astype(pl.dtype)
    yc = None
    if ctx_out:
        yc = (jax.nn.gelu(pc[..., RG_OFF:RG_OFF + RNN_WIDTH]) * (hf_c + hb_c)).astype(pc.dtype)
    return yl, yc


def ssd_scan(x, dt, a_head, bm, cm, h0, reverse, with_y):
    if reverse:
        x, dt, bm, cm = (jnp.flip(t, axis=1) for t in (x, dt, bm, cm))
    bsz, t_len = x.shape[:2]
    nc = t_len // SSD_CHUNK
    hg = SSD_HEADS // SSD_GROUPS
    xdt = (x.astype(jnp.float32) * dt[..., None]).reshape(bsz, nc, SSD_CHUNK, SSD_GROUPS, hg, SSD_HEAD_DIM)
    a = (dt * a_head).reshape(bsz, nc, SSD_CHUNK, SSD_GROUPS, hg)
    bc = bm.astype(jnp.float32).reshape(bsz, nc, SSD_CHUNK, SSD_GROUPS, SSD_STATE)
    cc = cm.astype(jnp.float32).reshape(bsz, nc, SSD_CHUNK, SSD_GROUPS, SSD_STATE)
    a_cum = jnp.cumsum(a, axis=2)
    a_tot = a_cum[:, :, -1]
    states = jnp.einsum('bcqgn,bcqgh,bcqghp->bcghpn', bc, jnp.exp(a_tot[:, :, None] - a_cum), xdt)
    s_end = linear_scan(jnp.exp(a_tot)[..., None, None], states, h0, False)
    final = s_end[:, -1]
    if not with_y:
        return None, final
    init = h0 if h0 is not None else jnp.zeros_like(final)
    s_in = jnp.concatenate([init[:, None], s_end[:, :-1]], axis=1)
    diff = a_cum[:, :, :, None] - a_cum[:, :, None]
    lower = jnp.tril(jnp.ones((SSD_CHUNK, SSD_CHUNK), dtype=bool))[:, :, None, None]
    lmat = jnp.exp(jnp.where(lower, diff, -jnp.inf))
    cb = jnp.einsum('bcign,bcjgn->bcijg', cc, bc)
    y = jnp.einsum('bcijg,bcijgh,bcjghp->bcighp', cb, lmat, xdt)
    y = y + jnp.einsum('bcign,bcghpn,bcigh->bcighp', cc, s_in, jnp.exp(a_cum))
    y = y.reshape(bsz, t_len, SSD_HEADS, SSD_HEAD_DIM).astype(x.dtype)
    if reverse:
        y = jnp.flip(y, axis=1)
    return y, final


def ssd_branch(pl, pc, conv_w, conv_b, dt_bias, a_log, d_skip, norm_g, ctx_out):
    def prep(p):
        bsz, t_len = p.shape[:2]
        xbc = jax.nn.silu(centred_dwconv(p[..., SX_OFF:SX_OFF + SSD_XBC], conv_w, conv_b))
        xs = xbc[..., :SSD_WIDTH].reshape(bsz, t_len, SSD_HEADS, SSD_HEAD_DIM)
        bm = xbc[..., SSD_WIDTH:SSD_WIDTH + SSD_BC].reshape(bsz, t_len, SSD_GROUPS, SSD_STATE)
        cm = xbc[..., SSD_WIDTH + SSD_BC:].reshape(bsz, t_len, SSD_GROUPS, SSD_STATE)
        dt_raw = p[..., SDT_OFF:SDT_OFF + 2 * SSD_HEADS].astype(jnp.float32).reshape(bsz, t_len, 2, SSD_HEADS)
        return xs, bm, cm, dt_raw

    def gated_norm(y, p):
        z = p[..., SZ_OFF:SZ_OFF + SSD_WIDTH]
        g = (y.reshape(z.shape) * jax.nn.silu(z)).reshape(z.shape[:2] + (SSD_GROUPS, SSD_WIDTH // SSD_GROUPS))
        return rmsnorm(g, norm_g.reshape(SSD_GROUPS, SSD_WIDTH // SSD_GROUPS)).reshape(z.shape)

    xs_l, b_l, c_l, dt_l = prep(pl)
    xs_c, b_c, c_c, dt_c = prep(pc)
    y_l = d_skip[:, None] * xs_l
    y_c = d_skip[:, None] * xs_c if ctx_out else None
    for d, rev in enumerate((False, True)):
        a_head = -jnp.exp(a_log[d].astype(jnp.float32))
        yc_d, hc_final = ssd_scan(xs_c, jax.nn.softplus(dt_c[:, :, d] + dt_bias[d]), a_head, b_c, c_c,
                                  None, rev, ctx_out)
        yl_d, _ = ssd_scan(xs_l, jax.nn.softplus(dt_l[:, :, d] + dt_bias[d]), a_head, b_l, c_l,
                           hc_final, rev, True)
        y_l = y_l + yl_d
        if ctx_out:
            y_c = y_c + yc_d
    yl = gated_norm(y_l, pl)
    yc = gated_norm(y_c, pc) if ctx_out else None
    return yl, yc


def merge_branches(p, branches, w_up, w_o):
    gates = jax.nn.sigmoid(p[..., MIX_COLS:].reshape(p.shape[:2] + (N_BRANCH, D_MODEL)))
    up = jnp.einsum('btnw,nwd->btnd', jnp.stack(branches, axis=2), w_up)
    return jnp.sum(gates * up, axis=2) @ w_o


def moe_ffn(h, router_w, router_b, w_gate, w_up, w_down):
    shp = h.shape
    t = h.reshape(-1, D_MODEL)
    score = jax.nn.sigmoid((t @ router_w).astype(jnp.float32))
    sel = (score + router_b).reshape(-1, N_EXPERT_GROUPS, EXPERTS_PER_GROUP)
    group_score = jnp.sum(lax.top_k(sel, TOP_K)[0], axis=-1)
    in_group = jnp.argmax(group_score, axis=-1)[:, None] == jnp.arange(N_EXPERT_GROUPS)
    masked = jnp.where(in_group[..., None], sel, -jnp.inf).reshape(-1, N_EXPERTS)
    _, idx = lax.top_k(masked, TOP_K)
    w = jnp.take_along_axis(score, idx, axis=-1)
    w = w / jnp.sum(w, axis=-1, keepdims=True)
    combine = jnp.sum(jax.nn.one_hot(idx, N_EXPERTS, dtype=jnp.float32) * w[..., None], axis=1)
    hid = jax.nn.silu(jnp.einsum('td,edf->tef', t, w_gate)) * jnp.einsum('td,edf->tef', t, w_up)
    y = jnp.einsum('tef,efd->td', hid * combine[..., None].astype(hid.dtype), w_down)
    return y.reshape(shp)


def setup_inputs(seed: int = 0) -> dict:
    key = jax.random.key(seed)
    ks = iter(jax.random.split(key, 48))
    f32 = jnp.float32
    nl = DEPTH

    def normal(shape, scale):
        return jax.random.normal(next(ks), shape, f32) * scale

    def gain(shape):
        return 1.0 + normal(shape, 0.02)

    a_c = jax.random.uniform(next(ks), (nl, 2, RNN_WIDTH), f32, 0.9, 0.999)
    s_lam = a_c ** (1.0 / LRU_C)
    rnn_lambda = jnp.log(s_lam) - jnp.log1p(-s_lam)
    dt0 = jnp.exp(jax.random.uniform(next(ks), (nl, 2, SSD_HEADS), f32, math.log(1e-3), math.log(1e-1)))
    ssd_dt_bias = dt0 + jnp.log(-jnp.expm1(-dt0))
    ssd_a_log = jnp.log(jax.random.uniform(next(ks), (nl, 2, SSD_HEADS), f32, 1.0, 16.0))

    return {
        'x': normal((BATCH, SEQ, D_MODEL), 1.0),
        'c': normal((BATCH, D_MODEL), 1.0),
        'ctx': normal((BATCH, CTX_LEN, D_MODEL), 1.0),
        'c_ctx': normal((D_MODEL,), 1.0),
        'w_mod_a': normal((nl, D_MODEL, MOD_RANK), D_MODEL ** -0.5),
        'w_mod_b': normal((nl, MOD_RANK, N_MOD * D_MODEL), 0.3 * MOD_RANK ** -0.5),
        'b_mod': normal((nl, N_MOD * D_MODEL), 0.02),
        'g_mix': gain((nl, D_MODEL)),
        'g_ffn': gain((nl, D_MODEL)),
        'w_in': normal((nl, D_MODEL, IN_COLS), D_MODEL ** -0.5),
        'w_up': normal((nl, N_BRANCH, BRANCH_WIDTH, D_MODEL), BRANCH_WIDTH ** -0.5),
        'w_o': normal((nl, D_MODEL, D_MODEL), D_MODEL ** -0.5),
        'q_norm': gain((nl, HEAD_DIM)),
        'k_norm': gain((nl, HEAD_DIM)),
        'rnn_conv_w': normal((nl, CONV_W, RNN_WIDTH), CONV_W ** -0.5),
        'rnn_conv_b': normal((nl, RNN_WIDTH), 0.02),
        'rnn_lambda': rnn_lambda,
        'rnn_w_r': normal((nl, 2, RNN_BLOCKS, RNN_BLOCK_DIM, RNN_BLOCK_DIM), RNN_BLOCK_DIM ** -0.5),
        'rnn_b_r': normal((nl, 2, RNN_WIDTH), 0.1),
        'rnn_w_i': normal((nl, 2, RNN_BLOCKS, RNN_BLOCK_DIM, RNN_BLOCK_DIM), RNN_BLOCK_DIM ** -0.5),
        'rnn_b_i': normal((nl, 2, RNN_WIDTH), 0.1),
        'ssd_conv_w': normal((nl, CONV_W, SSD_XBC), CONV_W ** -0.5),
        'ssd_conv_b': normal((nl, SSD_XBC), 0.02),
        'ssd_dt_bias': ssd_dt_bias,
        'ssd_a_log': ssd_a_log,
        'ssd_d': 1.0 + normal((nl, SSD_HEADS), 0.1),
        'ssd_norm': gain((nl, SSD_WIDTH)),
        'router_w': normal((D_MODEL, N_EXPERTS), D_MODEL ** -0.5),
        'router_b': normal((N_EXPERTS,), 0.01),
        'moe_w_gate': normal((nl, N_EXPERTS, D_MODEL, EXPERT_FF), D_MODEL ** -0.5),
        'moe_w_up': normal((nl, N_EXPERTS, D_MODEL, EXPERT_FF), D_MODEL ** -0.5),
        'moe_w_down': normal((nl, N_EXPERTS, EXPERT_FF, D_MODEL), EXPERT_FF ** -0.5),
        'g_final': gain((D_MODEL,)),
    }


def reference(x, c, ctx, c_ctx, w_mod_a, w_mod_b, b_mod, g_mix, g_ffn, w_in, w_up, w_o, q_norm, k_norm,
              rnn_conv_w, rnn_conv_b, rnn_lambda, rnn_w_r, rnn_b_r, rnn_w_i, rnn_b_i,
              ssd_conv_w, ssd_conv_b, ssd_dt_bias, ssd_a_log, ssd_d, ssd_norm,
              router_w, router_b, moe_w_gate, moe_w_up, moe_w_down, g_final):
    cos, sin = axial_rope_tables(x.shape[1], x.dtype)
    xl, xc = x, ctx
    for l in range(DEPTH):
        ctx_out = l < DEPTH - 1
        mod_l = adaln(c, w_mod_a[l], w_mod_b[l], b_mod[l])
        mod_c = adaln(c_ctx[None], w_mod_a[l], w_mod_b[l], b_mod[l])

        hl = modulate(rmsnorm(xl, g_mix[l]), mod_l, 0)
        hc = modulate(rmsnorm(xc, g_mix[l]), mod_c, 0)
        pl = hl @ w_in[l]
        pc = hc @ (w_in[l] if ctx_out else w_in[l][:, :MIX_COLS])
        ya_l, ya_c = gqa_branch(pl, pc, q_norm[l], k_norm[l], cos, sin, ctx_out)
        yr_l, yr_c = rglru_branch(pl, pc, rnn_conv_w[l], rnn_conv_b[l], rnn_lambda[l], rnn_w_r[l],
                                  rnn_b_r[l], rnn_w_i[l], rnn_b_i[l], ctx_out)
        ys_l, ys_c = ssd_branch(pl, pc, ssd_conv_w[l], ssd_conv_b[l], ssd_dt_bias[l], ssd_a_log[l],
                                ssd_d[l], ssd_norm[l], ctx_out)
        xl = xl + mod_l[..., 2, :] * merge_branches(pl, (ya_l, yr_l, ys_l), w_up[l], w_o[l])

        hl2 = modulate(rmsnorm(xl, g_ffn[l]), mod_l, 3)
        if ctx_out:
            xc = xc + mod_c[..., 2, :] * merge_branches(pc, (ya_c, yr_c, ys_c), w_up[l], w_o[l])
            hc2 = modulate(rmsnorm(xc, g_ffn[l]), mod_c, 3)
            n_lat = hl2.shape[1]
            y2 = moe_ffn(jnp.concatenate([hl2, hc2], axis=1), router_w, router_b,
                         moe_w_gate[l], moe_w_up[l], moe_w_down[l])
            xl = xl + mod_l[..., 5, :] * y2[:, :n_lat]
            xc = xc + mod_c[..., 5, :] * y2[:, n_lat:]
        else:
            xl = xl + mod_l[..., 5, :] * moe_ffn(hl2, router_w, router_b,
                                                 moe_w_gate[l], moe_w_up[l], moe_w_down[l])
    return rmsnorm(xl, g_final)
```

```python
import functools
from typing import NamedTuple

import jax
import jax.numpy as jnp
from jax import lax
from jax.experimental import pallas as pl
from jax.experimental.pallas import tpu as pltpu

F32 = jnp.float32
BF16 = jnp.bfloat16

D_MODEL = 4096
DEPTH = 4
GRID_W = 64
N_BRANCH = 3
BRANCH_WIDTH = 1024
HEAD_DIM = 128
N_Q_HEADS = 8
N_KV_HEADS = 2
Q_PER_KV = 4
KV_WIDTH = 256
N_ROPE_FREQ = 32
ROPE_THETA = 10000.0
RNN_WIDTH = 1024
RNN_BLOCKS = 8
RNN_BLOCK_DIM = 128
LRU_C = 8.0
CONV_W = 4
SSD_WIDTH = 1024
SSD_HEAD_DIM = 64
SSD_HEADS = 16
SSD_GROUPS = 2
SSD_STATE = 128
SSD_CHUNK = 128
SSD_BC = 256
SSD_XBC = 1536
HEADS_PER_GROUP = SSD_HEADS // SSD_GROUPS
GROUP_WIDTH = SSD_WIDTH // SSD_GROUPS
Q_OFF = 0
K_OFF = 1024
V_OFF = 1280
RX_OFF = 1536
RG_OFF = 2560
SZ_OFF = 3584
SX_OFF = 4608
SDT_OFF = 6144
MIX_COLS = 6176
MOD_RANK = 256
N_MOD = 6
N_EXPERTS = 16
N_EXPERT_GROUPS = 4
EXPERTS_PER_GROUP = 4
EXPERT_FF = 512
EPS = 1e-6

LANES = 128
SUBLANES = 8
VMEM_LIMIT_BYTES = 56 * 1024 * 1024
DT_BLOCK0 = SDT_OFF // LANES
P_COLS = SDT_OFF + SSD_GROUPS * LANES
ROW_TILE = 256
MM_TILE_M = 512
NEG_BIG = -1e30


class Dims(NamedTuple):
    batch: int
    seq: int
    ctx: int

    @property
    def n_lat(self):
        return self.batch * self.seq

    @property
    def n_tok(self):
        return self.batch * (self.seq + self.ctx)


def _cparams(sem):
    return pltpu.CompilerParams(dimension_semantics=sem, vmem_limit_bytes=VMEM_LIMIT_BYTES)


def _sigmoid(x):
    return 1.0 / (1.0 + jnp.exp(-x))


def _silu(x):
    return x * _sigmoid(x)


def _softplus(x):
    return jnp.maximum(x, 0.0) + jnp.log(1.0 + jnp.exp(-jnp.abs(x)))


def _gelu_tanh(x):
    return 0.5 * x * (1.0 + jnp.tanh(0.7978845608028654 * (x + 0.044715 * (x * x * x))))


def _dot(a, b):
    return jnp.dot(a, b, preferred_element_type=F32)


def _dot_nt(a, b):
    return lax.dot_general(a, b, (((1,), (1,)), ((), ())), preferred_element_type=F32)


def _mod_row(dims, tile, k):
    n_lat_tiles = dims.n_lat // tile
    per_batch = dims.seq // tile

    def f(i):
        return jnp.where(i < n_lat_tiles, 1 + i // per_batch, 0) * N_MOD + k
    return f


def _seg_block(dims, rows):
    lat_pb = dims.seq // rows
    ctx_pb = dims.ctx // rows
    ctx_base = dims.n_lat // rows

    def f(b, r):
        return jnp.where(r < lat_pb, b * lat_pb + r, ctx_base + b * ctx_pb + (r - lat_pb))
    return f


def _adaln_a_kernel(c_ref, w_ref, o_ref):
    c = c_ref[...]
    o_ref[0] = _dot(_silu(c).astype(BF16), w_ref[0].astype(BF16))


def _adaln_b_kernel(a_ref, w_ref, b_ref, o_ref):
    o_ref[0] = _dot(a_ref[0].astype(BF16), w_ref[0].astype(BF16)) + b_ref[0]


def adaln_all(cond, w_a, w_b, b):
    nl = w_a.shape[0]
    a = pl.pallas_call(
        _adaln_a_kernel,
        grid=(nl,),
        in_specs=[pl.BlockSpec((SUBLANES, D_MODEL), lambda l: (0, 0)),
                  pl.BlockSpec((1, D_MODEL, MOD_RANK), lambda l: (l, 0, 0))],
        out_specs=pl.BlockSpec((1, SUBLANES, MOD_RANK), lambda l: (l, 0, 0)),
        out_shape=jax.ShapeDtypeStruct((nl, SUBLANES, MOD_RANK), F32),
        compiler_params=_cparams(("arbitrary",)),
        name="adaln_a",
    )(cond, w_a)
    tn = 4096
    ncol = N_MOD * D_MODEL
    return pl.pallas_call(
        _adaln_b_kernel,
        grid=(nl, ncol // tn),
        in_specs=[pl.BlockSpec((1, SUBLANES, MOD_RANK), lambda l, j: (l, 0, 0)),
                  pl.BlockSpec((1, MOD_RANK, tn), lambda l, j: (l, 0, j)),
                  pl.BlockSpec((1, 1, tn), lambda l, j: (l, 0, j))],
        out_specs=pl.BlockSpec((1, SUBLANES, tn), lambda l, j: (l, 0, j)),
        out_shape=jax.ShapeDtypeStruct((nl, SUBLANES, ncol), F32),
        compiler_params=_cparams(("arbitrary", "arbitrary")),
        name="adaln_b",
    )(a, w_b, b.reshape(nl, 1, ncol))


def _norm_mod_kernel(x_ref, g_ref, sh_ref, sc_ref, o_ref):
    x = x_ref[...]
    y = x * lax.rsqrt(jnp.mean(x * x, axis=-1, keepdims=True) + EPS) * g_ref[...]
    o_ref[...] = (y * (1.0 + sc_ref[0]) + sh_ref[0]).astype(o_ref.dtype)


def _norm_mod_router_kernel(x_ref, g_ref, sh_ref, sc_ref, rw_ref, o_ref, lg_ref):
    x = x_ref[...]
    y = x * lax.rsqrt(jnp.mean(x * x, axis=-1, keepdims=True) + EPS) * g_ref[...]
    h = (y * (1.0 + sc_ref[0]) + sh_ref[0]).astype(o_ref.dtype)
    o_ref[...] = h
    lg_ref[...] = _dot_nt(rw_ref[...], h)


def norm_mod(dims, x, gain, modtab, k, m_rows, router_wt=None):
    tm = ROW_TILE
    mrow = _mod_row(dims, tm, k)
    mrow1 = _mod_row(dims, tm, k + 1)
    in_specs = [pl.BlockSpec((tm, D_MODEL), lambda i: (i, 0)),
                pl.BlockSpec((1, D_MODEL), lambda i: (0, 0)),
                pl.BlockSpec((1, 1, D_MODEL), lambda i: (mrow(i), 0, 0)),
                pl.BlockSpec((1, 1, D_MODEL), lambda i: (mrow1(i), 0, 0))]
    h_spec = pl.BlockSpec((tm, D_MODEL), lambda i: (i, 0))
    h_shape = jax.ShapeDtypeStruct((m_rows, D_MODEL), BF16)
    if router_wt is None:
        return pl.pallas_call(
            _norm_mod_kernel, grid=(m_rows // tm,), in_specs=in_specs, out_specs=h_spec,
            out_shape=h_shape, compiler_params=_cparams(("arbitrary",)), name="norm_mod",
        )(x, gain.reshape(1, D_MODEL), modtab, modtab)
    return pl.pallas_call(
        _norm_mod_router_kernel, grid=(m_rows // tm,),
        in_specs=in_specs + [pl.BlockSpec((N_EXPERTS, D_MODEL), lambda i: (0, 0))],
        out_specs=[h_spec, pl.BlockSpec((N_EXPERTS, tm), lambda i: (0, i))],
        out_shape=[h_shape, jax.ShapeDtypeStruct((N_EXPERTS, m_rows), F32)],
        compiler_params=_cparams(("arbitrary",)), name="norm_mod_router",
    )(x, gain.reshape(1, D_MODEL), modtab, modtab, router_wt)


def _final_norm_kernel(x_ref, g_ref, o_ref):
    x = x_ref[...]
    o_ref[...] = x * lax.rsqrt(jnp.mean(x * x, axis=-1, keepdims=True) + EPS) * g_ref[...]


def final_norm(x, gain, m_rows):
    tm = ROW_TILE
    return pl.pallas_call(
        _final_norm_kernel, grid=(m_rows // tm,),
        in_specs=[pl.BlockSpec((tm, D_MODEL), lambda i: (i, 0)),
                  pl.BlockSpec((1, D_MODEL), lambda i: (0, 0))],
        out_specs=pl.BlockSpec((tm, D_MODEL), lambda i: (i, 0)),
        out_shape=jax.ShapeDtypeStruct((m_rows, D_MODEL), F32),
        compiler_params=_cparams(("arbitrary",)), name="final_norm",
    )(x, gain.reshape(1, D_MODEL))


def _mm_kernel(a_ref, w_ref, o_ref):
    o_ref[...] = _dot(a_ref[...], w_ref[...]).astype(o_ref.dtype)


def matmul(a, w, tn, out_dtype):
    m, k = a.shape
    n = w.shape[1]
    tm = MM_TILE_M
    return pl.pallas_call(
        _mm_kernel, grid=(n // tn, m // tm),
        in_specs=[pl.BlockSpec((tm, k), lambda j, i: (i, 0)),
                  pl.BlockSpec((k, tn), lambda j, i: (0, j))],
        out_specs=pl.BlockSpec((tm, tn), lambda j, i: (i, j)),
        out_shape=jax.ShapeDtypeStruct((m, n), out_dtype),
        compiler_params=_cparams(("arbitrary", "arbitrary")), name="mm_in",
    )(a, w)


def _mm_res_kernel(a_ref, w_ref, x_ref, g_ref, o_ref):
    o_ref[...] = x_ref[...] + g_ref[0] * _dot(a_ref[...], w_ref[...])


def matmul_residual(dims, a, w, x, modtab, k, m_rows):
    kk = a.shape[1]
    n = w.shape[1]
    tm, tn = MM_TILE_M, 512
    mrow = _mod_row(dims, tm, k)
    return pl.pallas_call(
        _mm_res_kernel, grid=(n // tn, m_rows // tm),
        in_specs=[pl.BlockSpec((tm, kk), lambda j, i: (i, 0)),
                  pl.BlockSpec((kk, tn), lambda j, i: (0, j)),
                  pl.BlockSpec((tm, tn), lambda j, i: (i, j)),
                  pl.BlockSpec((1, 1, tn), lambda j, i: (mrow(i), 0, j))],
        out_specs=pl.BlockSpec((tm, tn), lambda j, i: (i, j)),
        out_shape=jax.ShapeDtypeStruct((m_rows, n), F32),
        compiler_params=_cparams(("arbitrary", "arbitrary")), name="mm_out_res",
    )(a, w, x, modtab)


def _merge_kernel(h_ref, ya_ref, yr_ref, ys_ref, wg0_ref, wg1_ref, wg2_ref,
                  wu0_ref, wu1_ref, wu2_ref, o_ref):
    h = h_ref[...]
    acc = _sigmoid(_dot(h, wg0_ref[...])) * _dot(ya_ref[...], wu0_ref[0])
    acc += _sigmoid(_dot(h, wg1_ref[...])) * _dot(yr_ref[...], wu1_ref[0])
    acc += _sigmoid(_dot(h, wg2_ref[...])) * _dot(ys_ref[...], wu2_ref[0])
    o_ref[...] = acc.astype(o_ref.dtype)


def merge_branches(h, ya, yr, ys, w_gate, w_up, m_rows):
    tm, tn = MM_TILE_M, 256
    nj = D_MODEL // tn
    y_spec = pl.BlockSpec((tm, BRANCH_WIDTH), lambda j, i: (i, 0))

    def wg_spec(n):
        return pl.BlockSpec((D_MODEL, tn), lambda j, i: (0, n * nj + j))

    def wu_spec(n):
        return pl.BlockSpec((1, BRANCH_WIDTH, tn), lambda j, i: (n, 0, j))

    return pl.pallas_call(
        _merge_kernel, grid=(nj, m_rows // tm),
        in_specs=[pl.BlockSpec((tm, D_MODEL), lambda j, i: (i, 0)), y_spec, y_spec, y_spec,
                  wg_spec(0), wg_spec(1), wg_spec(2), wu_spec(0), wu_spec(1), wu_spec(2)],
        out_specs=pl.BlockSpec((tm, tn), lambda j, i: (i, j)),
        out_shape=jax.ShapeDtypeStruct((m_rows, D_MODEL), BF16),
        compiler_params=_cparams(("arbitrary", "arbitrary")), name="merge",
    )(h, ya, yr, ys, w_gate, w_gate, w_gate, w_up, w_up, w_up)


def _residual_kernel(x_ref, y_ref, g_ref, o_ref):
    o_ref[...] = x_ref[...] + g_ref[0] * y_ref[...]


def residual(dims, x, y, modtab, k, m_rows):
    tm = ROW_TILE
    mrow = _mod_row(dims, tm, k)
    return pl.pallas_call(
        _residual_kernel, grid=(m_rows // tm,),
        in_specs=[pl.BlockSpec((tm, D_MODEL), lambda i: (i, 0)),
                  pl.BlockSpec((tm, D_MODEL), lambda i: (i, 0)),
                  pl.BlockSpec((1, 1, D_MODEL), lambda i: (mrow(i), 0, 0))],
        out_specs=pl.BlockSpec((tm, D_MODEL), lambda i: (i, 0)),
        out_shape=jax.ShapeDtypeStruct((m_rows, D_MODEL), F32),
        compiler_params=_cparams(("arbitrary",)), name="residual",
    )(x, y, modtab)


def _qkv_prep_kernel(q_ref, k_ref, v_ref, c_ref, s0_ref, s1_ref, qg_ref, kg_ref,
                     qo_ref, ko_ref, vo_ref):
    c, s0, s1 = c_ref[...], s0_ref[...], s1_ref[...]

    def head(xh, gain, scale):
        y = xh * lax.rsqrt(jnp.mean(xh * xh, axis=-1, keepdims=True) + EPS) * gain
        r = y * c + pltpu.roll(y, 96, 1) * s0 + pltpu.roll(y, 32, 1) * s1
        return r * scale

    for hh in range(N_Q_HEADS):
        sl = slice(hh * HEAD_DIM, (hh + 1) * HEAD_DIM)
        qo_ref[:, sl] = head(q_ref[:, sl], qg_ref[...], HEAD_DIM ** -0.5).astype(qo_ref.dtype)
    for hh in range(N_KV_HEADS):
        sl = slice(hh * HEAD_DIM, (hh + 1) * HEAD_DIM)
        ko_ref[:, sl] = head(k_ref[:, sl], kg_ref[...], 1.0).astype(ko_ref.dtype)
    vo_ref[...] = v_ref[...].astype(vo_ref.dtype)


def qkv_prep(dims, p, rope_c, rope_s0, rope_s1, q_gain, k_gain):
    tm = ROW_TILE
    n_lat_tiles = dims.n_lat // tm
    per_batch = dims.seq // tm

    def tab(i):
        return (jnp.where(i < n_lat_tiles, i % per_batch, per_batch), 0)

    m = dims.n_tok
    tab_spec = pl.BlockSpec((tm, HEAD_DIM), tab)
    g_spec = pl.BlockSpec((1, HEAD_DIM), lambda i: (0, 0))
    return pl.pallas_call(
        _qkv_prep_kernel, grid=(m // tm,),
        in_specs=[pl.BlockSpec((tm, BRANCH_WIDTH), lambda i: (i, Q_OFF // BRANCH_WIDTH)),
                  pl.BlockSpec((tm, KV_WIDTH), lambda i: (i, K_OFF // KV_WIDTH)),
                  pl.BlockSpec((tm, KV_WIDTH), lambda i: (i, V_OFF // KV_WIDTH)),
                  tab_spec, tab_spec, tab_spec, g_spec, g_spec],
        out_specs=[pl.BlockSpec((tm, BRANCH_WIDTH), lambda i: (i, 0)),
                   pl.BlockSpec((tm, KV_WIDTH), lambda i: (i, 0)),
                   pl.BlockSpec((tm, KV_WIDTH), lambda i: (i, 0))],
        out_shape=[jax.ShapeDtypeStruct((m, BRANCH_WIDTH), BF16),
                   jax.ShapeDtypeStruct((m, KV_WIDTH), BF16),
                   jax.ShapeDtypeStruct((m, KV_WIDTH), BF16)],
        compiler_params=_cparams(("arbitrary",)), name="qkv_prep",
    )(p, p, p, rope_c, rope_s0, rope_s1, q_gain.reshape(1, HEAD_DIM), k_gain.reshape(1, HEAD_DIM))


def _attn_kernel(q_ref, kl_ref, kc_ref, vl_ref, vc_ref, o_ref, *, lat_tiles):
    qi = pl.program_id(2)

    @pl.when(qi < lat_tiles)
    def _():
        for g in range(Q_PER_KV):
            sl = slice(g * HEAD_DIM, (g + 1) * HEAD_DIM)
            q = q_ref[:, sl]
            s_l = _dot_nt(q, kl_ref[...])
            s_c = _dot_nt(q, kc_ref[...])
            mx = jnp.maximum(jnp.max(s_l, axis=-1, keepdims=True), jnp.max(s_c, axis=-1, keepdims=True))
            p_l = jnp.exp(s_l - mx)
            p_c = jnp.exp(s_c - mx)
            den = jnp.sum(p_l, axis=-1, keepdims=True) + jnp.sum(p_c, axis=-1, keepdims=True)
            o = _dot(p_l.astype(BF16), vl_ref[...]) + _dot(p_c.astype(BF16), vc_ref[...])
            o_ref[:, sl] = (o / den).astype(o_ref.dtype)

    @pl.when(qi >= lat_tiles)
    def _():
        for g in range(Q_PER_KV):
            sl = slice(g * HEAD_DIM, (g + 1) * HEAD_DIM)
            s_c = _dot_nt(q_ref[:, sl], kc_ref[...])
            p_c = jnp.exp(s_c - jnp.max(s_c, axis=-1, keepdims=True))
            den = jnp.sum(p_c, axis=-1, keepdims=True)
            o_ref[:, sl] = (_dot(p_c.astype(BF16), vc_ref[...]) / den).astype(o_ref.dtype)


def attention(dims, qn, kn, vn):
    tq = ROW_TILE
    assert dims.ctx == tq
    lat_tiles = dims.seq // tq
    seg = _seg_block(dims, tq)
    ctx_blk = dims.n_lat // dims.ctx
    gw = Q_PER_KV * HEAD_DIM
    return pl.pallas_call(
        functools.partial(_attn_kernel, lat_tiles=lat_tiles),
        grid=(dims.batch, N_KV_HEADS, lat_tiles + 1),
        in_specs=[pl.BlockSpec((tq, gw), lambda b, h, r: (seg(b, r), h)),
                  pl.BlockSpec((dims.seq, HEAD_DIM), lambda b, h, r: (b, h)),
                  pl.BlockSpec((dims.ctx, HEAD_DIM), lambda b, h, r: (ctx_blk + b, h)),
                  pl.BlockSpec((dims.seq, HEAD_DIM), lambda b, h, r: (b, h)),
                  pl.BlockSpec((dims.ctx, HEAD_DIM), lambda b, h, r: (ctx_blk + b, h))],
        out_specs=pl.BlockSpec((tq, gw), lambda b, h, r: (seg(b, r), h)),
        out_shape=jax.ShapeDtypeStruct((dims.n_tok, BRANCH_WIDTH), BF16),
        compiler_params=_cparams(("arbitrary", "arbitrary", "arbitrary")), name="attention",
    )(qn, kn, kn, vn, vn)


def _conv_rows(prev8, cur, next8, w_ref, b_ref):
    rows = cur.shape[0]
    ext = jnp.concatenate([prev8, cur, next8], axis=0)
    y = b_ref[...] + w_ref[0:1, :] * ext[7:7 + rows]
    y = y + w_ref[1:2, :] * cur
    y = y + w_ref[2:3, :] * ext[9:9 + rows]
    y = y + w_ref[3:4, :] * ext[10:10 + rows]
    return y


def _conv_chunk(src_ref, r0, seg_len, rows, w_ref, b_ref):
    cur = src_ref[pl.ds(r0, rows), :]
    p0 = pl.multiple_of(jnp.maximum(r0 - SUBLANES, 0), SUBLANES)
    n0 = pl.multiple_of(jnp.minimum(r0 + rows, seg_len - SUBLANES), SUBLANES)
    prev8 = src_ref[pl.ds(p0, SUBLANES), :] * jnp.where(r0 > 0, 1.0, 0.0)
    next8 = src_ref[pl.ds(n0, SUBLANES), :] * jnp.where(r0 + rows < seg_len, 1.0, 0.0)
    return _conv_rows(prev8, cur, next8, w_ref, b_ref)


def _rglru_kernel(rxl_ref, rxc_ref, rgl_ref, rgc_ref, cw_ref, cb_ref, wg_ref, bg_ref, lam_ref,
                  o_ref, x_s, af_s, uf_s, ab_s, ub_s, h_s, *, seq, ctx):
    r = pl.program_id(2)
    tot = seq + ctx
    rows = ROW_TILE
    w = RNN_BLOCK_DIM

    @pl.when(r == 0)
    def _():
        sp_f = _softplus(-lam_ref[0, :, 0:w])
        sp_b = _softplus(-lam_ref[0, :, w:2 * w])

        def gates(x, base):
            z = _dot(x.astype(BF16), wg_ref[0]) + bg_ref[0]
            sg = _sigmoid(z)
            for d, (sp, a_s, u_s) in enumerate(((sp_f, af_s, uf_s), (sp_b, ab_s, ub_s))):
                rg = sg[:, (2 * d) * w:(2 * d + 1) * w]
                ig = sg[:, (2 * d + 1) * w:(2 * d + 2) * w]
                a = jnp.exp(-LRU_C * rg * sp)
                a_s[pl.ds(base, rows), :] = a
                u_s[pl.ds(base, rows), :] = jnp.sqrt(1.0 - a * a) * (ig * x)

        for c0 in range(0, ctx, rows):
            gates(_conv_chunk(rxc_ref, c0, ctx, rows, cw_ref, cb_ref), c0)

        def lat_body(c, carry):
            r0 = pl.multiple_of(c * rows, rows)
            gates(_conv_chunk(rxl_ref, r0, seq, rows, cw_ref, cb_ref), pl.multiple_of(ctx + r0, rows))
            return carry
        lax.fori_loop(0, seq // rows, lat_body, 0)

        row = lax.broadcasted_iota(jnp.int32, (SUBLANES, w), 0)

        def scan_group(g, carry, a_s, u_s, reverse, accumulate):
            base = pl.multiple_of(g * SUBLANES, SUBLANES)
            a = a_s[pl.ds(base, SUBLANES), :]
            u = u_s[pl.ds(base, SUBLANES), :]
            for k in (1, 2, 4):
                sh = SUBLANES - k if reverse else k
                m = (row < SUBLANES - k) if reverse else (row >= k)
                u = u + a * jnp.where(m, pltpu.roll(u, sh, 0), 0.0)
                a = a * jnp.where(m, pltpu.roll(a, sh, 0), 1.0)
            h = u + a * carry
            if accumulate:
                h_s[pl.ds(base, SUBLANES), :] = h_s[pl.ds(base, SUBLANES), :] + h
            else:
                h_s[pl.ds(base, SUBLANES), :] = h
            last = h[0:1, :] if reverse else h[SUBLANES - 1:SUBLANES, :]
            return jnp.broadcast_to(last, (SUBLANES, w))

        zero = jnp.zeros((SUBLANES, w), F32)
        ng_ctx, ng_tot = ctx // SUBLANES, tot // SUBLANES
        lax.fori_loop(0, ng_tot, lambda g, c: scan_group(g, c, af_s, uf_s, False, False), zero)
        cb = lax.fori_loop(0, ng_ctx, lambda i, c: scan_group(ng_ctx - 1 - i, c, ab_s, ub_s, True, True), zero)
        lax.fori_loop(0, ng_tot - ng_ctx, lambda i, c: scan_group(ng_tot - 1 - i, c, ab_s, ub_s, True, True), cb)

        for c0 in range(0, ctx, rows):
            h_s[pl.ds(c0, rows), :] = _gelu_tanh(rgc_ref[pl.ds(c0, rows), :]) * h_s[pl.ds(c0, rows), :]

        def out_body(c, carry):
            r0 = pl.multiple_of(c * rows, rows)
            hb = pl.multiple_of(ctx + r0, rows)
            h_s[pl.ds(hb, rows), :] = _gelu_tanh(rgl_ref[pl.ds(r0, rows), :]) * h_s[pl.ds(hb, rows), :]
            return carry
        lax.fori_loop(0, seq // rows, out_body, 0)

    lat_pieces = seq // rows
    src = jnp.where(r < lat_pieces, ctx + r * rows, (r - lat_pieces) * rows)
    o_ref[...] = h_s[pl.ds(pl.multiple_of(src, rows), rows), :].astype(o_ref.dtype)


def rglru_branch(dims, p, conv_w, conv_b, w_gates, b_gates, lam):
    rows = ROW_TILE
    w = RNN_BLOCK_DIM
    pieces = (dims.seq + dims.ctx) // rows
    seg = _seg_block(dims, rows)
    ctx_blk = dims.n_lat // dims.ctx
    rx0, rg0 = RX_OFF // w, RG_OFF // w
    tot = dims.seq + dims.ctx
    return pl.pallas_call(
        functools.partial(_rglru_kernel, seq=dims.seq, ctx=dims.ctx),
        grid=(dims.batch, RNN_BLOCKS, pieces),
        in_specs=[pl.BlockSpec((dims.seq, w), lambda b, n, r: (b, rx0 + n)),
                  pl.BlockSpec((dims.ctx, w), lambda b, n, r: (ctx_blk + b, rx0 + n)),
                  pl.BlockSpec((dims.seq, w), lambda b, n, r: (b, rg0 + n)),
                  pl.BlockSpec((dims.ctx, w), lambda b, n, r: (ctx_blk + b, rg0 + n)),
                  pl.BlockSpec((CONV_W, w), lambda b, n, r: (0, n)),
                  pl.BlockSpec((1, w), lambda b, n, r: (0, n)),
                  pl.BlockSpec((1, w, 4 * w), lambda b, n, r: (n, 0, 0)),
                  pl.BlockSpec((1, 1, 4 * w), lambda b, n, r: (n, 0, 0)),
                  pl.BlockSpec((1, 1, 2 * w), lambda b, n, r: (n, 0, 0))],
        out_specs=pl.BlockSpec((rows, w), lambda b, n, r: (seg(b, r), n)),
        out_shape=jax.ShapeDtypeStruct((dims.n_tok, RNN_WIDTH), BF16),
        scratch_shapes=[pltpu.VMEM((tot, w), F32) for _ in range(6)],
        compiler_params=_cparams(("arbitrary", "arbitrary", "arbitrary")), name="rglru",
    )(p, p, p, p, conv_w, conv_b.reshape(1, RNN_WIDTH), w_gates, b_gates, lam)


def _ssd_prep_kernel(cur_ref, prev_ref, next_ref, w_ref, b_ref, o_ref, *, n_lat_tiles, per_batch):
    i = pl.program_id(0)
    j = i % per_batch
    lat = i < n_lat_tiles
    pv = jnp.where(jnp.logical_and(lat, j != 0), 1.0, 0.0)
    nv = jnp.where(jnp.logical_and(lat, j != per_batch - 1), 1.0, 0.0)
    y = _conv_rows(prev_ref[...] * pv, cur_ref[...], next_ref[...] * nv, w_ref, b_ref)
    o_ref[...] = _silu(y)


def ssd_prep(dims, p, conv_w, conv_b):
    tm = ROW_TILE
    assert dims.ctx == tm
    m = dims.n_tok
    hb = tm // SUBLANES
    last8 = m // SUBLANES - 1
    c0 = SX_OFF // SSD_XBC
    return pl.pallas_call(
        functools.partial(_ssd_prep_kernel, n_lat_tiles=dims.n_lat // tm, per_batch=dims.seq // tm),
        grid=(m // tm,),
        in_specs=[pl.BlockSpec((tm, SSD_XBC), lambda i: (i, c0)),
                  pl.BlockSpec((SUBLANES, SSD_XBC), lambda i: (jnp.maximum(i * hb - 1, 0), c0)),
                  pl.BlockSpec((SUBLANES, SSD_XBC), lambda i: (jnp.minimum((i + 1) * hb, last8), c0)),
                  pl.BlockSpec((CONV_W, SSD_XBC), lambda i: (0, 0)),
                  pl.BlockSpec((1, SSD_XBC), lambda i: (0, 0))],
        out_specs=pl.BlockSpec((tm, SSD_XBC), lambda i: (i, 0)),
        out_shape=jax.ShapeDtypeStruct((m, SSD_XBC), F32),
        compiler_params=_cparams(("arbitrary",)), name="ssd_prep",
    )(p, p, p, conv_w, conv_b.reshape(1, SSD_XBC))


def _ssd_chunk(x_ref, b_ref, c_ref, dt_ref, dtb_ref, alog_ref, s_ref, y_ref, reverse):
    q = SSD_CHUNK
    col0 = HEADS_PER_GROUP if reverse else 0
    ri = lax.broadcasted_iota(jnp.int32, (q, q), 0)
    ci = lax.broadcasted_iota(jnp.int32, (q, q), 1)
    tri = (ri <= ci) if reverse else (ri >= ci)
    cum = jnp.where(tri, 1.0, 0.0).astype(BF16)
    ones = jnp.ones((q, q), BF16)
    left = ci < SSD_HEAD_DIM
    top = ri < SSD_HEAD_DIM

    dtc = _softplus(dt_ref[...] + dtb_ref[0])
    a = dtc * (-jnp.exp(alog_ref[0]))
    a_hi = a.astype(BF16)
    r1 = a - a_hi.astype(F32)
    a_mid = r1.astype(BF16)
    a_lo = (r1 - a_mid.astype(F32)).astype(BF16)
    acum = _dot(cum, a_hi) + _dot(cum, a_mid) + _dot(cum, a_lo)
    atot = _dot(ones, a_hi) + _dot(ones, a_mid) + _dot(ones, a_lo)
    acum_t = acum.T

    bm = b_ref[...].astype(BF16)
    cm = c_ref[...].astype(BF16)
    cb = _dot_nt(cm, bm)

    for pair in range(HEADS_PER_GROUP // 2):
        ca = col0 + 2 * pair
        lanes = slice(pair * LANES, (pair + 1) * LANES)
        col_a, col_b = acum[:, ca:ca + 1], acum[:, ca + 1:ca + 2]
        row_a, row_b = acum_t[ca:ca + 1, :], acum_t[ca + 1:ca + 2, :]
        l_a = jnp.exp(jnp.where(tri, col_a - row_a, NEG_BIG))
        l_b = jnp.exp(jnp.where(tri, col_b - row_b, NEG_BIG))
        xdt = x_ref[:, lanes] * jnp.where(left, dtc[:, ca:ca + 1], dtc[:, ca + 1:ca + 2])
        xdt_b = xdt.astype(BF16)
        y_in = jnp.where(left, _dot((cb * l_a).astype(BF16), xdt_b), _dot((cb * l_b).astype(BF16), xdt_b))
        s_old = s_ref[lanes, :]
        y_st = _dot_nt(cm, s_old.astype(BF16)) * jnp.where(left, jnp.exp(col_a), jnp.exp(col_b))
        y_ref[:, lanes] = y_in + y_st
        tot_a, tot_b = atot[:, ca:ca + 1], atot[:, ca + 1:ca + 2]
        xw = xdt * jnp.where(left, jnp.exp(tot_a - col_a), jnp.exp(tot_b - col_b))
        s_new = _dot(xw.T.astype(BF16), bm)
        s_ref[lanes, :] = s_old * jnp.where(top, jnp.exp(tot_a), jnp.exp(tot_b)) + s_new


def _ssd_scan_kernel(xf_ref, bf_ref, cf_ref, dtf_ref, xb_ref, bb_ref, cb_ref, dtb_ref_,
                     bias_ref, alog_ref, yf_ref, yb_ref, sf_s, sb_s):
    @pl.when(pl.program_id(2) == 0)
    def _():
        sf_s[...] = jnp.zeros_like(sf_s)
        sb_s[...] = jnp.zeros_like(sb_s)

    _ssd_chunk(xf_ref, bf_ref, cf_ref, dtf_ref, bias_ref, alog_ref, sf_s, yf_ref, False)
    _ssd_chunk(xb_ref, bb_ref, cb_ref, dtb_ref_, bias_ref, alog_ref, sb_s, yb_ref, True)


def ssd_scan(dims, xbc, p, dt_bias, a_log):
    q = SSD_CHUNK
    lat_c, ctx_c = dims.seq // q, dims.ctx // q
    ctx_base = dims.n_lat // q
    steps = lat_c + ctx_c

    def cf(b, s):
        return jnp.where(s < ctx_c, ctx_base + b * ctx_c + s, b * lat_c + (s - ctx_c))

    def cbk(b, s):
        return jnp.where(s < ctx_c, ctx_base + b * ctx_c + (ctx_c - 1 - s),
                         b * lat_c + (lat_c - 1 - (s - ctx_c)))

    gw = GROUP_WIDTH
    bcol, ccol = SSD_WIDTH // SSD_STATE, (SSD_WIDTH + SSD_BC) // SSD_STATE

    def specs(cfun):
        return [pl.BlockSpec((q, gw), lambda b, g, s: (cfun(b, s), g)),
                pl.BlockSpec((q, SSD_STATE), lambda b, g, s: (cfun(b, s), bcol + g)),
                pl.BlockSpec((q, SSD_STATE), lambda b, g, s: (cfun(b, s), ccol + g)),
                pl.BlockSpec((q, LANES), lambda b, g, s: (cfun(b, s), DT_BLOCK0 + g))]

    par_spec = pl.BlockSpec((1, 1, LANES), lambda b, g, s: (g, 0, 0))
    yshape = jax.ShapeDtypeStruct((dims.n_tok, SSD_WIDTH), F32)
    return pl.pallas_call(
        _ssd_scan_kernel, grid=(dims.batch, SSD_GROUPS, steps),
        in_specs=specs(cf) + specs(cbk) + [par_spec, par_spec],
        out_specs=[pl.BlockSpec((q, gw), lambda b, g, s: (cf(b, s), g)),
                   pl.BlockSpec((q, gw), lambda b, g, s: (cbk(b, s), g))],
        out_shape=[yshape, yshape],
        scratch_shapes=[pltpu.VMEM((gw, SSD_STATE), F32), pltpu.VMEM((gw, SSD_STATE), F32)],
        compiler_params=_cparams(("arbitrary", "arbitrary", "arbitrary")), name="ssd_scan",
    )(xbc, xbc, xbc, p, xbc, xbc, xbc, p, dt_bias, a_log)


def _ssd_finish_kernel(yf_ref, yb_ref, x_ref, z_ref, d_ref, g_ref, o_ref):
    y = d_ref[...] * x_ref[...] + yf_ref[...] + yb_ref[...]
    gt = y * _silu(z_ref[...])
    o_ref[...] = (gt * lax.rsqrt(jnp.mean(gt * gt, axis=-1, keepdims=True) + EPS) * g_ref[...]).astype(o_ref.dtype)


def ssd_finish(dims, yf, yb, xbc, p, d_chan, norm_g):
    tm = ROW_TILE
    gw = GROUP_WIDTH
    z0 = SZ_OFF // gw
    blk = pl.BlockSpec((tm, gw), lambda i, g: (i, g))
    vec = pl.BlockSpec((1, gw), lambda i, g: (0, g))
    return pl.pallas_call(
        _ssd_finish_kernel, grid=(dims.n_tok // tm, SSD_GROUPS),
        in_specs=[blk, blk, blk, pl.BlockSpec((tm, gw), lambda i, g: (i, z0 + g)), vec, vec],
        out_specs=blk,
        out_shape=jax.ShapeDtypeStruct((dims.n_tok, SSD_WIDTH), BF16),
        compiler_params=_cparams(("arbitrary", "arbitrary")), name="ssd_finish",
    )(yf, yb, xbc, p, d_chan, norm_g.reshape(1, SSD_WIDTH))


def _route_kernel(lg_ref, rb_ref, o_ref):
    score = _sigmoid(lg_ref[...])
    sel = score + rb_ref[...]
    v = [sel[e:e + 1, :] for e in range(N_EXPERTS)]
    sc = [score[e:e + 1, :] for e in range(N_EXPERTS)]
    best, best_g = None, None
    for g in range(N_EXPERT_GROUPS):
        m = v[4 * g:4 * g + 4]
        gs = m[0] + m[1]
        for (i, j) in ((0, 2), (0, 3), (1, 2), (1, 3), (2, 3)):
            gs = jnp.maximum(gs, m[i] + m[j])
        if g == 0:
            best, best_g = gs, jnp.zeros_like(gs, dtype=jnp.int32)
        else:
            better = gs > best
            best_g = jnp.where(better, g, best_g)
            best = jnp.where(better, gs, best)
    wts = []
    for e in range(N_EXPERTS):
        g = e // EXPERTS_PER_GROUP
        rank = jnp.zeros_like(best_g)
        for k in range(4 * g, 4 * g + 4):
            if k == e:
                continue
            ahead = (v[k] >= v[e]) if k < e else (v[k] > v[e])
            rank = rank + jnp.where(ahead, 1, 0)
        chosen = jnp.logical_and(best_g == g, rank < 2)
        wts.append(jnp.where(chosen, sc[e], 0.0))
    tot = wts[0]
    for e in range(1, N_EXPERTS):
        tot = tot + wts[e]
    for e in range(N_EXPERTS):
        o_ref[e:e + 1, :] = wts[e] / tot


def route(logits_t, router_b):
    m = logits_t.shape[1]
    tn = m // 2
    return pl.pallas_call(
        _route_kernel, grid=(m // tn,),
        in_specs=[pl.BlockSpec((N_EXPERTS, tn), lambda i: (0, i)),
                  pl.BlockSpec((N_EXPERTS, 1), lambda i: (0, 0))],
        out_specs=pl.BlockSpec((N_EXPERTS, tn), lambda i: (0, i)),
        out_shape=jax.ShapeDtypeStruct((N_EXPERTS, m), F32),
        compiler_params=_cparams(("arbitrary",)), name="route",
    )(logits_t, router_b.reshape(N_EXPERTS, 1))


def _moe_dense_kernel(h_ref, c_ref, wg_ref, wu_ref, wd_ref, o_ref):
    e = pl.program_id(1)
    f = pl.program_id(2)

    @pl.when(jnp.logical_and(e == 0, f == 0))
    def _():
        o_ref[...] = jnp.zeros_like(o_ref)

    h = h_ref[...]
    g = _dot(h, wg_ref[0])
    u = _dot(h, wu_ref[0])
    c = c_ref[...]
    lane = lax.broadcasted_iota(jnp.int32, c.shape, 1)
    cw = jnp.sum(jnp.where(lane == e, c, 0.0), axis=-1, keepdims=True)
    hid = _silu(g) * u * cw
    o_ref[...] += _dot(hid.astype(BF16), wd_ref[0])


def moe_dense(h, combine, w_gate, w_up, w_down, m_rows):
    tm = MM_TILE_M
    tf = EXPERT_FF // 2
    return pl.pallas_call(
        _moe_dense_kernel, grid=(m_rows // tm, N_EXPERTS, EXPERT_FF // tf),
        in_specs=[pl.BlockSpec((tm, D_MODEL), lambda i, e, f: (i, 0)),
                  pl.BlockSpec((tm, N_EXPERTS), lambda i, e, f: (i, 0)),
                  pl.BlockSpec((1, D_MODEL, tf), lambda i, e, f: (e, 0, f)),
                  pl.BlockSpec((1, D_MODEL, tf), lambda i, e, f: (e, 0, f)),
                  pl.BlockSpec((1, tf, D_MODEL), lambda i, e, f: (e, f, 0))],
        out_specs=pl.BlockSpec((tm, D_MODEL), lambda i, e, f: (i, 0)),
        out_shape=jax.ShapeDtypeStruct((m_rows, D_MODEL), F32),
        compiler_params=_cparams(("arbitrary", "arbitrary", "arbitrary")), name="moe_dense",
    )(h, combine, w_gate, w_up, w_down)


def rope_tables(dims):
    s = dims.seq
    rows = s // GRID_W
    row = jnp.repeat(jnp.arange(rows, dtype=F32), GRID_W)
    col = jnp.tile(jnp.arange(GRID_W, dtype=F32), rows)
    inv = ROPE_THETA ** (-jnp.arange(N_ROPE_FREQ, dtype=F32) / N_ROPE_FREQ)
    ang = jnp.stack([row[:, None] * inv, col[:, None] * inv], axis=1)
    cos, sin = jnp.cos(ang), jnp.sin(ang)
    zero = jnp.zeros_like(sin)
    c = jnp.stack([cos, cos], axis=2).reshape(s, HEAD_DIM)
    s0 = jnp.stack([-sin, zero], axis=2).reshape(s, HEAD_DIM)
    s1 = jnp.stack([zero, sin], axis=2).reshape(s, HEAD_DIM)
    ident = jnp.ones((ROW_TILE, HEAD_DIM), F32)
    zpad = jnp.zeros((ROW_TILE, HEAD_DIM), F32)
    return (jnp.concatenate([c, ident], 0), jnp.concatenate([s0, zpad], 0), jnp.concatenate([s1, zpad], 0))


def _dt_cols():
    cols = []
    for g in range(SSD_GROUPS):
        for d in range(2):
            cols += [SDT_OFF + d * SSD_HEADS + g * HEADS_PER_GROUP + hh for hh in range(HEADS_PER_GROUP)]
    return cols


def _group_dt_param(v):
    out = []
    for g in range(SSD_GROUPS):
        hs = slice(g * HEADS_PER_GROUP, (g + 1) * HEADS_PER_GROUP)
        row = jnp.concatenate([v[0, hs], v[1, hs], jnp.zeros((LANES - 2 * HEADS_PER_GROUP,), v.dtype)])
        out.append(row.reshape(1, LANES))
    return jnp.stack(out, 0)


def _mixer_weight(w_in_l):
    pad = jnp.zeros((D_MODEL, LANES - 2 * HEADS_PER_GROUP), w_in_l.dtype)
    dtc = _dt_cols()
    parts = [w_in_l[:, :SDT_OFF]]
    for g in range(SSD_GROUPS):
        cols = jnp.array(dtc[g * 2 * HEADS_PER_GROUP:(g + 1) * 2 * HEADS_PER_GROUP])
        parts += [w_in_l[:, cols], pad]
    return jnp.concatenate(parts, axis=1).astype(BF16)


def run_model(dims, x, c, ctx, c_ctx, w_mod_a, w_mod_b, b_mod, g_mix, g_ffn, w_in, w_up, w_o, q_norm, k_norm,
              rnn_conv_w, rnn_conv_b, rnn_lambda, rnn_w_r, rnn_b_r, rnn_w_i, rnn_b_i,
              ssd_conv_w, ssd_conv_b, ssd_dt_bias, ssd_a_log, ssd_d, ssd_norm,
              router_w, router_b, moe_w_gate, moe_w_up, moe_w_down, g_final):
    depth = w_in.shape[0]
    bsz = dims.batch
    xs = jnp.concatenate([x.reshape(dims.n_lat, D_MODEL), ctx.reshape(bsz * dims.ctx, D_MODEL)], axis=0)

    cond = jnp.zeros((SUBLANES, D_MODEL), F32).at[0].set(c_ctx).at[1:1 + bsz].set(c)
    mod_all = adaln_all(cond, w_mod_a, w_mod_b, b_mod).reshape(depth, SUBLANES * N_MOD, 1, D_MODEL)
    rope_c, rope_s0, rope_s1 = rope_tables(dims)
    router_wt = router_w.T.astype(BF16)

    for l in range(depth):
        last = l == depth - 1
        m_rows = dims.n_lat if last else dims.n_tok
        modtab = mod_all[l]
        w_mix = _mixer_weight(w_in[l])
        w_gate = w_in[l][:, MIX_COLS:].astype(BF16)

        h = norm_mod(dims, xs, g_mix[l], modtab, 0, dims.n_tok)
        p = matmul(h, w_mix, 1280, F32)

        qn, kn, vn = qkv_prep(dims, p, rope_c, rope_s0, rope_s1, q_norm[l], k_norm[l])
        ya = attention(dims, qn, kn, vn)

        w_gates = jnp.concatenate([rnn_w_r[l, 0], rnn_w_i[l, 0], rnn_w_r[l, 1], rnn_w_i[l, 1]], axis=-1).astype(BF16)
        b_gates = jnp.concatenate(
            [v.reshape(RNN_BLOCKS, 1, RNN_BLOCK_DIM) for v in (rnn_b_r[l, 0], rnn_b_i[l, 0], rnn_b_r[l, 1], rnn_b_i[l, 1])],
            axis=-1)
        lam = jnp.concatenate([rnn_lambda[l, d].reshape(RNN_BLOCKS, 1, RNN_BLOCK_DIM) for d in range(2)], axis=-1)
        yr = rglru_branch(dims, p, rnn_conv_w[l], rnn_conv_b[l], w_gates, b_gates, lam)

        xbc = ssd_prep(dims, p, ssd_conv_w[l], ssd_conv_b[l])
        yf, yb = ssd_scan(dims, xbc, p, _group_dt_param(ssd_dt_bias[l]), _group_dt_param(ssd_a_log[l]))
        d_chan = jnp.repeat(ssd_d[l], SSD_HEAD_DIM).reshape(1, SSD_WIDTH)
        ys = ssd_finish(dims, yf, yb, xbc, p, d_chan, ssd_norm[l])

        merged = merge_branches(h, ya, yr, ys, w_gate, w_up[l].astype(BF16), m_rows)
        xs = matmul_residual(dims, merged, w_o[l].astype(BF16), xs, modtab, 2, m_rows)

        h2, logits_t = norm_mod(dims, xs, g_ffn[l], modtab, 3, m_rows, router_wt)
        combine = route(logits_t, router_b).T
        y2 = moe_dense(h2, combine, moe_w_gate[l].astype(BF16), moe_w_up[l].astype(BF16),
                       moe_w_down[l].astype(BF16), m_rows)
        xs = residual(dims, xs, y2, modtab, 5, m_rows)

    out = final_norm(xs, g_final, dims.n_lat)
    return out.reshape(bsz, dims.seq, D_MODEL)


def kernel(x, c, ctx, c_ctx, w_mod_a, w_mod_b, b_mod, g_mix, g_ffn, w_in, w_up, w_o, q_norm, k_norm, rnn_conv_w, rnn_conv_b, rnn_lambda, rnn_w_r, rnn_b_r, rnn_w_i, rnn_b_i, ssd_conv_w, ssd_conv_b, ssd_dt_bias, ssd_a_log, ssd_d, ssd_norm, router_w, router_b, moe_w_gate, moe_w_up, moe_w_down, g_final):
    dims = Dims(batch=x.shape[0], seq=x.shape[1], ctx=ctx.shape[1])
    return run_model(dims, x, c, ctx, c_ctx, w_mod_a, w_mod_b, b_mod, g_mix, g_ffn, w_in, w_up, w_o, q_norm, k_norm,
                     rnn_conv_w, rnn_conv_b, rnn_lambda, rnn_w_r, rnn_b_r, rnn_w_i, rnn_b_i,
                     ssd_conv_w, ssd_conv_b, ssd_dt_bias, ssd_a_log, ssd_d, ssd_norm,
                     router_w, router_b, moe_w_gate, moe_w_up, moe_w_down, g_final)
```

```python
import functools
from typing import NamedTuple

import jax
import jax.numpy as jnp
from jax import lax
from jax.experimental import pallas as pl
from jax.experimental.pallas import tpu as pltpu

F32 = jnp.float32
BF16 = jnp.bfloat16

D_MODEL = 4096
DEPTH = 4
GRID_W = 64
N_BRANCH = 3
BRANCH_WIDTH = 1024
HEAD_DIM = 128
N_Q_HEADS = 8
N_KV_HEADS = 2
Q_PER_KV = 4
KV_WIDTH = 256
N_ROPE_FREQ = 32
ROPE_THETA = 10000.0
RNN_WIDTH = 1024
RNN_BLOCKS = 8
RNN_BLOCK_DIM = 128
LRU_C = 8.0
CONV_W = 4
SSD_WIDTH = 1024
SSD_HEAD_DIM = 64
SSD_HEADS = 16
SSD_GROUPS = 2
SSD_STATE = 128
SSD_CHUNK = 128
SSD_BC = 256
SSD_XBC = 1536
HEADS_PER_GROUP = SSD_HEADS // SSD_GROUPS
GROUP_WIDTH = SSD_WIDTH // SSD_GROUPS
Q_OFF = 0
K_OFF = 1024
V_OFF = 1280
RX_OFF = 1536
RG_OFF = 2560
SZ_OFF = 3584
SX_OFF = 4608
SDT_OFF = 6144
MIX_COLS = 6176
MOD_RANK = 256
N_MOD = 6
N_EXPERTS = 16
N_EXPERT_GROUPS = 4
EXPERTS_PER_GROUP = 4
EXPERT_FF = 512
EPS = 1e-6

LANES = 128
SUBLANES = 8
VMEM_LIMIT_BYTES = 56 * 1024 * 1024
ROW_TILE = 256
SCAN_GROUPS = 4
MOE_TILE = 512
DMA_CHUNK = 128
RANK_BLOCK = 256
MM_TILE_M = 512
NEG_BIG = -1e30


class Dims(NamedTuple):
    batch: int
    seq: int
    ctx: int

    @property
    def n_lat(self):
        return self.batch * self.seq

    @property
    def n_tok(self):
        return self.batch * (self.seq + self.ctx)


def _cparams(sem):
    return pltpu.CompilerParams(dimension_semantics=sem, vmem_limit_bytes=VMEM_LIMIT_BYTES)


def _sigmoid(x):
    return 1.0 / (1.0 + jnp.exp(-x))


def _silu(x):
    return x * _sigmoid(x)


def _softplus(x):
    return jnp.maximum(x, 0.0) + jnp.log(1.0 + jnp.exp(-jnp.abs(x)))


def _gelu_tanh(x):
    return 0.5 * x * (1.0 + jnp.tanh(0.7978845608028654 * (x + 0.044715 * (x * x * x))))


def _dot(a, b):
    return jnp.dot(a, b, preferred_element_type=F32)


def _dot_nt(a, b):
    return lax.dot_general(a, b, (((1,), (1,)), ((), ())), preferred_element_type=F32)


def _mod_row(dims, tile, k):
    n_lat_tiles = dims.n_lat // tile
    per_batch = dims.seq // tile

    def f(i):
        return jnp.where(i < n_lat_tiles, 1 + i // per_batch, 0) * N_MOD + k
    return f


def _seg_block(dims, rows):
    lat_pb = dims.seq // rows
    ctx_pb = dims.ctx // rows
    ctx_base = dims.n_lat // rows

    def f(b, r):
        return jnp.where(r < lat_pb, b * lat_pb + r, ctx_base + b * ctx_pb + (r - lat_pb))
    return f


def _adaln_a_kernel(c_ref, w_ref, o_ref):
    c = c_ref[...]
    o_ref[0] = _dot(_silu(c).astype(BF16), w_ref[0].astype(BF16))


def _adaln_b_kernel(a_ref, w_ref, b_ref, o_ref):
    o_ref[0] = _dot(a_ref[0].astype(BF16), w_ref[0].astype(BF16)) + b_ref[0]


def adaln_all(cond, w_a, w_b, b):
    nl = w_a.shape[0]
    a = pl.pallas_call(
        _adaln_a_kernel,
        grid=(nl,),
        in_specs=[pl.BlockSpec((SUBLANES, D_MODEL), lambda l: (0, 0)),
                  pl.BlockSpec((1, D_MODEL, MOD_RANK), lambda l: (l, 0, 0))],
        out_specs=pl.BlockSpec((1, SUBLANES, MOD_RANK), lambda l: (l, 0, 0)),
        out_shape=jax.ShapeDtypeStruct((nl, SUBLANES, MOD_RANK), F32),
        compiler_params=_cparams(("arbitrary",)),
        name="adaln_a",
    )(cond, w_a)
    tn = 4096
    ncol = N_MOD * D_MODEL
    return pl.pallas_call(
        _adaln_b_kernel,
        grid=(nl, ncol // tn),
        in_specs=[pl.BlockSpec((1, SUBLANES, MOD_RANK), lambda l, j: (l, 0, 0)),
                  pl.BlockSpec((1, MOD_RANK, tn), lambda l, j: (l, 0, j)),
                  pl.BlockSpec((1, 1, tn), lambda l, j: (l, 0, j))],
        out_specs=pl.BlockSpec((1, SUBLANES, tn), lambda l, j: (l, 0, j)),
        out_shape=jax.ShapeDtypeStruct((nl, SUBLANES, ncol), F32),
        compiler_params=_cparams(("arbitrary", "arbitrary")),
        name="adaln_b",
    )(a, w_b, b.reshape(nl, 1, ncol))


def _norm_mod_kernel(x_ref, g_ref, sh_ref, sc_ref, o_ref):
    x = x_ref[...]
    y = x * lax.rsqrt(jnp.mean(x * x, axis=-1, keepdims=True) + EPS) * g_ref[...]
    o_ref[...] = (y * (1.0 + sc_ref[0]) + sh_ref[0]).astype(o_ref.dtype)


def _norm_mod_router_kernel(x_ref, g_ref, sh_ref, sc_ref, rw_ref, o_ref, lg_ref):
    x = x_ref[...]
    y = x * lax.rsqrt(jnp.mean(x * x, axis=-1, keepdims=True) + EPS) * g_ref[...]
    h = y * (1.0 + sc_ref[0]) + sh_ref[0]
    o_ref[...] = h
    lg_ref[...] = _dot_nt(rw_ref[...], h.astype(BF16))


def norm_mod(dims, x, gain, modtab, k, m_rows, router_wt=None):
    tm = ROW_TILE
    mrow = _mod_row(dims, tm, k)
    mrow1 = _mod_row(dims, tm, k + 1)
    in_specs = [pl.BlockSpec((tm, D_MODEL), lambda i: (i, 0)),
                pl.BlockSpec((1, D_MODEL), lambda i: (0, 0)),
                pl.BlockSpec((1, 1, D_MODEL), lambda i: (mrow(i), 0, 0)),
                pl.BlockSpec((1, 1, D_MODEL), lambda i: (mrow1(i), 0, 0))]
    h_spec = pl.BlockSpec((tm, D_MODEL), lambda i: (i, 0))
    if router_wt is None:
        return pl.pallas_call(
            _norm_mod_kernel, grid=(m_rows // tm,), in_specs=in_specs, out_specs=h_spec,
            out_shape=jax.ShapeDtypeStruct((m_rows, D_MODEL), BF16),
            compiler_params=_cparams(("arbitrary",)), name="norm_mod",
        )(x, gain.reshape(1, D_MODEL), modtab, modtab)
    return pl.pallas_call(
        _norm_mod_router_kernel, grid=(m_rows // tm,),
        in_specs=in_specs + [pl.BlockSpec((N_EXPERTS, D_MODEL), lambda i: (0, 0))],
        out_specs=[h_spec, pl.BlockSpec((N_EXPERTS, tm), lambda i: (0, i))],
        out_shape=[jax.ShapeDtypeStruct((m_rows, D_MODEL), F32),
                   jax.ShapeDtypeStruct((N_EXPERTS, m_rows), F32)],
        compiler_params=_cparams(("arbitrary",)), name="norm_mod_router",
    )(x, gain.reshape(1, D_MODEL), modtab, modtab, router_wt)


def _final_norm_kernel(x_ref, g_ref, o_ref):
    x = x_ref[...]
    o_ref[...] = x * lax.rsqrt(jnp.mean(x * x, axis=-1, keepdims=True) + EPS) * g_ref[...]


def final_norm(x, gain, m_rows):
    tm = ROW_TILE
    return pl.pallas_call(
        _final_norm_kernel, grid=(m_rows // tm,),
        in_specs=[pl.BlockSpec((tm, D_MODEL), lambda i: (i, 0)),
                  pl.BlockSpec((1, D_MODEL), lambda i: (0, 0))],
        out_specs=pl.BlockSpec((tm, D_MODEL), lambda i: (i, 0)),
        out_shape=jax.ShapeDtypeStruct((m_rows, D_MODEL), F32),
        compiler_params=_cparams(("arbitrary",)), name="final_norm",
    )(x, gain.reshape(1, D_MODEL))


def _mm_kernel(a_ref, w_ref, o_ref):
    o_ref[...] = _dot(a_ref[...], w_ref[0]).astype(o_ref.dtype)


def matmul(a, w_all, l, tn, out_dtype):
    m, k = a.shape
    n = w_all.shape[2]
    tm = MM_TILE_M
    return pl.pallas_call(
        _mm_kernel, grid=(n // tn, m // tm),
        in_specs=[pl.BlockSpec((tm, k), lambda j, i: (i, 0)),
                  pl.BlockSpec((1, k, tn), lambda j, i: (l, 0, j))],
        out_specs=pl.BlockSpec((tm, tn), lambda j, i: (i, j)),
        out_shape=jax.ShapeDtypeStruct((m, n), out_dtype),
        compiler_params=_cparams(("arbitrary", "arbitrary")), name="mm_in",
    )(a, w_all)


def _mm_res_kernel(a_ref, w_ref, x_ref, g_ref, o_ref):
    o_ref[...] = x_ref[...] + g_ref[0] * _dot(a_ref[...], w_ref[0])


def matmul_residual(dims, a, w_all, l, x, modtab, k, m_rows):
    kk = a.shape[1]
    n = w_all.shape[2]
    tm, tn = MM_TILE_M, 512
    mrow = _mod_row(dims, tm, k)
    return pl.pallas_call(
        _mm_res_kernel, grid=(n // tn, m_rows // tm),
        in_specs=[pl.BlockSpec((tm, kk), lambda j, i: (i, 0)),
                  pl.BlockSpec((1, kk, tn), lambda j, i: (l, 0, j)),
                  pl.BlockSpec((tm, tn), lambda j, i: (i, j)),
                  pl.BlockSpec((1, 1, tn), lambda j, i: (mrow(i), 0, j))],
        out_specs=pl.BlockSpec((tm, tn), lambda j, i: (i, j)),
        out_shape=jax.ShapeDtypeStruct((m_rows, n), F32),
        compiler_params=_cparams(("arbitrary", "arbitrary")), name="mm_out_res",
    )(a, w_all, x, modtab)


def _merge_kernel(h_ref, ya_ref, yr_ref, ys_ref, wg0_ref, wg1_ref, wg2_ref,
                  wu0_ref, wu1_ref, wu2_ref, o_ref):
    h = h_ref[...]
    acc = _sigmoid(_dot(h, wg0_ref[0])) * _dot(ya_ref[...], wu0_ref[0, 0])
    acc += _sigmoid(_dot(h, wg1_ref[0])) * _dot(yr_ref[...], wu1_ref[0, 0])
    acc += _sigmoid(_dot(h, wg2_ref[0])) * _dot(ys_ref[...], wu2_ref[0, 0])
    o_ref[...] = acc.astype(o_ref.dtype)


def merge_branches(h, ya, yr, ys, w_gate_all, w_up_all, l, m_rows):
    tm, tn = MM_TILE_M, 256
    nj = D_MODEL // tn
    y_spec = pl.BlockSpec((tm, BRANCH_WIDTH), lambda j, i: (i, 0))

    def wg_spec(n):
        return pl.BlockSpec((1, D_MODEL, tn), lambda j, i: (l, 0, n * nj + j))

    def wu_spec(n):
        return pl.BlockSpec((1, 1, BRANCH_WIDTH, tn), lambda j, i: (l, n, 0, j))

    return pl.pallas_call(
        _merge_kernel, grid=(nj, m_rows // tm),
        in_specs=[pl.BlockSpec((tm, D_MODEL), lambda j, i: (i, 0)), y_spec, y_spec, y_spec,
                  wg_spec(0), wg_spec(1), wg_spec(2), wu_spec(0), wu_spec(1), wu_spec(2)],
        out_specs=pl.BlockSpec((tm, tn), lambda j, i: (i, j)),
        out_shape=jax.ShapeDtypeStruct((m_rows, D_MODEL), BF16),
        compiler_params=_cparams(("arbitrary", "arbitrary")), name="merge",
    )(h, ya, yr, ys, w_gate_all, w_gate_all, w_gate_all, w_up_all, w_up_all, w_up_all)


def _qkv_prep_kernel(q_ref, k_ref, v_ref, c_ref, s0_ref, s1_ref, qg_ref, kg_ref,
                     qo_ref, ko_ref, vo_ref):
    c, s0, s1 = c_ref[...], s0_ref[...], s1_ref[...]

    def head(xh, gain, scale):
        y = xh * lax.rsqrt(jnp.mean(xh * xh, axis=-1, keepdims=True) + EPS) * gain
        r = y * c + pltpu.roll(y, 96, 1) * s0 + pltpu.roll(y, 32, 1) * s1
        return r * scale

    for hh in range(N_Q_HEADS):
        sl = slice(hh * HEAD_DIM, (hh + 1) * HEAD_DIM)
        qo_ref[:, sl] = head(q_ref[:, sl], qg_ref[...], HEAD_DIM ** -0.5).astype(qo_ref.dtype)
    for hh in range(N_KV_HEADS):
        sl = slice(hh * HEAD_DIM, (hh + 1) * HEAD_DIM)
        ko_ref[:, sl] = head(k_ref[:, sl], kg_ref[...], 1.0).astype(ko_ref.dtype)
    vo_ref[...] = v_ref[...].astype(vo_ref.dtype)


def qkv_prep(dims, p, rope_c, rope_s0, rope_s1, q_gain, k_gain):
    tm = ROW_TILE
    n_lat_tiles = dims.n_lat // tm
    per_batch = dims.seq // tm

    def tab(i):
        return (jnp.where(i < n_lat_tiles, i % per_batch, per_batch), 0)

    m = dims.n_tok
    tab_spec = pl.BlockSpec((tm, HEAD_DIM), tab)
    g_spec = pl.BlockSpec((1, HEAD_DIM), lambda i: (0, 0))
    return pl.pallas_call(
        _qkv_prep_kernel, grid=(m // tm,),
        in_specs=[pl.BlockSpec((tm, BRANCH_WIDTH), lambda i: (i, Q_OFF // BRANCH_WIDTH)),
                  pl.BlockSpec((tm, KV_WIDTH), lambda i: (i, K_OFF // KV_WIDTH)),
                  pl.BlockSpec((tm, KV_WIDTH), lambda i: (i, V_OFF // KV_WIDTH)),
                  tab_spec, tab_spec, tab_spec, g_spec, g_spec],
        out_specs=[pl.BlockSpec((tm, BRANCH_WIDTH), lambda i: (i, 0)),
                   pl.BlockSpec((tm, KV_WIDTH), lambda i: (i, 0)),
                   pl.BlockSpec((tm, KV_WIDTH), lambda i: (i, 0))],
        out_shape=[jax.ShapeDtypeStruct((m, BRANCH_WIDTH), BF16),
                   jax.ShapeDtypeStruct((m, KV_WIDTH), BF16),
                   jax.ShapeDtypeStruct((m, KV_WIDTH), BF16)],
        compiler_params=_cparams(("arbitrary",)), name="qkv_prep",
    )(p, p, p, rope_c, rope_s0, rope_s1, q_gain.reshape(1, HEAD_DIM), k_gain.reshape(1, HEAD_DIM))


def _attn_kernel(q_ref, kl_ref, kc_ref, vl_ref, vc_ref, o_ref, *, lat_tiles):
    qi = pl.program_id(2)

    @pl.when(qi < lat_tiles)
    def _():
        for g in range(Q_PER_KV):
            sl = slice(g * HEAD_DIM, (g + 1) * HEAD_DIM)
            q = q_ref[:, sl]
            s_l = _dot_nt(q, kl_ref[...])
            s_c = _dot_nt(q, kc_ref[...])
            mx = jnp.maximum(jnp.max(s_l, axis=-1, keepdims=True), jnp.max(s_c, axis=-1, keepdims=True))
            p_l = jnp.exp(s_l - mx)
            p_c = jnp.exp(s_c - mx)
            den = jnp.sum(p_l, axis=-1, keepdims=True) + jnp.sum(p_c, axis=-1, keepdims=True)
            o = _dot(p_l.astype(BF16), vl_ref[...]) + _dot(p_c.astype(BF16), vc_ref[...])
            o_ref[:, sl] = (o / den).astype(o_ref.dtype)

    @pl.when(qi >= lat_tiles)
    def _():
        for g in range(Q_PER_KV):
            sl = slice(g * HEAD_DIM, (g + 1) * HEAD_DIM)
            s_c = _dot_nt(q_ref[:, sl], kc_ref[...])
            p_c = jnp.exp(s_c - jnp.max(s_c, axis=-1, keepdims=True))
            den = jnp.sum(p_c, axis=-1, keepdims=True)
            o_ref[:, sl] = (_dot(p_c.astype(BF16), vc_ref[...]) / den).astype(o_ref.dtype)


def attention(dims, qn, kn, vn):
    tq = ROW_TILE
    assert dims.ctx == tq
    lat_tiles = dims.seq // tq
    seg = _seg_block(dims, tq)
    ctx_blk = dims.n_lat // dims.ctx
    gw = Q_PER_KV * HEAD_DIM
    return pl.pallas_call(
        functools.partial(_attn_kernel, lat_tiles=lat_tiles),
        grid=(dims.batch, N_KV_HEADS, lat_tiles + 1),
        in_specs=[pl.BlockSpec((tq, gw), lambda b, h, r: (seg(b, r), h)),
                  pl.BlockSpec((dims.seq, HEAD_DIM), lambda b, h, r: (b, h)),
                  pl.BlockSpec((dims.ctx, HEAD_DIM), lambda b, h, r: (ctx_blk + b, h)),
                  pl.BlockSpec((dims.seq, HEAD_DIM), lambda b, h, r: (b, h)),
                  pl.BlockSpec((dims.ctx, HEAD_DIM), lambda b, h, r: (ctx_blk + b, h))],
        out_specs=pl.BlockSpec((tq, gw), lambda b, h, r: (seg(b, r), h)),
        out_shape=jax.ShapeDtypeStruct((dims.n_tok, BRANCH_WIDTH), BF16),
        compiler_params=_cparams(("arbitrary", "arbitrary", "arbitrary")), name="attention",
    )(qn, kn, kn, vn, vn)


def _conv_rows(prev8, cur, next8, w_ref, b_ref):
    rows = cur.shape[0]
    ext = jnp.concatenate([prev8, cur, next8], axis=0)
    y = b_ref[...] + w_ref[0:1, :] * ext[7:7 + rows]
    y = y + w_ref[1:2, :] * cur
    y = y + w_ref[2:3, :] * ext[9:9 + rows]
    y = y + w_ref[3:4, :] * ext[10:10 + rows]
    return y


def _conv_chunk(src_ref, r0, seg_len, rows, w_ref, b_ref):
    cur = src_ref[pl.ds(r0, rows), :]
    p0 = pl.multiple_of(jnp.maximum(r0 - SUBLANES, 0), SUBLANES)
    n0 = pl.multiple_of(jnp.minimum(r0 + rows, seg_len - SUBLANES), SUBLANES)
    prev8 = src_ref[pl.ds(p0, SUBLANES), :] * jnp.where(r0 > 0, 1.0, 0.0)
    next8 = src_ref[pl.ds(n0, SUBLANES), :] * jnp.where(r0 + rows < seg_len, 1.0, 0.0)
    return _conv_rows(prev8, cur, next8, w_ref, b_ref)


def _rglru_kernel(rxl_ref, rxc_ref, rgl_ref, rgc_ref, cw_ref, cb_ref, wg_ref, bg_ref, lam_ref,
                  o_ref, af_s, uf_s, ab_s, ub_s, hf_s, hb_s, *, seq, ctx):
    r = pl.program_id(2)
    tot = seq + ctx
    rows = ROW_TILE
    w = RNN_BLOCK_DIM

    @pl.when(r == 0)
    def _():
        sp_f = _softplus(-lam_ref[0, :, 0:w])
        sp_b = _softplus(-lam_ref[0, :, w:2 * w])

        def gates(x, base):
            z = _dot(x.astype(BF16), wg_ref[0]) + bg_ref[0]
            sg = _sigmoid(z)
            for d, (sp, a_s, u_s) in enumerate(((sp_f, af_s, uf_s), (sp_b, ab_s, ub_s))):
                rg = sg[:, (2 * d) * w:(2 * d + 1) * w]
                ig = sg[:, (2 * d + 1) * w:(2 * d + 2) * w]
                a = jnp.exp(-LRU_C * rg * sp)
                a_s[pl.ds(base, rows), :] = a
                u_s[pl.ds(base, rows), :] = jnp.sqrt(1.0 - a * a) * (ig * x)

        for c0 in range(0, ctx, rows):
            gates(_conv_chunk(rxc_ref, c0, ctx, rows, cw_ref, cb_ref), c0)

        def lat_body(c, carry):
            r0 = pl.multiple_of(c * rows, rows)
            gates(_conv_chunk(rxl_ref, r0, seq, rows, cw_ref, cb_ref), pl.multiple_of(ctx + r0, rows))
            return carry
        lax.fori_loop(0, seq // rows, lat_body, 0)

        row = lax.broadcasted_iota(jnp.int32, (SUBLANES, w), 0)
        blk = SCAN_GROUPS * SUBLANES

        def scan_block(base, carry, a_s, u_s, out_s, reverse):
            a_blk = a_s[pl.ds(base, blk), :]
            u_blk = u_s[pl.ds(base, blk), :]
            hs = [None] * SCAN_GROUPS
            for g in (range(SCAN_GROUPS - 1, -1, -1) if reverse else range(SCAN_GROUPS)):
                a = a_blk[g * SUBLANES:(g + 1) * SUBLANES]
                u = u_blk[g * SUBLANES:(g + 1) * SUBLANES]
                for k in (1, 2, 4):
                    sh = SUBLANES - k if reverse else k
                    m = (row < SUBLANES - k) if reverse else (row >= k)
                    u = u + a * jnp.where(m, pltpu.roll(u, sh, 0), 0.0)
                    a = a * jnp.where(m, pltpu.roll(a, sh, 0), 1.0)
                hs[g] = u + a * carry
                e = 0 if reverse else SUBLANES - 1
                carry = (jnp.broadcast_to(u[e:e + 1, :], (SUBLANES, w))
                         + jnp.broadcast_to(a[e:e + 1, :], (SUBLANES, w)) * carry)
            out_s[pl.ds(base, blk), :] = jnp.concatenate(hs, axis=0)
            return carry

        def both(i, carry, f0, b0):
            cf, cb = carry
            cf = scan_block(pl.multiple_of(f0 + i * blk, blk), cf, af_s, uf_s, hf_s, False)
            cb = scan_block(pl.multiple_of(b0 - (i + 1) * blk, blk), cb, ab_s, ub_s, hb_s, True)
            return cf, cb

        zero = jnp.zeros((SUBLANES, w), F32)
        carry = lax.fori_loop(0, ctx // blk, lambda i, c: both(i, c, 0, ctx), (zero, zero))
        lax.fori_loop(0, seq // blk, lambda i, c: both(i, c, ctx, tot), carry)

        for c0 in range(0, ctx, rows):
            sl = pl.ds(c0, rows)
            hf_s[sl, :] = _gelu_tanh(rgc_ref[sl, :]) * (hf_s[sl, :] + hb_s[sl, :])

        def out_body(c, carry):
            r0 = pl.multiple_of(c * rows, rows)
            sl = pl.ds(pl.multiple_of(ctx + r0, rows), rows)
            hf_s[sl, :] = _gelu_tanh(rgl_ref[pl.ds(r0, rows), :]) * (hf_s[sl, :] + hb_s[sl, :])
            return carry
        lax.fori_loop(0, seq // rows, out_body, 0)

    lat_pieces = seq // rows
    src = jnp.where(r < lat_pieces, ctx + r * rows, (r - lat_pieces) * rows)
    o_ref[...] = hf_s[pl.ds(pl.multiple_of(src, rows), rows), :].astype(o_ref.dtype)


def rglru_branch(dims, p, conv_w, conv_b, w_gates, b_gates, lam):
    rows = ROW_TILE
    w = RNN_BLOCK_DIM
    pieces = (dims.seq + dims.ctx) // rows
    seg = _seg_block(dims, rows)
    ctx_blk = dims.n_lat // dims.ctx
    rx0, rg0 = RX_OFF // w, RG_OFF // w
    tot = dims.seq + dims.ctx
    return pl.pallas_call(
        functools.partial(_rglru_kernel, seq=dims.seq, ctx=dims.ctx),
        grid=(dims.batch, RNN_BLOCKS, pieces),
        in_specs=[pl.BlockSpec((dims.seq, w), lambda b, n, r: (b, rx0 + n)),
                  pl.BlockSpec((dims.ctx, w), lambda b, n, r: (ctx_blk + b, rx0 + n)),
                  pl.BlockSpec((dims.seq, w), lambda b, n, r: (b, rg0 + n)),
                  pl.BlockSpec((dims.ctx, w), lambda b, n, r: (ctx_blk + b, rg0 + n)),
                  pl.BlockSpec((CONV_W, w), lambda b, n, r: (0, n)),
                  pl.BlockSpec((1, w), lambda b, n, r: (0, n)),
                  pl.BlockSpec((1, w, 4 * w), lambda b, n, r: (n, 0, 0)),
                  pl.BlockSpec((1, 1, 4 * w), lambda b, n, r: (n, 0, 0)),
                  pl.BlockSpec((1, 1, 2 * w), lambda b, n, r: (n, 0, 0))],
        out_specs=pl.BlockSpec((rows, w), lambda b, n, r: (seg(b, r), n)),
        out_shape=jax.ShapeDtypeStruct((dims.n_tok, RNN_WIDTH), BF16),
        scratch_shapes=[pltpu.VMEM((tot, w), F32) for _ in range(6)],
        compiler_params=_cparams(("arbitrary", "arbitrary", "arbitrary")), name="rglru",
    )(p, p, p, p, conv_w, conv_b.reshape(1, RNN_WIDTH), w_gates, b_gates, lam)


def _ssd_prep_kernel(cur_ref, prev_ref, next_ref, w_ref, b_ref, o_ref, *, n_lat_tiles, per_batch):
    i = pl.program_id(0)
    j = i % per_batch
    lat = i < n_lat_tiles
    pv = jnp.where(jnp.logical_and(lat, j != 0), 1.0, 0.0)
    nv = jnp.where(jnp.logical_and(lat, j != per_batch - 1), 1.0, 0.0)
    y = _conv_rows(prev_ref[...] * pv, cur_ref[...], next_ref[...] * nv, w_ref, b_ref)
    o_ref[...] = _silu(y)


def ssd_prep(dims, p, conv_w, conv_b):
    tm = ROW_TILE
    assert dims.ctx == tm
    m = dims.n_tok
    hb = tm // SUBLANES
    last8 = m // SUBLANES - 1
    c0 = SX_OFF // SSD_XBC
    return pl.pallas_call(
        functools.partial(_ssd_prep_kernel, n_lat_tiles=dims.n_lat // tm, per_batch=dims.seq // tm),
        grid=(m // tm,),
        in_specs=[pl.BlockSpec((tm, SSD_XBC), lambda i: (i, c0)),
                  pl.BlockSpec((SUBLANES, SSD_XBC), lambda i: (jnp.maximum(i * hb - 1, 0), c0)),
                  pl.BlockSpec((SUBLANES, SSD_XBC), lambda i: (jnp.minimum((i + 1) * hb, last8), c0)),
                  pl.BlockSpec((CONV_W, SSD_XBC), lambda i: (0, 0)),
                  pl.BlockSpec((1, SSD_XBC), lambda i: (0, 0))],
        out_specs=pl.BlockSpec((tm, SSD_XBC), lambda i: (i, 0)),
        out_shape=jax.ShapeDtypeStruct((m, SSD_XBC), F32),
        compiler_params=_cparams(("arbitrary",)), name="ssd_prep",
    )(p, p, p, conv_w, conv_b.reshape(1, SSD_XBC))


def _ssd_chunk(x_ref, b_ref, c_ref, dt_ref, dtb_ref, alog_ref, s_ref, y_ref, reverse):
    q = SSD_CHUNK
    col0 = HEADS_PER_GROUP if reverse else 0
    ri = lax.broadcasted_iota(jnp.int32, (q, q), 0)
    ci = lax.broadcasted_iota(jnp.int32, (q, q), 1)
    tri = (ri <= ci) if reverse else (ri >= ci)
    cum = jnp.where(tri, 1.0, 0.0).astype(BF16)
    ones = jnp.ones((q, q), BF16)
    left = ci < SSD_HEAD_DIM
    top = ri < SSD_HEAD_DIM

    dtc = _softplus(dt_ref[...] + dtb_ref[0])
    a = dtc * (-jnp.exp(alog_ref[0]))
    a_hi = a.astype(BF16)
    r1 = a - a_hi.astype(F32)
    a_mid = r1.astype(BF16)
    a_lo = (r1 - a_mid.astype(F32)).astype(BF16)
    acum = _dot(cum, a_hi) + _dot(cum, a_mid) + _dot(cum, a_lo)
    atot = _dot(ones, a_hi) + _dot(ones, a_mid) + _dot(ones, a_lo)
    acum_t = acum.T

    bm = b_ref[...].astype(BF16)
    cm = c_ref[...].astype(BF16)
    cb = _dot_nt(cm, bm)

    for pair in range(HEADS_PER_GROUP // 2):
        ca = col0 + 2 * pair
        lanes = slice(pair * LANES, (pair + 1) * LANES)
        col_a, col_b = acum[:, ca:ca + 1], acum[:, ca + 1:ca + 2]
        row_a, row_b = acum_t[ca:ca + 1, :], acum_t[ca + 1:ca + 2, :]
        l_a = jnp.exp(jnp.where(tri, col_a - row_a, NEG_BIG))
        l_b = jnp.exp(jnp.where(tri, col_b - row_b, NEG_BIG))
        xdt = x_ref[:, lanes] * jnp.where(left, dtc[:, ca:ca + 1], dtc[:, ca + 1:ca + 2])
        xdt_b = xdt.astype(BF16)
        y_in = jnp.where(left, _dot((cb * l_a).astype(BF16), xdt_b), _dot((cb * l_b).astype(BF16), xdt_b))
        s_old = s_ref[lanes, :]
        y_st = _dot_nt(cm, s_old.astype(BF16)) * jnp.where(left, jnp.exp(col_a), jnp.exp(col_b))
        y_ref[:, lanes] = y_in + y_st
        tot_a, tot_b = atot[:, ca:ca + 1], atot[:, ca + 1:ca + 2]
        xw = xdt * jnp.where(left, jnp.exp(tot_a - col_a), jnp.exp(tot_b - col_b))
        s_new = _dot(xw.T.astype(BF16), bm)
        s_ref[lanes, :] = s_old * jnp.where(top, jnp.exp(tot_a), jnp.exp(tot_b)) + s_new


def _ssd_scan_kernel(xf_ref, bf_ref, cf_ref, dtf_ref, xb_ref, bb_ref, cb_ref, dtb_ref_,
                     bias_ref, alog_ref, yf_ref, yb_ref, sf_s, sb_s):
    @pl.when(pl.program_id(2) == 0)
    def _():
        sf_s[...] = jnp.zeros_like(sf_s)
        sb_s[...] = jnp.zeros_like(sb_s)

    _ssd_chunk(xf_ref, bf_ref, cf_ref, dtf_ref, bias_ref, alog_ref, sf_s, yf_ref, False)
    _ssd_chunk(xb_ref, bb_ref, cb_ref, dtb_ref_, bias_ref, alog_ref, sb_s, yb_ref, True)


def ssd_scan(dims, xbc, dt, dt_bias, a_log):
    q = SSD_CHUNK
    lat_c, ctx_c = dims.seq // q, dims.ctx // q
    ctx_base = dims.n_lat // q
    steps = lat_c + ctx_c

    def cf(b, s):
        return jnp.where(s < ctx_c, ctx_base + b * ctx_c + s, b * lat_c + (s - ctx_c))

    def cbk(b, s):
        return jnp.where(s < ctx_c, ctx_base + b * ctx_c + (ctx_c - 1 - s),
                         b * lat_c + (lat_c - 1 - (s - ctx_c)))

    gw = GROUP_WIDTH
    bcol, ccol = SSD_WIDTH // SSD_STATE, (SSD_WIDTH + SSD_BC) // SSD_STATE

    def specs(cfun):
        return [pl.BlockSpec((q, gw), lambda b, g, s: (cfun(b, s), g)),
                pl.BlockSpec((q, SSD_STATE), lambda b, g, s: (cfun(b, s), bcol + g)),
                pl.BlockSpec((q, SSD_STATE), lambda b, g, s: (cfun(b, s), ccol + g)),
                pl.BlockSpec((q, LANES), lambda b, g, s: (cfun(b, s), g))]

    par_spec = pl.BlockSpec((1, 1, LANES), lambda b, g, s: (g, 0, 0))
    yshape = jax.ShapeDtypeStruct((dims.n_tok, SSD_WIDTH), F32)
    return pl.pallas_call(
        _ssd_scan_kernel, grid=(dims.batch, SSD_GROUPS, steps),
        in_specs=specs(cf) + specs(cbk) + [par_spec, par_spec],
        out_specs=[pl.BlockSpec((q, gw), lambda b, g, s: (cf(b, s), g)),
                   pl.BlockSpec((q, gw), lambda b, g, s: (cbk(b, s), g))],
        out_shape=[yshape, yshape],
        scratch_shapes=[pltpu.VMEM((gw, SSD_STATE), F32), pltpu.VMEM((gw, SSD_STATE), F32)],
        compiler_params=_cparams(("arbitrary", "arbitrary", "arbitrary")), name="ssd_scan",
    )(xbc, xbc, xbc, dt, xbc, xbc, xbc, dt, dt_bias, a_log)


def _ssd_finish_kernel(yf_ref, yb_ref, x_ref, z_ref, d_ref, g_ref, o_ref):
    y = d_ref[...] * x_ref[...] + yf_ref[...] + yb_ref[...]
    gt = y * _silu(z_ref[...])
    o_ref[...] = (gt * lax.rsqrt(jnp.mean(gt * gt, axis=-1, keepdims=True) + EPS) * g_ref[...]).astype(o_ref.dtype)


def ssd_finish(dims, yf, yb, xbc, p, d_chan, norm_g):
    tm = ROW_TILE
    gw = GROUP_WIDTH
    z0 = SZ_OFF // gw
    blk = pl.BlockSpec((tm, gw), lambda i, g: (i, g))
    vec = pl.BlockSpec((1, gw), lambda i, g: (0, g))
    return pl.pallas_call(
        _ssd_finish_kernel, grid=(dims.n_tok // tm, SSD_GROUPS),
        in_specs=[blk, blk, blk, pl.BlockSpec((tm, gw), lambda i, g: (i, z0 + g)), vec, vec],
        out_specs=blk,
        out_shape=jax.ShapeDtypeStruct((dims.n_tok, SSD_WIDTH), BF16),
        compiler_params=_cparams(("arbitrary", "arbitrary")), name="ssd_finish",
    )(yf, yb, xbc, p, d_chan, norm_g.reshape(1, SSD_WIDTH))


def _route_kernel(lg_ref, rb_ref, cw_ref, dest_ref, tab_ref, *, m, tile):
    score = _sigmoid(lg_ref[...])
    sel = score + rb_ref[...]
    v = [sel[e:e + 1, :] for e in range(N_EXPERTS)]
    sc = [score[e:e + 1, :] for e in range(N_EXPERTS)]
    best, best_g = None, None
    for g in range(N_EXPERT_GROUPS):
        vg = v[4 * g:4 * g + 4]
        gs = vg[0] + vg[1]
        for (i, j) in ((0, 2), (0, 3), (1, 2), (1, 3), (2, 3)):
            gs = jnp.maximum(gs, vg[i] + vg[j])
        if g == 0:
            best, best_g = gs, jnp.zeros_like(gs, dtype=jnp.int32)
        else:
            better = gs > best
            best_g = jnp.where(better, g, best_g)
            best = jnp.where(better, gs, best)
    wts = []
    for e in range(N_EXPERTS):
        g = e // EXPERTS_PER_GROUP
        rank = jnp.zeros_like(best_g)
        for k in range(4 * g, 4 * g + 4):
            if k == e:
                continue
            ahead = (v[k] >= v[e]) if k < e else (v[k] > v[e])
            rank = rank + jnp.where(ahead, 1, 0)
        chosen = jnp.logical_and(best_g == g, rank < 2)
        wts.append(jnp.where(chosen, sc[e], 0.0))
    tot = wts[0]
    for e in range(1, N_EXPERTS):
        tot = tot + wts[e]
    for j in range(EXPERTS_PER_GROUP):
        wj = wts[j]
        for g in range(1, N_EXPERT_GROUPS):
            wj = wj + wts[EXPERTS_PER_GROUP * g + j]
        cw_ref[j:j + 1, :] = wj / tot
    cw_ref[EXPERTS_PER_GROUP:SUBLANES, :] = jnp.zeros((SUBLANES - EXPERTS_PER_GROUP, m), F32)

    row8 = lax.broadcasted_iota(jnp.int32, (SUBLANES, m), 0)
    onehot = jnp.where(row8 == best_g, 1.0, 0.0)
    counts = jnp.sum(onehot, axis=-1, keepdims=True)
    padded = jnp.floor((counts + (tile - 1)) * (1.0 / tile)) * tile
    pb = jnp.broadcast_to(padded, (SUBLANES, LANES))
    r128 = lax.broadcasted_iota(jnp.int32, (SUBLANES, LANES), 0)
    off_b = jnp.zeros((SUBLANES, LANES), F32)
    for k in range(1, N_EXPERT_GROUPS):
        off_b = off_b + jnp.where(r128 >= k, pltpu.roll(pb, k, 0), 0.0)
    off = off_b[:, 0:1]
    ri = lax.broadcasted_iota(jnp.int32, (RANK_BLOCK, RANK_BLOCK), 0)
    ci = lax.broadcasted_iota(jnp.int32, (RANK_BLOCK, RANK_BLOCK), 1)
    before = jnp.where(ri < ci, 1.0, 0.0).astype(BF16)
    carry = jnp.zeros((SUBLANES, 1), F32)
    for b in range(m // RANK_BLOCK):
        sl = slice(b * RANK_BLOCK, (b + 1) * RANK_BLOCK)
        ob = onehot[:, sl]
        pre = _dot(ob.astype(BF16), before) + carry
        dest_ref[:, sl] = jnp.sum(ob * (pre + off), axis=0, keepdims=True).astype(jnp.int32)
        carry = carry + jnp.sum(ob, axis=-1, keepdims=True)
    lane = lax.broadcasted_iota(jnp.int32, (SUBLANES, LANES), 1).astype(F32)
    ends = off + padded
    passed = jnp.where(jnp.logical_and(lane * tile >= ends, r128 < N_EXPERT_GROUPS), 1.0, 0.0)
    tile_group = jnp.sum(passed, axis=0, keepdims=True)
    n_used = jnp.sum(pb, axis=0, keepdims=True) * (1.0 / tile)
    tab = jnp.where(r128 == 0, tile_group, jnp.where(r128 == 1, n_used, 0.0))
    tab_ref[...] = tab.astype(jnp.int32)


def route(logits_t, router_b):
    m = logits_t.shape[1]
    return pl.pallas_call(
        functools.partial(_route_kernel, m=m, tile=MOE_TILE), grid=(1,),
        in_specs=[pl.BlockSpec((N_EXPERTS, m), lambda i: (0, 0)),
                  pl.BlockSpec((N_EXPERTS, 1), lambda i: (0, 0))],
        out_specs=[pl.BlockSpec((SUBLANES, m), lambda i: (0, 0)),
                   pl.BlockSpec((1, m), lambda i: (0, 0)),
                   pl.BlockSpec((SUBLANES, LANES), lambda i: (0, 0))],
        out_shape=[jax.ShapeDtypeStruct((SUBLANES, m), F32),
                   jax.ShapeDtypeStruct((1, m), jnp.int32),
                   jax.ShapeDtypeStruct((SUBLANES, LANES), jnp.int32)],
        compiler_params=_cparams(("arbitrary",)), name="route",
    )(logits_t, router_b.reshape(N_EXPERTS, 1))


def _dispatch_kernel(dest_ref, h_ref, c_ref, hs_in, cs_in, hs_ref, cs_ref, sem):
    del hs_in, cs_in
    m = h_ref.shape[0]
    n_chunks = m // DMA_CHUNK

    def copies(t):
        d = dest_ref[t]
        return (pltpu.make_async_copy(h_ref.at[pl.ds(t, 1)], hs_ref.at[pl.ds(d, 1)], sem.at[0]),
                pltpu.make_async_copy(c_ref.at[pl.ds(t, 1)], cs_ref.at[pl.ds(d, 1)], sem.at[1]))

    def issue(chunk):
        def body(r, c):
            for cp in copies(chunk * DMA_CHUNK + r):
                cp.start()
            return c
        lax.fori_loop(0, DMA_CHUNK, body, 0)

    def drain(chunk):
        def body(r, c):
            for cp in copies(chunk * DMA_CHUNK + r):
                cp.wait()
            return c
        lax.fori_loop(0, DMA_CHUNK, body, 0)

    issue(0)

    def step(chunk, c):
        issue(chunk)
        drain(chunk - 1)
        return c
    lax.fori_loop(1, n_chunks, step, 0)
    drain(n_chunks - 1)


def dispatch(dest, h, cw_rows, n_slots):
    m = h.shape[0]
    assert m % DMA_CHUNK == 0
    any_spec = pl.BlockSpec(memory_space=pl.ANY)
    hs0 = jnp.zeros((n_slots, D_MODEL), F32)
    cs0 = jnp.zeros((n_slots, LANES), F32)
    return pl.pallas_call(
        _dispatch_kernel,
        grid_spec=pltpu.PrefetchScalarGridSpec(
            num_scalar_prefetch=1, grid=(1,),
            in_specs=[any_spec, any_spec, any_spec, any_spec],
            out_specs=[any_spec, any_spec],
            scratch_shapes=[pltpu.SemaphoreType.DMA((2,))]),
        out_shape=[jax.ShapeDtypeStruct((n_slots, D_MODEL), F32),
                   jax.ShapeDtypeStruct((n_slots, LANES), F32)],
        input_output_aliases={3: 0, 4: 1},
        compiler_params=_cparams(("arbitrary",)), name="moe_dispatch",
    )(dest, h, cw_rows, hs0, cs0)


def _moe_group_kernel(tg_ref, nu_ref, x_ref, c_ref, wg_ref, wu_ref, wd_ref, o_ref, xb_s):
    i = pl.program_id(0)
    j = pl.program_id(1)
    f = pl.program_id(2)
    active = i < nu_ref[0]

    @pl.when(jnp.logical_and(j == 0, f == 0))
    def _():
        xb_s[...] = x_ref[...].astype(BF16)
        o_ref[...] = jnp.zeros_like(o_ref)

    @pl.when(active)
    def _():
        xb = xb_s[...]
        g = _dot(xb, wg_ref[0, 0])
        u = _dot(xb, wu_ref[0, 0])
        c = c_ref[...]
        lane = lax.broadcasted_iota(jnp.int32, c.shape, 1)
        cw = jnp.sum(jnp.where(lane == j, c, 0.0), axis=-1, keepdims=True)
        hid = _silu(g) * u * cw
        o_ref[...] += _dot(hid.astype(BF16), wd_ref[0, 0])


def moe_grouped(hs, cs, tile_group, n_used, w_gate_all, w_up_all, w_down_all, l):
    tm = MOE_TILE
    tf = EXPERT_FF // 2
    n_tiles = hs.shape[0] // tm
    last_f = EXPERT_FF // tf - 1

    def tile(i, nu):
        return jnp.minimum(i, nu[0] - 1)

    def expert(i, j, tg, nu):
        jj = jnp.where(i < nu[0], j, EXPERTS_PER_GROUP - 1)
        return tg[tile(i, nu)] * EXPERTS_PER_GROUP + jj

    def fblk(i, f, nu):
        return jnp.where(i < nu[0], f, last_f)

    return pl.pallas_call(
        _moe_group_kernel,
        grid_spec=pltpu.PrefetchScalarGridSpec(
            num_scalar_prefetch=2, grid=(n_tiles, EXPERTS_PER_GROUP, EXPERT_FF // tf),
            in_specs=[pl.BlockSpec((tm, D_MODEL), lambda i, j, f, tg, nu: (tile(i, nu), 0)),
                      pl.BlockSpec((tm, LANES), lambda i, j, f, tg, nu: (tile(i, nu), 0)),
                      pl.BlockSpec((1, 1, D_MODEL, tf),
                                   lambda i, j, f, tg, nu: (l, expert(i, j, tg, nu), 0, fblk(i, f, nu))),
                      pl.BlockSpec((1, 1, D_MODEL, tf),
                                   lambda i, j, f, tg, nu: (l, expert(i, j, tg, nu), 0, fblk(i, f, nu))),
                      pl.BlockSpec((1, 1, tf, D_MODEL),
                                   lambda i, j, f, tg, nu: (l, expert(i, j, tg, nu), fblk(i, f, nu), 0))],
            out_specs=pl.BlockSpec((tm, D_MODEL), lambda i, j, f, tg, nu: (i, 0)),
            scratch_shapes=[pltpu.VMEM((tm, D_MODEL), BF16)]),
        out_shape=jax.ShapeDtypeStruct((hs.shape[0], D_MODEL), F32),
        compiler_params=_cparams(("arbitrary", "arbitrary", "arbitrary")), name="moe_grouped",
    )(tile_group, n_used, hs, cs, w_gate_all, w_up_all, w_down_all)


def _combine_kernel(dest_ref, x_ref, g_ref, ys_ref, o_ref, buf, sem):
    tm = x_ref.shape[0]
    i = pl.program_id(0)
    n = pl.num_programs(0)

    def gather(tile, slot, wait):
        def body(r, c):
            d = dest_ref[tile * tm + r]
            cp = pltpu.make_async_copy(ys_ref.at[pl.ds(d, 1)], buf.at[slot, pl.ds(r, 1)], sem.at[slot])
            if wait:
                cp.wait()
            else:
                cp.start()
            return c
        lax.fori_loop(0, tm, body, 0)

    @pl.when(i == 0)
    def _():
        gather(0, 0, False)

    @pl.when(i + 1 < n)
    def _():
        gather(i + 1, (i + 1) % 2, False)

    slot = i % 2
    gather(i, slot, True)
    o_ref[...] = x_ref[...] + g_ref[0] * buf[slot]


def combine(dims, dest, x, ys, modtab, k, m_rows):
    tm = ROW_TILE
    mrow = _mod_row(dims, tm, k)
    return pl.pallas_call(
        _combine_kernel,
        grid_spec=pltpu.PrefetchScalarGridSpec(
            num_scalar_prefetch=1, grid=(m_rows // tm,),
            in_specs=[pl.BlockSpec((tm, D_MODEL), lambda i, d: (i, 0)),
                      pl.BlockSpec((1, 1, D_MODEL), lambda i, d: (mrow(i), 0, 0)),
                      pl.BlockSpec(memory_space=pl.ANY)],
            out_specs=pl.BlockSpec((tm, D_MODEL), lambda i, d: (i, 0)),
            scratch_shapes=[pltpu.VMEM((2, tm, D_MODEL), F32), pltpu.SemaphoreType.DMA((2,))]),
        out_shape=jax.ShapeDtypeStruct((m_rows, D_MODEL), F32),
        compiler_params=_cparams(("arbitrary",)), name="moe_combine",
    )(dest, x, modtab, ys)


def rope_tables(dims):
    s = dims.seq
    rows = s // GRID_W
    row = jnp.repeat(jnp.arange(rows, dtype=F32), GRID_W)
    col = jnp.tile(jnp.arange(GRID_W, dtype=F32), rows)
    inv = ROPE_THETA ** (-jnp.arange(N_ROPE_FREQ, dtype=F32) / N_ROPE_FREQ)
    ang = jnp.stack([row[:, None] * inv, col[:, None] * inv], axis=1)
    cos, sin = jnp.cos(ang), jnp.sin(ang)
    zero = jnp.zeros_like(sin)
    c = jnp.stack([cos, cos], axis=2).reshape(s, HEAD_DIM)
    s0 = jnp.stack([-sin, zero], axis=2).reshape(s, HEAD_DIM)
    s1 = jnp.stack([zero, sin], axis=2).reshape(s, HEAD_DIM)
    ident = jnp.ones((ROW_TILE, HEAD_DIM), F32)
    zpad = jnp.zeros((ROW_TILE, HEAD_DIM), F32)
    return (jnp.concatenate([c, ident], 0), jnp.concatenate([s0, zpad], 0), jnp.concatenate([s1, zpad], 0))


def _dt_cols():
    cols = []
    for g in range(SSD_GROUPS):
        for d in range(2):
            cols += [SDT_OFF + d * SSD_HEADS + g * HEADS_PER_GROUP + hh for hh in range(HEADS_PER_GROUP)]
    return cols


def _group_dt_param(v):
    out = []
    for g in range(SSD_GROUPS):
        hs = slice(g * HEADS_PER_GROUP, (g + 1) * HEADS_PER_GROUP)
        row = jnp.concatenate([v[0, hs], v[1, hs], jnp.zeros((LANES - 2 * HEADS_PER_GROUP,), v.dtype)])
        out.append(row.reshape(1, LANES))
    return jnp.stack(out, 0)


def _dt_weight(w_in):
    nl = w_in.shape[0]
    pad = jnp.zeros((nl, D_MODEL, LANES - 2 * HEADS_PER_GROUP), w_in.dtype)
    dtw = w_in[:, :, SDT_OFF:MIX_COLS].reshape(nl, D_MODEL, 2, SSD_GROUPS, HEADS_PER_GROUP)
    parts = []
    for g in range(SSD_GROUPS):
        parts += [dtw[:, :, :, g, :].reshape(nl, D_MODEL, 2 * HEADS_PER_GROUP), pad]
    return jnp.concatenate(parts, axis=2).astype(BF16)


def run_model(dims, x, c, ctx, c_ctx, w_mod_a, w_mod_b, b_mod, g_mix, g_ffn, w_in, w_up, w_o, q_norm, k_norm,
              rnn_conv_w, rnn_conv_b, rnn_lambda, rnn_w_r, rnn_b_r, rnn_w_i, rnn_b_i,
              ssd_conv_w, ssd_conv_b, ssd_dt_bias, ssd_a_log, ssd_d, ssd_norm,
              router_w, router_b, moe_w_gate, moe_w_up, moe_w_down, g_final):
    depth = w_in.shape[0]
    bsz = dims.batch
    xs = jnp.concatenate([x.reshape(dims.n_lat, D_MODEL), ctx.reshape(bsz * dims.ctx, D_MODEL)], axis=0)

    cond = jnp.zeros((SUBLANES, D_MODEL), F32).at[0].set(c_ctx).at[1:1 + bsz].set(c)
    mod_all = adaln_all(cond, w_mod_a, w_mod_b, b_mod).reshape(depth, SUBLANES * N_MOD, 1, D_MODEL)
    rope_c, rope_s0, rope_s1 = rope_tables(dims)
    router_wt = router_w.T.astype(BF16)

    w_mix_all = w_in[:, :, :SDT_OFF].astype(BF16)
    w_gate_all = w_in[:, :, MIX_COLS:].astype(BF16)
    w_dt_all = _dt_weight(w_in)
    w_up_all = w_up.astype(BF16)
    w_o_all = w_o.astype(BF16)
    moe_g_all = moe_w_gate.astype(BF16)
    moe_u_all = moe_w_up.astype(BF16)
    moe_d_all = moe_w_down.astype(BF16)

    for l in range(depth):
        last = l == depth - 1
        m_rows = dims.n_lat if last else dims.n_tok
        modtab = mod_all[l]

        h = norm_mod(dims, xs, g_mix[l], modtab, 0, dims.n_tok)
        p = matmul(h, w_mix_all, l, 768, F32)
        dt = matmul(h, w_dt_all, l, SSD_GROUPS * LANES, F32)

        qn, kn, vn = qkv_prep(dims, p, rope_c, rope_s0, rope_s1, q_norm[l], k_norm[l])
        ya = attention(dims, qn, kn, vn)

        w_gates = jnp.concatenate([rnn_w_r[l, 0], rnn_w_i[l, 0], rnn_w_r[l, 1], rnn_w_i[l, 1]], axis=-1).astype(BF16)
        b_gates = jnp.concatenate(
            [v.reshape(RNN_BLOCKS, 1, RNN_BLOCK_DIM) for v in (rnn_b_r[l, 0], rnn_b_i[l, 0], rnn_b_r[l, 1], rnn_b_i[l, 1])],
            axis=-1)
        lam = jnp.concatenate([rnn_lambda[l, d].reshape(RNN_BLOCKS, 1, RNN_BLOCK_DIM) for d in range(2)], axis=-1)
        yr = rglru_branch(dims, p, rnn_conv_w[l], rnn_conv_b[l], w_gates, b_gates, lam)

        xbc = ssd_prep(dims, p, ssd_conv_w[l], ssd_conv_b[l])
        yf, yb = ssd_scan(dims, xbc, dt, _group_dt_param(ssd_dt_bias[l]), _group_dt_param(ssd_a_log[l]))
        d_chan = jnp.repeat(ssd_d[l], SSD_HEAD_DIM).reshape(1, SSD_WIDTH)
        ys = ssd_finish(dims, yf, yb, xbc, p, d_chan, ssd_norm[l])

        merged = merge_branches(h, ya, yr, ys, w_gate_all, w_up_all, l, m_rows)
        xs = matmul_residual(dims, merged, w_o_all, l, xs, modtab, 2, m_rows)

        h2, logits_t = norm_mod(dims, xs, g_ffn[l], modtab, 3, m_rows, router_wt)
        cw_t, dest, tab = route(logits_t, router_b)
        dest = dest.reshape(m_rows)
        cw_rows = jnp.pad(cw_t.T, ((0, 0), (0, LANES - SUBLANES)))
        n_slots = m_rows + N_EXPERT_GROUPS * MOE_TILE
        hs, cs = dispatch(dest, h2, cw_rows, n_slots)
        ys2 = moe_grouped(hs, cs, tab[0], tab[1, :1], moe_g_all, moe_u_all, moe_d_all, l)
        xs = combine(dims, dest, xs, ys2, modtab, 5, m_rows)

    out = final_norm(xs, g_final, dims.n_lat)
    return out.reshape(bsz, dims.seq, D_MODEL)


def kernel(x, c, ctx, c_ctx, w_mod_a, w_mod_b, b_mod, g_mix, g_ffn, w_in, w_up, w_o, q_norm, k_norm, rnn_conv_w, rnn_conv_b, rnn_lambda, rnn_w_r, rnn_b_r, rnn_w_i, rnn_b_i, ssd_conv_w, ssd_conv_b, ssd_dt_bias, ssd_a_log, ssd_d, ssd_norm, router_w, router_b, moe_w_gate, moe_w_up, moe_w_down, g_final):
    dims = Dims(batch=x.shape[0], seq=x.shape[1], ctx=ctx.shape[1])
    return run_model(dims, x, c, ctx, c_ctx, w_mod_a, w_mod_b, b_mod, g_mix, g_ffn, w_in, w_up, w_o, q_norm, k_norm,
                     rnn_conv_w, rnn_conv_b, rnn_lambda, rnn_w_r, rnn_b_r, rnn_w_i, rnn_b_i,
                     ssd_conv_w, ssd_conv_b, ssd_dt_bias, ssd_a_log, ssd_d, ssd_norm,
                     router_w, router_b, moe_w_gate, moe_w_up, moe_w_down, g_final)
```

```python
import functools
from typing import NamedTuple

import jax
import jax.numpy as jnp
from jax import lax
from jax.experimental import pallas as pl
from jax.experimental.pallas import tpu as pltpu

F32 = jnp.float32
BF16 = jnp.bfloat16

D_MODEL = 4096
DEPTH = 4
GRID_W = 64
N_BRANCH = 3
BRANCH_WIDTH = 1024
HEAD_DIM = 128
N_Q_HEADS = 8
N_KV_HEADS = 2
Q_PER_KV = 4
KV_WIDTH = 256
N_ROPE_FREQ = 32
ROPE_THETA = 10000.0
RNN_WIDTH = 1024
RNN_BLOCKS = 8
RNN_BLOCK_DIM = 128
LRU_C = 8.0
CONV_W = 4
SSD_WIDTH = 1024
SSD_HEAD_DIM = 64
SSD_HEADS = 16
SSD_GROUPS = 2
SSD_STATE = 128
SSD_CHUNK = 128
SSD_BC = 256
SSD_XBC = 1536
HEADS_PER_GROUP = SSD_HEADS // SSD_GROUPS
GROUP_WIDTH = SSD_WIDTH // SSD_GROUPS
Q_OFF = 0
K_OFF = 1024
V_OFF = 1280
RX_OFF = 1536
RG_OFF = 2560
SZ_OFF = 3584
SX_OFF = 4608
SDT_OFF = 6144
MIX_COLS = 6176
MOD_RANK = 256
N_MOD = 6
N_EXPERTS = 16
N_EXPERT_GROUPS = 4
EXPERTS_PER_GROUP = 4
EXPERT_FF = 512
EPS = 1e-6

LANES = 128
SUBLANES = 8
VMEM_LIMIT_BYTES = 56 * 1024 * 1024
ROW_TILE = 256
SCAN_GROUPS = 4
MOE_TILE = 512
RANK_BLOCK = 256
MM_TILE_M = 512
NEG_BIG = -1e30


class Dims(NamedTuple):
    batch: int
    seq: int
    ctx: int

    @property
    def n_lat(self):
        return self.batch * self.seq

    @property
    def n_tok(self):
        return self.batch * (self.seq + self.ctx)


def _cparams(sem):
    return pltpu.CompilerParams(dimension_semantics=sem, vmem_limit_bytes=VMEM_LIMIT_BYTES)


def _sigmoid(x):
    return 1.0 / (1.0 + jnp.exp(-x))


def _silu(x):
    return x * _sigmoid(x)


def _softplus(x):
    return jnp.maximum(x, 0.0) + jnp.log(1.0 + jnp.exp(-jnp.abs(x)))


def _gelu_tanh(x):
    return 0.5 * x * (1.0 + jnp.tanh(0.7978845608028654 * (x + 0.044715 * (x * x * x))))


def _dot(a, b):
    return jnp.dot(a, b, preferred_element_type=F32)


def _dot_nt(a, b):
    return lax.dot_general(a, b, (((1,), (1,)), ((), ())), preferred_element_type=F32)


def _mod_row(dims, tile, k):
    n_lat_tiles = dims.n_lat // tile
    per_batch = dims.seq // tile

    def f(i):
        return jnp.where(i < n_lat_tiles, 1 + i // per_batch, 0) * N_MOD + k
    return f


def _seg_block(dims, rows):
    lat_pb = dims.seq // rows
    ctx_pb = dims.ctx // rows
    ctx_base = dims.n_lat // rows

    def f(b, r):
        return jnp.where(r < lat_pb, b * lat_pb + r, ctx_base + b * ctx_pb + (r - lat_pb))
    return f


def _adaln_a_kernel(c_ref, w_ref, o_ref):
    c = c_ref[...]
    o_ref[0] = _dot(_silu(c).astype(BF16), w_ref[0].astype(BF16))


def _adaln_b_kernel(a_ref, w_ref, b_ref, o_ref):
    o_ref[0] = _dot(a_ref[0].astype(BF16), w_ref[0].astype(BF16)) + b_ref[0]


def adaln_all(cond, w_a, w_b, b):
    nl = w_a.shape[0]
    a = pl.pallas_call(
        _adaln_a_kernel,
        grid=(nl,),
        in_specs=[pl.BlockSpec((SUBLANES, D_MODEL), lambda l: (0, 0)),
                  pl.BlockSpec((1, D_MODEL, MOD_RANK), lambda l: (l, 0, 0))],
        out_specs=pl.BlockSpec((1, SUBLANES, MOD_RANK), lambda l: (l, 0, 0)),
        out_shape=jax.ShapeDtypeStruct((nl, SUBLANES, MOD_RANK), F32),
        compiler_params=_cparams(("arbitrary",)),
        name="adaln_a",
    )(cond, w_a)
    tn = 4096
    ncol = N_MOD * D_MODEL
    return pl.pallas_call(
        _adaln_b_kernel,
        grid=(nl, ncol // tn),
        in_specs=[pl.BlockSpec((1, SUBLANES, MOD_RANK), lambda l, j: (l, 0, 0)),
                  pl.BlockSpec((1, MOD_RANK, tn), lambda l, j: (l, 0, j)),
                  pl.BlockSpec((1, 1, tn), lambda l, j: (l, 0, j))],
        out_specs=pl.BlockSpec((1, SUBLANES, tn), lambda l, j: (l, 0, j)),
        out_shape=jax.ShapeDtypeStruct((nl, SUBLANES, ncol), F32),
        compiler_params=_cparams(("arbitrary", "arbitrary")),
        name="adaln_b",
    )(a, w_b, b.reshape(nl, 1, ncol))


def _norm_mod_kernel(x_ref, g_ref, sh_ref, sc_ref, o_ref):
    x = x_ref[...]
    y = x * lax.rsqrt(jnp.mean(x * x, axis=-1, keepdims=True) + EPS) * g_ref[...]
    o_ref[...] = (y * (1.0 + sc_ref[0]) + sh_ref[0]).astype(o_ref.dtype)


def _norm_mod_router_kernel(x_ref, g_ref, sh_ref, sc_ref, rw_ref, o_ref, lg_ref):
    x = x_ref[...]
    y = x * lax.rsqrt(jnp.mean(x * x, axis=-1, keepdims=True) + EPS) * g_ref[...]
    h = y * (1.0 + sc_ref[0]) + sh_ref[0]
    o_ref[...] = h
    lg_ref[...] = _dot_nt(rw_ref[...], h.astype(BF16))


def norm_mod(dims, x, gain, modtab, k, m_rows, router_wt=None):
    tm = ROW_TILE
    mrow = _mod_row(dims, tm, k)
    mrow1 = _mod_row(dims, tm, k + 1)
    in_specs = [pl.BlockSpec((tm, D_MODEL), lambda i: (i, 0)),
                pl.BlockSpec((1, D_MODEL), lambda i: (0, 0)),
                pl.BlockSpec((1, 1, D_MODEL), lambda i: (mrow(i), 0, 0)),
                pl.BlockSpec((1, 1, D_MODEL), lambda i: (mrow1(i), 0, 0))]
    h_spec = pl.BlockSpec((tm, D_MODEL), lambda i: (i, 0))
    if router_wt is None:
        return pl.pallas_call(
            _norm_mod_kernel, grid=(m_rows // tm,), in_specs=in_specs, out_specs=h_spec,
            out_shape=jax.ShapeDtypeStruct((m_rows, D_MODEL), BF16),
            compiler_params=_cparams(("arbitrary",)), name="norm_mod",
        )(x, gain.reshape(1, D_MODEL), modtab, modtab)
    return pl.pallas_call(
        _norm_mod_router_kernel, grid=(m_rows // tm,),
        in_specs=in_specs + [pl.BlockSpec((N_EXPERTS, D_MODEL), lambda i: (0, 0))],
        out_specs=[h_spec, pl.BlockSpec((N_EXPERTS, tm), lambda i: (0, i))],
        out_shape=[jax.ShapeDtypeStruct((m_rows, D_MODEL), F32),
                   jax.ShapeDtypeStruct((N_EXPERTS, m_rows), F32)],
        compiler_params=_cparams(("arbitrary",)), name="norm_mod_router",
    )(x, gain.reshape(1, D_MODEL), modtab, modtab, router_wt)


def _final_norm_kernel(x_ref, g_ref, o_ref):
    x = x_ref[...]
    o_ref[...] = x * lax.rsqrt(jnp.mean(x * x, axis=-1, keepdims=True) + EPS) * g_ref[...]


def final_norm(x, gain, m_rows):
    tm = ROW_TILE
    return pl.pallas_call(
        _final_norm_kernel, grid=(m_rows // tm,),
        in_specs=[pl.BlockSpec((tm, D_MODEL), lambda i: (i, 0)),
                  pl.BlockSpec((1, D_MODEL), lambda i: (0, 0))],
        out_specs=pl.BlockSpec((tm, D_MODEL), lambda i: (i, 0)),
        out_shape=jax.ShapeDtypeStruct((m_rows, D_MODEL), F32),
        compiler_params=_cparams(("arbitrary",)), name="final_norm",
    )(x, gain.reshape(1, D_MODEL))


def _mm_kernel(a_ref, w_ref, o_ref):
    o_ref[...] = _dot(a_ref[...], w_ref[0]).astype(o_ref.dtype)


def matmul(a, w_all, l, tn, out_dtype):
    m, k = a.shape
    n = w_all.shape[2]
    tm = MM_TILE_M
    return pl.pallas_call(
        _mm_kernel, grid=(n // tn, m // tm),
        in_specs=[pl.BlockSpec((tm, k), lambda j, i: (i, 0)),
                  pl.BlockSpec((1, k, tn), lambda j, i: (l, 0, j))],
        out_specs=pl.BlockSpec((tm, tn), lambda j, i: (i, j)),
        out_shape=jax.ShapeDtypeStruct((m, n), out_dtype),
        compiler_params=_cparams(("arbitrary", "arbitrary")), name="mm_in",
    )(a, w_all)


def _mm_res_kernel(a_ref, w_ref, x_ref, g_ref, o_ref):
    o_ref[...] = x_ref[...] + g_ref[0] * _dot(a_ref[...], w_ref[0])


def matmul_residual(dims, a, w_all, l, x, modtab, k, m_rows):
    kk = a.shape[1]
    n = w_all.shape[2]
    tm, tn = MM_TILE_M, 512
    mrow = _mod_row(dims, tm, k)
    return pl.pallas_call(
        _mm_res_kernel, grid=(n // tn, m_rows // tm),
        in_specs=[pl.BlockSpec((tm, kk), lambda j, i: (i, 0)),
                  pl.BlockSpec((1, kk, tn), lambda j, i: (l, 0, j)),
                  pl.BlockSpec((tm, tn), lambda j, i: (i, j)),
                  pl.BlockSpec((1, 1, tn), lambda j, i: (mrow(i), 0, j))],
        out_specs=pl.BlockSpec((tm, tn), lambda j, i: (i, j)),
        out_shape=jax.ShapeDtypeStruct((m_rows, n), F32),
        compiler_params=_cparams(("arbitrary", "arbitrary")), name="mm_out_res",
    )(a, w_all, x, modtab)


def _merge_kernel(h_ref, ya_ref, yr_ref, ys_ref, wg0_ref, wg1_ref, wg2_ref,
                  wu0_ref, wu1_ref, wu2_ref, o_ref):
    h = h_ref[...]
    acc = _sigmoid(_dot(h, wg0_ref[0])) * _dot(ya_ref[...], wu0_ref[0, 0])
    acc += _sigmoid(_dot(h, wg1_ref[0])) * _dot(yr_ref[...], wu1_ref[0, 0])
    acc += _sigmoid(_dot(h, wg2_ref[0])) * _dot(ys_ref[...], wu2_ref[0, 0])
    o_ref[...] = acc.astype(o_ref.dtype)


def merge_branches(h, ya, yr, ys, w_gate_all, w_up_all, l, m_rows):
    tm, tn = MM_TILE_M, 256
    nj = D_MODEL // tn
    y_spec = pl.BlockSpec((tm, BRANCH_WIDTH), lambda j, i: (i, 0))

    def wg_spec(n):
        return pl.BlockSpec((1, D_MODEL, tn), lambda j, i: (l, 0, n * nj + j))

    def wu_spec(n):
        return pl.BlockSpec((1, 1, BRANCH_WIDTH, tn), lambda j, i: (l, n, 0, j))

    return pl.pallas_call(
        _merge_kernel, grid=(nj, m_rows // tm),
        in_specs=[pl.BlockSpec((tm, D_MODEL), lambda j, i: (i, 0)), y_spec, y_spec, y_spec,
                  wg_spec(0), wg_spec(1), wg_spec(2), wu_spec(0), wu_spec(1), wu_spec(2)],
        out_specs=pl.BlockSpec((tm, tn), lambda j, i: (i, j)),
        out_shape=jax.ShapeDtypeStruct((m_rows, D_MODEL), BF16),
        compiler_params=_cparams(("arbitrary", "arbitrary")), name="merge",
    )(h, ya, yr, ys, w_gate_all, w_gate_all, w_gate_all, w_up_all, w_up_all, w_up_all)


def _qkv_prep_kernel(q_ref, k_ref, v_ref, c_ref, s0_ref, s1_ref, qg_ref, kg_ref,
                     qo_ref, ko_ref, vo_ref):
    c, s0, s1 = c_ref[...], s0_ref[...], s1_ref[...]

    def head(xh, gain, scale):
        y = xh * lax.rsqrt(jnp.mean(xh * xh, axis=-1, keepdims=True) + EPS) * gain
        r = y * c + pltpu.roll(y, 96, 1) * s0 + pltpu.roll(y, 32, 1) * s1
        return r * scale

    for hh in range(N_Q_HEADS):
        sl = slice(hh * HEAD_DIM, (hh + 1) * HEAD_DIM)
        qo_ref[:, sl] = head(q_ref[:, sl], qg_ref[...], HEAD_DIM ** -0.5).astype(qo_ref.dtype)
    for hh in range(N_KV_HEADS):
        sl = slice(hh * HEAD_DIM, (hh + 1) * HEAD_DIM)
        ko_ref[:, sl] = head(k_ref[:, sl], kg_ref[...], 1.0).astype(ko_ref.dtype)
    vo_ref[...] = v_ref[...].astype(vo_ref.dtype)


def qkv_prep(dims, p, rope_c, rope_s0, rope_s1, q_gain, k_gain):
    tm = ROW_TILE
    n_lat_tiles = dims.n_lat // tm
    per_batch = dims.seq // tm

    def tab(i):
        return (jnp.where(i < n_lat_tiles, i % per_batch, per_batch), 0)

    m = dims.n_tok
    tab_spec = pl.BlockSpec((tm, HEAD_DIM), tab)
    g_spec = pl.BlockSpec((1, HEAD_DIM), lambda i: (0, 0))
    return pl.pallas_call(
        _qkv_prep_kernel, grid=(m // tm,),
        in_specs=[pl.BlockSpec((tm, BRANCH_WIDTH), lambda i: (i, Q_OFF // BRANCH_WIDTH)),
                  pl.BlockSpec((tm, KV_WIDTH), lambda i: (i, K_OFF // KV_WIDTH)),
                  pl.BlockSpec((tm, KV_WIDTH), lambda i: (i, V_OFF // KV_WIDTH)),
                  tab_spec, tab_spec, tab_spec, g_spec, g_spec],
        out_specs=[pl.BlockSpec((tm, BRANCH_WIDTH), lambda i: (i, 0)),
                   pl.BlockSpec((tm, KV_WIDTH), lambda i: (i, 0)),
                   pl.BlockSpec((tm, KV_WIDTH), lambda i: (i, 0))],
        out_shape=[jax.ShapeDtypeStruct((m, BRANCH_WIDTH), BF16),
                   jax.ShapeDtypeStruct((m, KV_WIDTH), BF16),
                   jax.ShapeDtypeStruct((m, KV_WIDTH), BF16)],
        compiler_params=_cparams(("arbitrary",)), name="qkv_prep",
    )(p, p, p, rope_c, rope_s0, rope_s1, q_gain.reshape(1, HEAD_DIM), k_gain.reshape(1, HEAD_DIM))


def _attn_kernel(q_ref, kl_ref, kc_ref, vl_ref, vc_ref, o_ref, *, lat_tiles):
    qi = pl.program_id(2)

    @pl.when(qi < lat_tiles)
    def _():
        for g in range(Q_PER_KV):
            sl = slice(g * HEAD_DIM, (g + 1) * HEAD_DIM)
            q = q_ref[:, sl]
            s_l = _dot_nt(q, kl_ref[...])
            s_c = _dot_nt(q, kc_ref[...])
            mx = jnp.maximum(jnp.max(s_l, axis=-1, keepdims=True), jnp.max(s_c, axis=-1, keepdims=True))
            p_l = jnp.exp(s_l - mx)
            p_c = jnp.exp(s_c - mx)
            den = jnp.sum(p_l, axis=-1, keepdims=True) + jnp.sum(p_c, axis=-1, keepdims=True)
            o = _dot(p_l.astype(BF16), vl_ref[...]) + _dot(p_c.astype(BF16), vc_ref[...])
            o_ref[:, sl] = (o / den).astype(o_ref.dtype)

    @pl.when(qi >= lat_tiles)
    def _():
        for g in range(Q_PER_KV):
            sl = slice(g * HEAD_DIM, (g + 1) * HEAD_DIM)
            s_c = _dot_nt(q_ref[:, sl], kc_ref[...])
            p_c = jnp.exp(s_c - jnp.max(s_c, axis=-1, keepdims=True))
            den = jnp.sum(p_c, axis=-1, keepdims=True)
            o_ref[:, sl] = (_dot(p_c.astype(BF16), vc_ref[...]) / den).astype(o_ref.dtype)


def attention(dims, qn, kn, vn):
    tq = ROW_TILE
    assert dims.ctx == tq
    lat_tiles = dims.seq // tq
    seg = _seg_block(dims, tq)
    ctx_blk = dims.n_lat // dims.ctx
    gw = Q_PER_KV * HEAD_DIM
    return pl.pallas_call(
        functools.partial(_attn_kernel, lat_tiles=lat_tiles),
        grid=(dims.batch, N_KV_HEADS, lat_tiles + 1),
        in_specs=[pl.BlockSpec((tq, gw), lambda b, h, r: (seg(b, r), h)),
                  pl.BlockSpec((dims.seq, HEAD_DIM), lambda b, h, r: (b, h)),
                  pl.BlockSpec((dims.ctx, HEAD_DIM), lambda b, h, r: (ctx_blk + b, h)),
                  pl.BlockSpec((dims.seq, HEAD_DIM), lambda b, h, r: (b, h)),
                  pl.BlockSpec((dims.ctx, HEAD_DIM), lambda b, h, r: (ctx_blk + b, h))],
        out_specs=pl.BlockSpec((tq, gw), lambda b, h, r: (seg(b, r), h)),
        out_shape=jax.ShapeDtypeStruct((dims.n_tok, BRANCH_WIDTH), BF16),
        compiler_params=_cparams(("arbitrary", "arbitrary", "arbitrary")), name="attention",
    )(qn, kn, kn, vn, vn)


def _conv_rows(prev8, cur, next8, w_ref, b_ref):
    rows = cur.shape[0]
    ext = jnp.concatenate([prev8, cur, next8], axis=0)
    y = b_ref[...] + w_ref[0:1, :] * ext[7:7 + rows]
    y = y + w_ref[1:2, :] * cur
    y = y + w_ref[2:3, :] * ext[9:9 + rows]
    y = y + w_ref[3:4, :] * ext[10:10 + rows]
    return y


def _conv_chunk(src_ref, r0, seg_len, rows, w_ref, b_ref):
    cur = src_ref[pl.ds(r0, rows), :]
    p0 = pl.multiple_of(jnp.maximum(r0 - SUBLANES, 0), SUBLANES)
    n0 = pl.multiple_of(jnp.minimum(r0 + rows, seg_len - SUBLANES), SUBLANES)
    prev8 = src_ref[pl.ds(p0, SUBLANES), :] * jnp.where(r0 > 0, 1.0, 0.0)
    next8 = src_ref[pl.ds(n0, SUBLANES), :] * jnp.where(r0 + rows < seg_len, 1.0, 0.0)
    return _conv_rows(prev8, cur, next8, w_ref, b_ref)


def _rglru_kernel(rxl_ref, rxc_ref, rgl_ref, rgc_ref, cw_ref, cb_ref, wg_ref, bg_ref, lam_ref,
                  o_ref, af_s, uf_s, ab_s, ub_s, hf_s, hb_s, *, seq, ctx):
    r = pl.program_id(2)
    tot = seq + ctx
    rows = ROW_TILE
    w = RNN_BLOCK_DIM

    @pl.when(r == 0)
    def _():
        sp_f = _softplus(-lam_ref[0, :, 0:w])
        sp_b = _softplus(-lam_ref[0, :, w:2 * w])

        def gates(x, base):
            z = _dot(x.astype(BF16), wg_ref[0]) + bg_ref[0]
            sg = _sigmoid(z)
            for d, (sp, a_s, u_s) in enumerate(((sp_f, af_s, uf_s), (sp_b, ab_s, ub_s))):
                rg = sg[:, (2 * d) * w:(2 * d + 1) * w]
                ig = sg[:, (2 * d + 1) * w:(2 * d + 2) * w]
                a = jnp.exp(-LRU_C * rg * sp)
                a_s[pl.ds(base, rows), :] = a
                u_s[pl.ds(base, rows), :] = jnp.sqrt(1.0 - a * a) * (ig * x)

        for c0 in range(0, ctx, rows):
            gates(_conv_chunk(rxc_ref, c0, ctx, rows, cw_ref, cb_ref), c0)

        def lat_body(c, carry):
            r0 = pl.multiple_of(c * rows, rows)
            gates(_conv_chunk(rxl_ref, r0, seq, rows, cw_ref, cb_ref), pl.multiple_of(ctx + r0, rows))
            return carry
        lax.fori_loop(0, seq // rows, lat_body, 0)

        row = lax.broadcasted_iota(jnp.int32, (SUBLANES, w), 0)
        blk = SCAN_GROUPS * SUBLANES

        def scan_block(base, carry, a_s, u_s, out_s, reverse):
            a_blk = a_s[pl.ds(base, blk), :]
            u_blk = u_s[pl.ds(base, blk), :]
            hs = [None] * SCAN_GROUPS
            for g in (range(SCAN_GROUPS - 1, -1, -1) if reverse else range(SCAN_GROUPS)):
                a = a_blk[g * SUBLANES:(g + 1) * SUBLANES]
                u = u_blk[g * SUBLANES:(g + 1) * SUBLANES]
                for k in (1, 2, 4):
                    sh = SUBLANES - k if reverse else k
                    m = (row < SUBLANES - k) if reverse else (row >= k)
                    u = u + a * jnp.where(m, pltpu.roll(u, sh, 0), 0.0)
                    a = a * jnp.where(m, pltpu.roll(a, sh, 0), 1.0)
                hs[g] = u + a * carry
                e = 0 if reverse else SUBLANES - 1
                carry = (jnp.broadcast_to(u[e:e + 1, :], (SUBLANES, w))
                         + jnp.broadcast_to(a[e:e + 1, :], (SUBLANES, w)) * carry)
            out_s[pl.ds(base, blk), :] = jnp.concatenate(hs, axis=0)
            return carry

        def both(i, carry, f0, b0):
            cf, cb = carry
            cf = scan_block(pl.multiple_of(f0 + i * blk, blk), cf, af_s, uf_s, hf_s, False)
            cb = scan_block(pl.multiple_of(b0 - (i + 1) * blk, blk), cb, ab_s, ub_s, hb_s, True)
            return cf, cb

        zero = jnp.zeros((SUBLANES, w), F32)
        carry = lax.fori_loop(0, ctx // blk, lambda i, c: both(i, c, 0, ctx), (zero, zero))
        lax.fori_loop(0, seq // blk, lambda i, c: both(i, c, ctx, tot), carry)

        for c0 in range(0, ctx, rows):
            sl = pl.ds(c0, rows)
            hf_s[sl, :] = _gelu_tanh(rgc_ref[sl, :]) * (hf_s[sl, :] + hb_s[sl, :])

        def out_body(c, carry):
            r0 = pl.multiple_of(c * rows, rows)
            sl = pl.ds(pl.multiple_of(ctx + r0, rows), rows)
            hf_s[sl, :] = _gelu_tanh(rgl_ref[pl.ds(r0, rows), :]) * (hf_s[sl, :] + hb_s[sl, :])
            return carry
        lax.fori_loop(0, seq // rows, out_body, 0)

    lat_pieces = seq // rows
    src = jnp.where(r < lat_pieces, ctx + r * rows, (r - lat_pieces) * rows)
    o_ref[...] = hf_s[pl.ds(pl.multiple_of(src, rows), rows), :].astype(o_ref.dtype)


def rglru_branch(dims, p, conv_w, conv_b, w_gates, b_gates, lam):
    rows = ROW_TILE
    w = RNN_BLOCK_DIM
    pieces = (dims.seq + dims.ctx) // rows
    seg = _seg_block(dims, rows)
    ctx_blk = dims.n_lat // dims.ctx
    rx0, rg0 = RX_OFF // w, RG_OFF // w
    tot = dims.seq + dims.ctx
    return pl.pallas_call(
        functools.partial(_rglru_kernel, seq=dims.seq, ctx=dims.ctx),
        grid=(dims.batch, RNN_BLOCKS, pieces),
        in_specs=[pl.BlockSpec((dims.seq, w), lambda b, n, r: (b, rx0 + n)),
                  pl.BlockSpec((dims.ctx, w), lambda b, n, r: (ctx_blk + b, rx0 + n)),
                  pl.BlockSpec((dims.seq, w), lambda b, n, r: (b, rg0 + n)),
                  pl.BlockSpec((dims.ctx, w), lambda b, n, r: (ctx_blk + b, rg0 + n)),
                  pl.BlockSpec((CONV_W, w), lambda b, n, r: (0, n)),
                  pl.BlockSpec((1, w), lambda b, n, r: (0, n)),
                  pl.BlockSpec((1, w, 4 * w), lambda b, n, r: (n, 0, 0)),
                  pl.BlockSpec((1, 1, 4 * w), lambda b, n, r: (n, 0, 0)),
                  pl.BlockSpec((1, 1, 2 * w), lambda b, n, r: (n, 0, 0))],
        out_specs=pl.BlockSpec((rows, w), lambda b, n, r: (seg(b, r), n)),
        out_shape=jax.ShapeDtypeStruct((dims.n_tok, RNN_WIDTH), BF16),
        scratch_shapes=[pltpu.VMEM((tot, w), F32) for _ in range(6)],
        compiler_params=_cparams(("arbitrary", "arbitrary", "arbitrary")), name="rglru",
    )(p, p, p, p, conv_w, conv_b.reshape(1, RNN_WIDTH), w_gates, b_gates, lam)


def _ssd_prep_kernel(cur_ref, prev_ref, next_ref, w_ref, b_ref, o_ref, *, n_lat_tiles, per_batch):
    i = pl.program_id(0)
    j = i % per_batch
    lat = i < n_lat_tiles
    pv = jnp.where(jnp.logical_and(lat, j != 0), 1.0, 0.0)
    nv = jnp.where(jnp.logical_and(lat, j != per_batch - 1), 1.0, 0.0)
    y = _conv_rows(prev_ref[...] * pv, cur_ref[...], next_ref[...] * nv, w_ref, b_ref)
    o_ref[...] = _silu(y)


def ssd_prep(dims, p, conv_w, conv_b):
    tm = ROW_TILE
    assert dims.ctx == tm
    m = dims.n_tok
    hb = tm // SUBLANES
    last8 = m // SUBLANES - 1
    c0 = SX_OFF // SSD_XBC
    return pl.pallas_call(
        functools.partial(_ssd_prep_kernel, n_lat_tiles=dims.n_lat // tm, per_batch=dims.seq // tm),
        grid=(m // tm,),
        in_specs=[pl.BlockSpec((tm, SSD_XBC), lambda i: (i, c0)),
                  pl.BlockSpec((SUBLANES, SSD_XBC), lambda i: (jnp.maximum(i * hb - 1, 0), c0)),
                  pl.BlockSpec((SUBLANES, SSD_XBC), lambda i: (jnp.minimum((i + 1) * hb, last8), c0)),
                  pl.BlockSpec((CONV_W, SSD_XBC), lambda i: (0, 0)),
                  pl.BlockSpec((1, SSD_XBC), lambda i: (0, 0))],
        out_specs=pl.BlockSpec((tm, SSD_XBC), lambda i: (i, 0)),
        out_shape=jax.ShapeDtypeStruct((m, SSD_XBC), F32),
        compiler_params=_cparams(("arbitrary",)), name="ssd_prep",
    )(p, p, p, conv_w, conv_b.reshape(1, SSD_XBC))


def _ssd_chunk(x_ref, b_ref, c_ref, dt_ref, dtb_ref, alog_ref, s_ref, y_ref, reverse):
    q = SSD_CHUNK
    col0 = HEADS_PER_GROUP if reverse else 0
    ri = lax.broadcasted_iota(jnp.int32, (q, q), 0)
    ci = lax.broadcasted_iota(jnp.int32, (q, q), 1)
    tri = (ri <= ci) if reverse else (ri >= ci)
    cum = jnp.where(tri, 1.0, 0.0).astype(BF16)
    ones = jnp.ones((q, q), BF16)
    left = ci < SSD_HEAD_DIM
    top = ri < SSD_HEAD_DIM

    dtc = _softplus(dt_ref[...] + dtb_ref[0])
    a = dtc * (-jnp.exp(alog_ref[0]))
    a_hi = a.astype(BF16)
    r1 = a - a_hi.astype(F32)
    a_mid = r1.astype(BF16)
    a_lo = (r1 - a_mid.astype(F32)).astype(BF16)
    acum = _dot(cum, a_hi) + _dot(cum, a_mid) + _dot(cum, a_lo)
    atot = _dot(ones, a_hi) + _dot(ones, a_mid) + _dot(ones, a_lo)
    acum_t = acum.T

    bm = b_ref[...].astype(BF16)
    cm = c_ref[...].astype(BF16)
    cb = _dot_nt(cm, bm)

    for pair in range(HEADS_PER_GROUP // 2):
        ca = col0 + 2 * pair
        lanes = slice(pair * LANES, (pair + 1) * LANES)
        col_a, col_b = acum[:, ca:ca + 1], acum[:, ca + 1:ca + 2]
        row_a, row_b = acum_t[ca:ca + 1, :], acum_t[ca + 1:ca + 2, :]
        l_a = jnp.exp(jnp.where(tri, col_a - row_a, NEG_BIG))
        l_b = jnp.exp(jnp.where(tri, col_b - row_b, NEG_BIG))
        xdt = x_ref[:, lanes] * jnp.where(left, dtc[:, ca:ca + 1], dtc[:, ca + 1:ca + 2])
        xdt_b = xdt.astype(BF16)
        y_in = jnp.where(left, _dot((cb * l_a).astype(BF16), xdt_b), _dot((cb * l_b).astype(BF16), xdt_b))
        s_old = s_ref[lanes, :]
        y_st = _dot_nt(cm, s_old.astype(BF16)) * jnp.where(left, jnp.exp(col_a), jnp.exp(col_b))
        y_ref[:, lanes] = y_in + y_st
        tot_a, tot_b = atot[:, ca:ca + 1], atot[:, ca + 1:ca + 2]
        xw = xdt * jnp.where(left, jnp.exp(tot_a - col_a), jnp.exp(tot_b - col_b))
        s_new = _dot(xw.T.astype(BF16), bm)
        s_ref[lanes, :] = s_old * jnp.where(top, jnp.exp(tot_a), jnp.exp(tot_b)) + s_new


def _ssd_scan_kernel(xf_ref, bf_ref, cf_ref, dtf_ref, xb_ref, bb_ref, cb_ref, dtb_ref_,
                     bias_ref, alog_ref, yf_ref, yb_ref, sf_s, sb_s):
    @pl.when(pl.program_id(2) == 0)
    def _():
        sf_s[...] = jnp.zeros_like(sf_s)
        sb_s[...] = jnp.zeros_like(sb_s)

    _ssd_chunk(xf_ref, bf_ref, cf_ref, dtf_ref, bias_ref, alog_ref, sf_s, yf_ref, False)
    _ssd_chunk(xb_ref, bb_ref, cb_ref, dtb_ref_, bias_ref, alog_ref, sb_s, yb_ref, True)


def ssd_scan(dims, xbc, dt, dt_bias, a_log):
    q = SSD_CHUNK
    lat_c, ctx_c = dims.seq // q, dims.ctx // q
    ctx_base = dims.n_lat // q
    steps = lat_c + ctx_c

    def cf(b, s):
        return jnp.where(s < ctx_c, ctx_base + b * ctx_c + s, b * lat_c + (s - ctx_c))

    def cbk(b, s):
        return jnp.where(s < ctx_c, ctx_base + b * ctx_c + (ctx_c - 1 - s),
                         b * lat_c + (lat_c - 1 - (s - ctx_c)))

    gw = GROUP_WIDTH
    bcol, ccol = SSD_WIDTH // SSD_STATE, (SSD_WIDTH + SSD_BC) // SSD_STATE

    def specs(cfun):
        return [pl.BlockSpec((q, gw), lambda b, g, s: (cfun(b, s), g)),
                pl.BlockSpec((q, SSD_STATE), lambda b, g, s: (cfun(b, s), bcol + g)),
                pl.BlockSpec((q, SSD_STATE), lambda b, g, s: (cfun(b, s), ccol + g)),
                pl.BlockSpec((q, LANES), lambda b, g, s: (cfun(b, s), g))]

    par_spec = pl.BlockSpec((1, 1, LANES), lambda b, g, s: (g, 0, 0))
    yshape = jax.ShapeDtypeStruct((dims.n_tok, SSD_WIDTH), F32)
    return pl.pallas_call(
        _ssd_scan_kernel, grid=(dims.batch, SSD_GROUPS, steps),
        in_specs=specs(cf) + specs(cbk) + [par_spec, par_spec],
        out_specs=[pl.BlockSpec((q, gw), lambda b, g, s: (cf(b, s), g)),
                   pl.BlockSpec((q, gw), lambda b, g, s: (cbk(b, s), g))],
        out_shape=[yshape, yshape],
        scratch_shapes=[pltpu.VMEM((gw, SSD_STATE), F32), pltpu.VMEM((gw, SSD_STATE), F32)],
        compiler_params=_cparams(("arbitrary", "arbitrary", "arbitrary")), name="ssd_scan",
    )(xbc, xbc, xbc, dt, xbc, xbc, xbc, dt, dt_bias, a_log)


def _ssd_finish_kernel(yf_ref, yb_ref, x_ref, z_ref, d_ref, g_ref, o_ref):
    y = d_ref[...] * x_ref[...] + yf_ref[...] + yb_ref[...]
    gt = y * _silu(z_ref[...])
    o_ref[...] = (gt * lax.rsqrt(jnp.mean(gt * gt, axis=-1, keepdims=True) + EPS) * g_ref[...]).astype(o_ref.dtype)


def ssd_finish(dims, yf, yb, xbc, p, d_chan, norm_g):
    tm = ROW_TILE
    gw = GROUP_WIDTH
    z0 = SZ_OFF // gw
    blk = pl.BlockSpec((tm, gw), lambda i, g: (i, g))
    vec = pl.BlockSpec((1, gw), lambda i, g: (0, g))
    return pl.pallas_call(
        _ssd_finish_kernel, grid=(dims.n_tok // tm, SSD_GROUPS),
        in_specs=[blk, blk, blk, pl.BlockSpec((tm, gw), lambda i, g: (i, z0 + g)), vec, vec],
        out_specs=blk,
        out_shape=jax.ShapeDtypeStruct((dims.n_tok, SSD_WIDTH), BF16),
        compiler_params=_cparams(("arbitrary", "arbitrary")), name="ssd_finish",
    )(yf, yb, xbc, p, d_chan, norm_g.reshape(1, SSD_WIDTH))


def _route_kernel(lg_ref, rb_ref, cw_ref, dest_ref, tab_ref, *, m, tile):
    score = _sigmoid(lg_ref[...])
    sel = score + rb_ref[...]
    v = [sel[e:e + 1, :] for e in range(N_EXPERTS)]
    sc = [score[e:e + 1, :] for e in range(N_EXPERTS)]
    best, best_g = None, None
    for g in range(N_EXPERT_GROUPS):
        vg = v[4 * g:4 * g + 4]
        gs = vg[0] + vg[1]
        for (i, j) in ((0, 2), (0, 3), (1, 2), (1, 3), (2, 3)):
            gs = jnp.maximum(gs, vg[i] + vg[j])
        if g == 0:
            best, best_g = gs, jnp.zeros_like(gs, dtype=jnp.int32)
        else:
            better = gs > best
            best_g = jnp.where(better, g, best_g)
            best = jnp.where(better, gs, best)
    wts = []
    for e in range(N_EXPERTS):
        g = e // EXPERTS_PER_GROUP
        rank = jnp.zeros_like(best_g)
        for k in range(4 * g, 4 * g + 4):
            if k == e:
                continue
            ahead = (v[k] >= v[e]) if k < e else (v[k] > v[e])
            rank = rank + jnp.where(ahead, 1, 0)
        chosen = jnp.logical_and(best_g == g, rank < 2)
        wts.append(jnp.where(chosen, sc[e], 0.0))
    tot = wts[0]
    for e in range(1, N_EXPERTS):
        tot = tot + wts[e]
    for j in range(EXPERTS_PER_GROUP):
        wj = wts[j]
        for g in range(1, N_EXPERT_GROUPS):
            wj = wj + wts[EXPERTS_PER_GROUP * g + j]
        cw_ref[j:j + 1, :] = wj / tot
    cw_ref[EXPERTS_PER_GROUP:SUBLANES, :] = jnp.zeros((SUBLANES - EXPERTS_PER_GROUP, m), F32)

    row8 = lax.broadcasted_iota(jnp.int32, (SUBLANES, m), 0)
    onehot = jnp.where(row8 == best_g, 1.0, 0.0)
    counts = jnp.sum(onehot, axis=-1, keepdims=True)
    padded = jnp.floor((counts + (tile - 1)) * (1.0 / tile)) * tile
    pb = jnp.broadcast_to(padded, (SUBLANES, LANES))
    r128 = lax.broadcasted_iota(jnp.int32, (SUBLANES, LANES), 0)
    off_b = jnp.zeros((SUBLANES, LANES), F32)
    for k in range(1, N_EXPERT_GROUPS):
        off_b = off_b + jnp.where(r128 >= k, pltpu.roll(pb, k, 0), 0.0)
    off = off_b[:, 0:1]
    ri = lax.broadcasted_iota(jnp.int32, (RANK_BLOCK, RANK_BLOCK), 0)
    ci = lax.broadcasted_iota(jnp.int32, (RANK_BLOCK, RANK_BLOCK), 1)
    before = jnp.where(ri < ci, 1.0, 0.0).astype(BF16)
    carry = jnp.zeros((SUBLANES, 1), F32)
    for b in range(m // RANK_BLOCK):
        sl = slice(b * RANK_BLOCK, (b + 1) * RANK_BLOCK)
        ob = onehot[:, sl]
        pre = _dot(ob.astype(BF16), before) + carry
        dest_ref[:, sl] = jnp.sum(ob * (pre + off), axis=0, keepdims=True).astype(jnp.int32)
        carry = carry + jnp.sum(ob, axis=-1, keepdims=True)
    lane = lax.broadcasted_iota(jnp.int32, (SUBLANES, LANES), 1).astype(F32)
    ends = off + padded
    passed = jnp.where(jnp.logical_and(lane * tile >= ends, r128 < N_EXPERT_GROUPS), 1.0, 0.0)
    tile_group = jnp.sum(passed, axis=0, keepdims=True)
    n_used = jnp.sum(pb, axis=0, keepdims=True) * (1.0 / tile)
    group_end = jnp.sum(jnp.where(r128.astype(F32) == tile_group, off + counts, 0.0), axis=0, keepdims=True)
    n_valid = jnp.clip(group_end - lane[0:1, :] * tile, 0.0, float(tile))
    tab = jnp.where(r128 == 0, tile_group,
                    jnp.where(r128 == 1, n_used, jnp.where(r128 == 2, n_valid, 0.0)))
    tab_ref[...] = tab.astype(jnp.int32)


def route(logits_t, router_b):
    m = logits_t.shape[1]
    return pl.pallas_call(
        functools.partial(_route_kernel, m=m, tile=MOE_TILE), grid=(1,),
        in_specs=[pl.BlockSpec((N_EXPERTS, m), lambda i: (0, 0)),
                  pl.BlockSpec((N_EXPERTS, 1), lambda i: (0, 0))],
        out_specs=[pl.BlockSpec((SUBLANES, m), lambda i: (0, 0)),
                   pl.BlockSpec((1, m), lambda i: (0, 0)),
                   pl.BlockSpec((SUBLANES, LANES), lambda i: (0, 0))],
        out_shape=[jax.ShapeDtypeStruct((SUBLANES, m), F32),
                   jax.ShapeDtypeStruct((1, m), jnp.int32),
                   jax.ShapeDtypeStruct((SUBLANES, LANES), jnp.int32)],
        compiler_params=_cparams(("arbitrary",)), name="route",
    )(logits_t, router_b.reshape(N_EXPERTS, 1))


def _invert_kernel(dest_ref, src_ref):
    def clear(s, c):
        src_ref[s] = 0
        return c
    lax.fori_loop(0, src_ref.shape[0], clear, 0)

    def place(t, c):
        src_ref[dest_ref[t]] = t
        return c
    lax.fori_loop(0, dest_ref.shape[0], place, 0)


def invert_slots(dest, n_slots):
    smem = pl.BlockSpec(memory_space=pltpu.SMEM)
    return pl.pallas_call(
        _invert_kernel, in_specs=[smem], out_specs=smem,
        out_shape=jax.ShapeDtypeStruct((n_slots,), jnp.int32),
        name="moe_invert",
    )(dest)


def _moe_group_kernel(tg_ref, nu_ref, nv_ref, src_ref, h_ref, c_ref, wg_ref, wu_ref, wd_ref, o_ref,
                      xbuf, cbuf, xb_s, sem):
    tm = o_ref.shape[0]
    i = pl.program_id(0)
    j = pl.program_id(1)
    f = pl.program_id(2)
    n_used = nu_ref[0]
    active = i < n_used
    first = jnp.logical_and(j == 0, f == 0)
    slot = i % 2

    def gather(tile, buf_slot, wait):
        def body(r, c):
            s = src_ref[tile * tm + r]
            cps = (pltpu.make_async_copy(h_ref.at[pl.ds(s, 1)], xbuf.at[buf_slot, pl.ds(r, 1)], sem.at[0, buf_slot]),
                   pltpu.make_async_copy(c_ref.at[pl.ds(s, 1)], cbuf.at[buf_slot, pl.ds(r, 1)], sem.at[1, buf_slot]))
            for cp in cps:
                if wait:
                    cp.wait()
                else:
                    cp.start()
            return c
        lax.fori_loop(0, tm, body, 0)

    @pl.when(jnp.logical_and(first, i == 0))
    def _():
        gather(0, 0, False)

    @pl.when(jnp.logical_and(first, i + 1 < n_used))
    def _():
        gather(i + 1, (i + 1) % 2, False)

    @pl.when(jnp.logical_and(first, active))
    def _():
        gather(i, slot, True)
        xb_s[...] = xbuf[slot].astype(BF16)

    @pl.when(first)
    def _():
        o_ref[...] = jnp.zeros_like(o_ref)

    @pl.when(active)
    def _():
        xb = xb_s[...]
        g = _dot(xb, wg_ref[0, 0])
        u = _dot(xb, wu_ref[0, 0])
        c = cbuf[slot]
        lane = lax.broadcasted_iota(jnp.int32, c.shape, 1)
        rowi = lax.broadcasted_iota(jnp.int32, c.shape, 0)
        keep = jnp.logical_and(lane == j, rowi < nv_ref[i])
        cw = jnp.sum(jnp.where(keep, c, 0.0), axis=-1, keepdims=True)
        hid = _silu(g) * u * cw
        o_ref[...] += _dot(hid.astype(BF16), wd_ref[0, 0])


def moe_grouped(h, cw_rows, tile_group, n_used, n_valid, src, w_gate_all, w_up_all, w_down_all, l, n_tiles):
    tm = MOE_TILE
    tf = EXPERT_FF // 2
    last_f = EXPERT_FF // tf - 1

    def tile(i, nu):
        return jnp.minimum(i, nu[0] - 1)

    def expert(i, j, tg, nu):
        jj = jnp.where(i < nu[0], j, EXPERTS_PER_GROUP - 1)
        return tg[tile(i, nu)] * EXPERTS_PER_GROUP + jj

    def fblk(i, f, nu):
        return jnp.where(i < nu[0], f, last_f)

    any_spec = pl.BlockSpec(memory_space=pl.ANY)
    return pl.pallas_call(
        _moe_group_kernel,
        grid_spec=pltpu.PrefetchScalarGridSpec(
            num_scalar_prefetch=4, grid=(n_tiles, EXPERTS_PER_GROUP, EXPERT_FF // tf),
            in_specs=[any_spec, any_spec,
                      pl.BlockSpec((1, 1, D_MODEL, tf),
                                   lambda i, j, f, tg, nu, nv, sr: (l, expert(i, j, tg, nu), 0, fblk(i, f, nu))),
                      pl.BlockSpec((1, 1, D_MODEL, tf),
                                   lambda i, j, f, tg, nu, nv, sr: (l, expert(i, j, tg, nu), 0, fblk(i, f, nu))),
                      pl.BlockSpec((1, 1, tf, D_MODEL),
                                   lambda i, j, f, tg, nu, nv, sr: (l, expert(i, j, tg, nu), fblk(i, f, nu), 0))],
            out_specs=pl.BlockSpec((tm, D_MODEL), lambda i, j, f, tg, nu, nv, sr: (i, 0)),
            scratch_shapes=[pltpu.VMEM((2, tm, D_MODEL), F32), pltpu.VMEM((2, tm, LANES), F32),
                            pltpu.VMEM((tm, D_MODEL), BF16), pltpu.SemaphoreType.DMA((2, 2))]),
        out_shape=jax.ShapeDtypeStruct((n_tiles * tm, D_MODEL), F32),
        compiler_params=_cparams(("arbitrary", "arbitrary", "arbitrary")), name="moe_grouped",
    )(tile_group, n_used, n_valid, src, h, cw_rows, w_gate_all, w_up_all, w_down_all)


def _combine_kernel(dest_ref, x_ref, g_ref, ys_ref, o_ref, buf, sem):
    tm = x_ref.shape[0]
    i = pl.program_id(0)
    n = pl.num_programs(0)

    def gather(tile, slot, wait):
        def body(r, c):
            d = dest_ref[tile * tm + r]
            cp = pltpu.make_async_copy(ys_ref.at[pl.ds(d, 1)], buf.at[slot, pl.ds(r, 1)], sem.at[slot])
            if wait:
                cp.wait()
            else:
                cp.start()
            return c
        lax.fori_loop(0, tm, body, 0)

    @pl.when(i == 0)
    def _():
        gather(0, 0, False)

    @pl.when(i + 1 < n)
    def _():
        gather(i + 1, (i + 1) % 2, False)

    slot = i % 2
    gather(i, slot, True)
    o_ref[...] = x_ref[...] + g_ref[0] * buf[slot]


def combine(dims, dest, x, ys, modtab, k, m_rows):
    tm = ROW_TILE
    mrow = _mod_row(dims, tm, k)
    return pl.pallas_call(
        _combine_kernel,
        grid_spec=pltpu.PrefetchScalarGridSpec(
            num_scalar_prefetch=1, grid=(m_rows // tm,),
            in_specs=[pl.BlockSpec((tm, D_MODEL), lambda i, d: (i, 0)),
                      pl.BlockSpec((1, 1, D_MODEL), lambda i, d: (mrow(i), 0, 0)),
                      pl.BlockSpec(memory_space=pl.ANY)],
            out_specs=pl.BlockSpec((tm, D_MODEL), lambda i, d: (i, 0)),
            scratch_shapes=[pltpu.VMEM((2, tm, D_MODEL), F32), pltpu.SemaphoreType.DMA((2,))]),
        out_shape=jax.ShapeDtypeStruct((m_rows, D_MODEL), F32),
        compiler_params=_cparams(("arbitrary",)), name="moe_combine",
    )(dest, x, modtab, ys)


def rope_tables(dims):
    s = dims.seq
    rows = s // GRID_W
    row = jnp.repeat(jnp.arange(rows, dtype=F32), GRID_W)
    col = jnp.tile(jnp.arange(GRID_W, dtype=F32), rows)
    inv = ROPE_THETA ** (-jnp.arange(N_ROPE_FREQ, dtype=F32) / N_ROPE_FREQ)
    ang = jnp.stack([row[:, None] * inv, col[:, None] * inv], axis=1)
    cos, sin = jnp.cos(ang), jnp.sin(ang)
    zero = jnp.zeros_like(sin)
    c = jnp.stack([cos, cos], axis=2).reshape(s, HEAD_DIM)
    s0 = jnp.stack([-sin, zero], axis=2).reshape(s, HEAD_DIM)
    s1 = jnp.stack([zero, sin], axis=2).reshape(s, HEAD_DIM)
    ident = jnp.ones((ROW_TILE, HEAD_DIM), F32)
    zpad = jnp.zeros((ROW_TILE, HEAD_DIM), F32)
    return (jnp.concatenate([c, ident], 0), jnp.concatenate([s0, zpad], 0), jnp.concatenate([s1, zpad], 0))


def _dt_cols():
    cols = []
    for g in range(SSD_GROUPS):
        for d in range(2):
            cols += [SDT_OFF + d * SSD_HEADS + g * HEADS_PER_GROUP + hh for hh in range(HEADS_PER_GROUP)]
    return cols


def _group_dt_param(v):
    out = []
    for g in range(SSD_GROUPS):
        hs = slice(g * HEADS_PER_GROUP, (g + 1) * HEADS_PER_GROUP)
        row = jnp.concatenate([v[0, hs], v[1, hs], jnp.zeros((LANES - 2 * HEADS_PER_GROUP,), v.dtype)])
        out.append(row.reshape(1, LANES))
    return jnp.stack(out, 0)


def _split_w_in_kernel(w_ref, mix_ref, gate_ref, dt_ref):
    mix_ref[0] = w_ref[0, :, :SDT_OFF].astype(BF16)
    gate_ref[0] = w_ref[0, :, MIX_COLS:].astype(BF16)
    dt_ref[0] = w_ref[0, :, SDT_OFF:SDT_OFF + LANES]


def split_w_in(w_in):
    nl, d, n_in = w_in.shape
    rows = 128
    n_gate = n_in - MIX_COLS
    return pl.pallas_call(
        _split_w_in_kernel, grid=(nl, d // rows),
        in_specs=[pl.BlockSpec((1, rows, n_in), lambda l, i: (l, i, 0))],
        out_specs=[pl.BlockSpec((1, rows, SDT_OFF), lambda l, i: (l, i, 0)),
                   pl.BlockSpec((1, rows, n_gate), lambda l, i: (l, i, 0)),
                   pl.BlockSpec((1, rows, LANES), lambda l, i: (l, i, 0))],
        out_shape=[jax.ShapeDtypeStruct((nl, d, SDT_OFF), BF16),
                   jax.ShapeDtypeStruct((nl, d, n_gate), BF16),
                   jax.ShapeDtypeStruct((nl, d, LANES), F32)],
        compiler_params=_cparams(("arbitrary", "arbitrary")), name="split_w_in",
    )(w_in)


def _dt_weight(dt_block):
    nl, d = dt_block.shape[:2]
    pad = jnp.zeros((nl, d, LANES - 2 * HEADS_PER_GROUP), dt_block.dtype)
    dtw = dt_block[:, :, :MIX_COLS - SDT_OFF].reshape(nl, d, 2, SSD_GROUPS, HEADS_PER_GROUP)
    parts = []
    for g in range(SSD_GROUPS):
        parts += [dtw[:, :, :, g, :].reshape(nl, d, 2 * HEADS_PER_GROUP), pad]
    return jnp.concatenate(parts, axis=2).astype(BF16)


def run_model(dims, x, c, ctx, c_ctx, w_mod_a, w_mod_b, b_mod, g_mix, g_ffn, w_in, w_up, w_o, q_norm, k_norm,
              rnn_conv_w, rnn_conv_b, rnn_lambda, rnn_w_r, rnn_b_r, rnn_w_i, rnn_b_i,
              ssd_conv_w, ssd_conv_b, ssd_dt_bias, ssd_a_log, ssd_d, ssd_norm,
              router_w, router_b, moe_w_gate, moe_w_up, moe_w_down, g_final):
    depth = w_in.shape[0]
    bsz = dims.batch
    xs = jnp.concatenate([x.reshape(dims.n_lat, D_MODEL), ctx.reshape(bsz * dims.ctx, D_MODEL)], axis=0)

    cond = jnp.zeros((SUBLANES, D_MODEL), F32).at[0].set(c_ctx).at[1:1 + bsz].set(c)
    mod_all = adaln_all(cond, w_mod_a, w_mod_b, b_mod).reshape(depth, SUBLANES * N_MOD, 1, D_MODEL)
    rope_c, rope_s0, rope_s1 = rope_tables(dims)
    router_wt = router_w.T.astype(BF16)

    w_mix_all, w_gate_all, dt_block = split_w_in(w_in)
    w_dt_all = _dt_weight(dt_block)
    w_up_all = w_up.astype(BF16)
    w_o_all = w_o.astype(BF16)
    moe_g_all = moe_w_gate.astype(BF16)
    moe_u_all = moe_w_up.astype(BF16)
    moe_d_all = moe_w_down.astype(BF16)

    for l in range(depth):
        last = l == depth - 1
        m_rows = dims.n_lat if last else dims.n_tok
        modtab = mod_all[l]

        h = norm_mod(dims, xs, g_mix[l], modtab, 0, dims.n_tok)
        p = matmul(h, w_mix_all, l, 768, F32)
        dt = matmul(h, w_dt_all, l, SSD_GROUPS * LANES, F32)

        qn, kn, vn = qkv_prep(dims, p, rope_c, rope_s0, rope_s1, q_norm[l], k_norm[l])
        ya = attention(dims, qn, kn, vn)

        w_gates = jnp.concatenate([rnn_w_r[l, 0], rnn_w_i[l, 0], rnn_w_r[l, 1], rnn_w_i[l, 1]], axis=-1).astype(BF16)
        b_gates = jnp.concatenate(
            [v.reshape(RNN_BLOCKS, 1, RNN_BLOCK_DIM) for v in (rnn_b_r[l, 0], rnn_b_i[l, 0], rnn_b_r[l, 1], rnn_b_i[l, 1])],
            axis=-1)
        lam = jnp.concatenate([rnn_lambda[l, d].reshape(RNN_BLOCKS, 1, RNN_BLOCK_DIM) for d in range(2)], axis=-1)
        yr = rglru_branch(dims, p, rnn_conv_w[l], rnn_conv_b[l], w_gates, b_gates, lam)

        xbc = ssd_prep(dims, p, ssd_conv_w[l], ssd_conv_b[l])
        yf, yb = ssd_scan(dims, xbc, dt, _group_dt_param(ssd_dt_bias[l]), _group_dt_param(ssd_a_log[l]))
        d_chan = jnp.repeat(ssd_d[l], SSD_HEAD_DIM).reshape(1, SSD_WIDTH)
        ys = ssd_finish(dims, yf, yb, xbc, p, d_chan, ssd_norm[l])

        merged = merge_branches(h, ya, yr, ys, w_gate_all, w_up_all, l, m_rows)
        xs = matmul_residual(dims, merged, w_o_all, l, xs, modtab, 2, m_rows)

        h2, logits_t = norm_mod(dims, xs, g_ffn[l], modtab, 3, m_rows, router_wt)
        cw_t, dest, tab = route(logits_t, router_b)
        dest = dest.reshape(m_rows)
        cw_rows = jnp.pad(cw_t.T, ((0, 0), (0, LANES - SUBLANES)))
        n_tiles = m_rows // MOE_TILE + N_EXPERT_GROUPS
        src = invert_slots(dest, n_tiles * MOE_TILE)
        ys2 = moe_grouped(h2, cw_rows, tab[0], tab[1, :1], tab[2], src, moe_g_all, moe_u_all, moe_d_all, l, n_tiles)
        xs = combine(dims, dest, xs, ys2, modtab, 5, m_rows)

    out = final_norm(xs, g_final, dims.n_lat)
    return out.reshape(bsz, dims.seq, D_MODEL)


def kernel(x, c, ctx, c_ctx, w_mod_a, w_mod_b, b_mod, g_mix, g_ffn, w_in, w_up, w_o, q_norm, k_norm, rnn_conv_w, rnn_conv_b, rnn_lambda, rnn_w_r, rnn_b_r, rnn_w_i, rnn_b_i, ssd_conv_w, ssd_conv_b, ssd_dt_bias, ssd_a_log, ssd_d, ssd_norm, router_w, router_b, moe_w_gate, moe_w_up, moe_w_down, g_final):
    dims = Dims(batch=x.shape[0], seq=x.shape[1], ctx=ctx.shape[1])
    return run_model(dims, x, c, ctx, c_ctx, w_mod_a, w_mod_b, b_mod, g_mix, g_ffn, w_in, w_up, w_o, q_norm, k_norm,
                     rnn_conv_w, rnn_conv_b, rnn_lambda, rnn_w_r, rnn_b_r, rnn_w_i, rnn_b_i,
                     ssd_conv_w, ssd_conv_b, ssd_dt_bias, ssd_a_log, ssd_d, ssd_norm,
                     router_w, router_b, moe_w_gate, moe_w_up, moe_w_down, g_final)
```

```python
import functools
from typing import NamedTuple

import jax
import jax.numpy as jnp
from jax import lax
from jax.experimental import pallas as pl
from jax.experimental.pallas import tpu as pltpu

F32 = jnp.float32
BF16 = jnp.bfloat16

D_MODEL = 4096
DEPTH = 4
GRID_W = 64
N_BRANCH = 3
BRANCH_WIDTH = 1024
HEAD_DIM = 128
N_Q_HEADS = 8
N_KV_HEADS = 2
Q_PER_KV = 4
KV_WIDTH = 256
N_ROPE_FREQ = 32
ROPE_THETA = 10000.0
RNN_WIDTH = 1024
RNN_BLOCKS = 8
RNN_BLOCK_DIM = 128
LRU_C = 8.0
CONV_W = 4
SSD_WIDTH = 1024
SSD_HEAD_DIM = 64
SSD_HEADS = 16
SSD_GROUPS = 2
SSD_STATE = 128
SSD_CHUNK = 128
SSD_BC = 256
SSD_XBC = 1536
HEADS_PER_GROUP = SSD_HEADS // SSD_GROUPS
GROUP_WIDTH = SSD_WIDTH // SSD_GROUPS
Q_OFF = 0
K_OFF = 1024
V_OFF = 1280
RX_OFF = 1536
RG_OFF = 2560
SZ_OFF = 3584
SX_OFF = 4608
SDT_OFF = 6144
MIX_COLS = 6176
MOD_RANK = 256
N_MOD = 6
N_EXPERTS = 16
N_EXPERT_GROUPS = 4
EXPERTS_PER_GROUP = 4
EXPERT_FF = 512
EPS = 1e-6

LANES = 128
SUBLANES = 8
VMEM_LIMIT_BYTES = 56 * 1024 * 1024
ROW_TILE = 256
SCAN_GROUPS = 4
MOE_TILE = 512
RANK_BLOCK = 256
MM_TILE_M = 512
NEG_BIG = -1e30


class Dims(NamedTuple):
    batch: int
    seq: int
    ctx: int

    @property
    def n_lat(self):
        return self.batch * self.seq

    @property
    def n_tok(self):
        return self.batch * (self.seq + self.ctx)


def _cparams(sem):
    return pltpu.CompilerParams(dimension_semantics=sem, vmem_limit_bytes=VMEM_LIMIT_BYTES)


def _sigmoid(x):
    return 1.0 / (1.0 + jnp.exp(-x))


def _silu(x):
    return x * _sigmoid(x)


def _softplus(x):
    return jnp.maximum(x, 0.0) + jnp.log(1.0 + jnp.exp(-jnp.abs(x)))


def _gelu_tanh(x):
    return 0.5 * x * (1.0 + jnp.tanh(0.7978845608028654 * (x + 0.044715 * (x * x * x))))


def _dot(a, b):
    return jnp.dot(a, b, preferred_element_type=F32)


def _dot_nt(a, b):
    return lax.dot_general(a, b, (((1,), (1,)), ((), ())), preferred_element_type=F32)


def _mod_row(dims, tile, k):
    n_lat_tiles = dims.n_lat // tile
    per_batch = dims.seq // tile

    def f(i):
        return jnp.where(i < n_lat_tiles, 1 + i // per_batch, 0) * N_MOD + k
    return f


def _seg_block(dims, rows):
    lat_pb = dims.seq // rows
    ctx_pb = dims.ctx // rows
    ctx_base = dims.n_lat // rows

    def f(b, r):
        return jnp.where(r < lat_pb, b * lat_pb + r, ctx_base + b * ctx_pb + (r - lat_pb))
    return f


def _adaln_a_kernel(c_ref, w_ref, o_ref):
    c = c_ref[...]
    o_ref[0] = _dot(_silu(c).astype(BF16), w_ref[0].astype(BF16))


def _adaln_b_kernel(a_ref, w_ref, b_ref, o_ref):
    o_ref[0] = _dot(a_ref[0].astype(BF16), w_ref[0].astype(BF16)) + b_ref[0]


def adaln_all(cond, w_a, w_b, b):
    nl = w_a.shape[0]
    a = pl.pallas_call(
        _adaln_a_kernel,
        grid=(nl,),
        in_specs=[pl.BlockSpec((SUBLANES, D_MODEL), lambda l: (0, 0)),
                  pl.BlockSpec((1, D_MODEL, MOD_RANK), lambda l: (l, 0, 0))],
        out_specs=pl.BlockSpec((1, SUBLANES, MOD_RANK), lambda l: (l, 0, 0)),
        out_shape=jax.ShapeDtypeStruct((nl, SUBLANES, MOD_RANK), F32),
        compiler_params=_cparams(("arbitrary",)),
        name="adaln_a",
    )(cond, w_a)
    tn = 4096
    ncol = N_MOD * D_MODEL
    return pl.pallas_call(
        _adaln_b_kernel,
        grid=(nl, ncol // tn),
        in_specs=[pl.BlockSpec((1, SUBLANES, MOD_RANK), lambda l, j: (l, 0, 0)),
                  pl.BlockSpec((1, MOD_RANK, tn), lambda l, j: (l, 0, j)),
                  pl.BlockSpec((1, 1, tn), lambda l, j: (l, 0, j))],
        out_specs=pl.BlockSpec((1, SUBLANES, tn), lambda l, j: (l, 0, j)),
        out_shape=jax.ShapeDtypeStruct((nl, SUBLANES, ncol), F32),
        compiler_params=_cparams(("arbitrary", "arbitrary")),
        name="adaln_b",
    )(a, w_b, b.reshape(nl, 1, ncol))


def _norm_mod_kernel(x_ref, g_ref, sh_ref, sc_ref, o_ref):
    x = x_ref[...]
    y = x * lax.rsqrt(jnp.mean(x * x, axis=-1, keepdims=True) + EPS) * g_ref[...]
    o_ref[...] = (y * (1.0 + sc_ref[0]) + sh_ref[0]).astype(o_ref.dtype)


def _norm_mod_router_kernel(x_ref, g_ref, sh_ref, sc_ref, rw_ref, o_ref, lg_ref):
    x = x_ref[...]
    y = x * lax.rsqrt(jnp.mean(x * x, axis=-1, keepdims=True) + EPS) * g_ref[...]
    h = y * (1.0 + sc_ref[0]) + sh_ref[0]
    o_ref[...] = h
    lg_ref[...] = _dot_nt(rw_ref[...], h.astype(BF16))


def norm_mod(dims, x, gain, modtab, k, m_rows, router_wt=None):
    tm = ROW_TILE
    mrow = _mod_row(dims, tm, k)
    mrow1 = _mod_row(dims, tm, k + 1)
    in_specs = [pl.BlockSpec((tm, D_MODEL), lambda i: (i, 0)),
                pl.BlockSpec((1, D_MODEL), lambda i: (0, 0)),
                pl.BlockSpec((1, 1, D_MODEL), lambda i: (mrow(i), 0, 0)),
                pl.BlockSpec((1, 1, D_MODEL), lambda i: (mrow1(i), 0, 0))]
    h_spec = pl.BlockSpec((tm, D_MODEL), lambda i: (i, 0))
    if router_wt is None:
        return pl.pallas_call(
            _norm_mod_kernel, grid=(m_rows // tm,), in_specs=in_specs, out_specs=h_spec,
            out_shape=jax.ShapeDtypeStruct((m_rows, D_MODEL), BF16),
            compiler_params=_cparams(("arbitrary",)), name="norm_mod",
        )(x, gain.reshape(1, D_MODEL), modtab, modtab)
    return pl.pallas_call(
        _norm_mod_router_kernel, grid=(m_rows // tm,),
        in_specs=in_specs + [pl.BlockSpec((N_EXPERTS, D_MODEL), lambda i: (0, 0))],
        out_specs=[h_spec, pl.BlockSpec((N_EXPERTS, tm), lambda i: (0, i))],
        out_shape=[jax.ShapeDtypeStruct((m_rows, D_MODEL), F32),
                   jax.ShapeDtypeStruct((N_EXPERTS, m_rows), F32)],
        compiler_params=_cparams(("arbitrary",)), name="norm_mod_router",
    )(x, gain.reshape(1, D_MODEL), modtab, modtab, router_wt)


def _final_norm_kernel(x_ref, g_ref, o_ref):
    x = x_ref[...]
    o_ref[...] = x * lax.rsqrt(jnp.mean(x * x, axis=-1, keepdims=True) + EPS) * g_ref[...]


def final_norm(x, gain, m_rows):
    tm = ROW_TILE
    return pl.pallas_call(
        _final_norm_kernel, grid=(m_rows // tm,),
        in_specs=[pl.BlockSpec((tm, D_MODEL), lambda i: (i, 0)),
                  pl.BlockSpec((1, D_MODEL), lambda i: (0, 0))],
        out_specs=pl.BlockSpec((tm, D_MODEL), lambda i: (i, 0)),
        out_shape=jax.ShapeDtypeStruct((m_rows, D_MODEL), F32),
        compiler_params=_cparams(("arbitrary",)), name="final_norm",
    )(x, gain.reshape(1, D_MODEL))


def _mm_kernel(a_ref, w_ref, o_ref):
    o_ref[...] = _dot(a_ref[...], w_ref[0]).astype(o_ref.dtype)


def matmul(a, w_all, l, tn, out_dtype):
    m, k = a.shape
    n = w_all.shape[2]
    tm = MM_TILE_M
    return pl.pallas_call(
        _mm_kernel, grid=(n // tn, m // tm),
        in_specs=[pl.BlockSpec((tm, k), lambda j, i: (i, 0)),
                  pl.BlockSpec((1, k, tn), lambda j, i: (l, 0, j))],
        out_specs=pl.BlockSpec((tm, tn), lambda j, i: (i, j)),
        out_shape=jax.ShapeDtypeStruct((m, n), out_dtype),
        compiler_params=_cparams(("arbitrary", "arbitrary")), name="mm_in",
    )(a, w_all)


def _mm_res_kernel(a_ref, w_ref, x_ref, g_ref, o_ref):
    o_ref[...] = x_ref[...] + g_ref[0] * _dot(a_ref[...], w_ref[0])


def matmul_residual(dims, a, w_all, l, x, modtab, k, m_rows):
    kk = a.shape[1]
    n = w_all.shape[2]
    tm, tn = MM_TILE_M, 512
    mrow = _mod_row(dims, tm, k)
    return pl.pallas_call(
        _mm_res_kernel, grid=(n // tn, m_rows // tm),
        in_specs=[pl.BlockSpec((tm, kk), lambda j, i: (i, 0)),
                  pl.BlockSpec((1, kk, tn), lambda j, i: (l, 0, j)),
                  pl.BlockSpec((tm, tn), lambda j, i: (i, j)),
                  pl.BlockSpec((1, 1, tn), lambda j, i: (mrow(i), 0, j))],
        out_specs=pl.BlockSpec((tm, tn), lambda j, i: (i, j)),
        out_shape=jax.ShapeDtypeStruct((m_rows, n), F32),
        compiler_params=_cparams(("arbitrary", "arbitrary")), name="mm_out_res",
    )(a, w_all, x, modtab)


def _merge_kernel(h_ref, ya_ref, yr_ref, ys_ref, wg0_ref, wg1_ref, wg2_ref,
                  wu0_ref, wu1_ref, wu2_ref, o_ref):
    h = h_ref[...]
    acc = _sigmoid(_dot(h, wg0_ref[0])) * _dot(ya_ref[...], wu0_ref[0, 0])
    acc += _sigmoid(_dot(h, wg1_ref[0])) * _dot(yr_ref[...], wu1_ref[0, 0])
    acc += _sigmoid(_dot(h, wg2_ref[0])) * _dot(ys_ref[...], wu2_ref[0, 0])
    o_ref[...] = acc.astype(o_ref.dtype)


def merge_branches(h, ya, yr, ys, w_gate_all, w_up_all, l, m_rows):
    tm, tn = MM_TILE_M, 256
    nj = D_MODEL // tn
    y_spec = pl.BlockSpec((tm, BRANCH_WIDTH), lambda j, i: (i, 0))

    def wg_spec(n):
        return pl.BlockSpec((1, D_MODEL, tn), lambda j, i: (l, 0, n * nj + j))

    def wu_spec(n):
        return pl.BlockSpec((1, 1, BRANCH_WIDTH, tn), lambda j, i: (l, n, 0, j))

    return pl.pallas_call(
        _merge_kernel, grid=(nj, m_rows // tm),
        in_specs=[pl.BlockSpec((tm, D_MODEL), lambda j, i: (i, 0)), y_spec, y_spec, y_spec,
                  wg_spec(0), wg_spec(1), wg_spec(2), wu_spec(0), wu_spec(1), wu_spec(2)],
        out_specs=pl.BlockSpec((tm, tn), lambda j, i: (i, j)),
        out_shape=jax.ShapeDtypeStruct((m_rows, D_MODEL), BF16),
        compiler_params=_cparams(("arbitrary", "arbitrary")), name="merge",
    )(h, ya, yr, ys, w_gate_all, w_gate_all, w_gate_all, w_up_all, w_up_all, w_up_all)


def _qkv_prep_kernel(q_ref, k_ref, v_ref, c_ref, s0_ref, s1_ref, qg_ref, kg_ref,
                     qo_ref, ko_ref, vo_ref):
    c, s0, s1 = c_ref[...], s0_ref[...], s1_ref[...]

    def head(xh, gain, scale):
        y = xh * lax.rsqrt(jnp.mean(xh * xh, axis=-1, keepdims=True) + EPS) * gain
        r = y * c + pltpu.roll(y, 96, 1) * s0 + pltpu.roll(y, 32, 1) * s1
        return r * scale

    for hh in range(N_Q_HEADS):
        sl = slice(hh * HEAD_DIM, (hh + 1) * HEAD_DIM)
        qo_ref[:, sl] = head(q_ref[:, sl], qg_ref[...], HEAD_DIM ** -0.5).astype(qo_ref.dtype)
    for hh in range(N_KV_HEADS):
        sl = slice(hh * HEAD_DIM, (hh + 1) * HEAD_DIM)
        ko_ref[:, sl] = head(k_ref[:, sl], kg_ref[...], 1.0).astype(ko_ref.dtype)
    vo_ref[...] = v_ref[...].astype(vo_ref.dtype)


def qkv_prep(dims, p, rope_c, rope_s0, rope_s1, q_gain, k_gain):
    tm = ROW_TILE
    n_lat_tiles = dims.n_lat // tm
    per_batch = dims.seq // tm

    def tab(i):
        return (jnp.where(i < n_lat_tiles, i % per_batch, per_batch), 0)

    m = dims.n_tok
    tab_spec = pl.BlockSpec((tm, HEAD_DIM), tab)
    g_spec = pl.BlockSpec((1, HEAD_DIM), lambda i: (0, 0))
    return pl.pallas_call(
        _qkv_prep_kernel, grid=(m // tm,),
        in_specs=[pl.BlockSpec((tm, BRANCH_WIDTH), lambda i: (i, Q_OFF // BRANCH_WIDTH)),
                  pl.BlockSpec((tm, KV_WIDTH), lambda i: (i, K_OFF // KV_WIDTH)),
                  pl.BlockSpec((tm, KV_WIDTH), lambda i: (i, V_OFF // KV_WIDTH)),
                  tab_spec, tab_spec, tab_spec, g_spec, g_spec],
        out_specs=[pl.BlockSpec((tm, BRANCH_WIDTH), lambda i: (i, 0)),
                   pl.BlockSpec((tm, KV_WIDTH), lambda i: (i, 0)),
                   pl.BlockSpec((tm, KV_WIDTH), lambda i: (i, 0))],
        out_shape=[jax.ShapeDtypeStruct((m, BRANCH_WIDTH), BF16),
                   jax.ShapeDtypeStruct((m, KV_WIDTH), BF16),
                   jax.ShapeDtypeStruct((m, KV_WIDTH), BF16)],
        compiler_params=_cparams(("arbitrary",)), name="qkv_prep",
    )(p, p, p, rope_c, rope_s0, rope_s1, q_gain.reshape(1, HEAD_DIM), k_gain.reshape(1, HEAD_DIM))


def _attn_kernel(q_ref, kl_ref, kc_ref, vl_ref, vc_ref, o_ref, *, lat_tiles):
    qi = pl.program_id(2)

    @pl.when(qi < lat_tiles)
    def _():
        for g in range(Q_PER_KV):
            sl = slice(g * HEAD_DIM, (g + 1) * HEAD_DIM)
            q = q_ref[:, sl]
            s_l = _dot_nt(q, kl_ref[...])
            s_c = _dot_nt(q, kc_ref[...])
            mx = jnp.maximum(jnp.max(s_l, axis=-1, keepdims=True), jnp.max(s_c, axis=-1, keepdims=True))
            p_l = jnp.exp(s_l - mx)
            p_c = jnp.exp(s_c - mx)
            den = jnp.sum(p_l, axis=-1, keepdims=True) + jnp.sum(p_c, axis=-1, keepdims=True)
            o = _dot(p_l.astype(BF16), vl_ref[...]) + _dot(p_c.astype(BF16), vc_ref[...])
            o_ref[:, sl] = (o / den).astype(o_ref.dtype)

    @pl.when(qi >= lat_tiles)
    def _():
        for g in range(Q_PER_KV):
            sl = slice(g * HEAD_DIM, (g + 1) * HEAD_DIM)
            s_c = _dot_nt(q_ref[:, sl], kc_ref[...])
            p_c = jnp.exp(s_c - jnp.max(s_c, axis=-1, keepdims=True))
            den = jnp.sum(p_c, axis=-1, keepdims=True)
            o_ref[:, sl] = (_dot(p_c.astype(BF16), vc_ref[...]) / den).astype(o_ref.dtype)


def attention(dims, qn, kn, vn):
    tq = ROW_TILE
    assert dims.ctx == tq
    lat_tiles = dims.seq // tq
    seg = _seg_block(dims, tq)
    ctx_blk = dims.n_lat // dims.ctx
    gw = Q_PER_KV * HEAD_DIM
    return pl.pallas_call(
        functools.partial(_attn_kernel, lat_tiles=lat_tiles),
        grid=(dims.batch, N_KV_HEADS, lat_tiles + 1),
        in_specs=[pl.BlockSpec((tq, gw), lambda b, h, r: (seg(b, r), h)),
                  pl.BlockSpec((dims.seq, HEAD_DIM), lambda b, h, r: (b, h)),
                  pl.BlockSpec((dims.ctx, HEAD_DIM), lambda b, h, r: (ctx_blk + b, h)),
                  pl.BlockSpec((dims.seq, HEAD_DIM), lambda b, h, r: (b, h)),
                  pl.BlockSpec((dims.ctx, HEAD_DIM), lambda b, h, r: (ctx_blk + b, h))],
        out_specs=pl.BlockSpec((tq, gw), lambda b, h, r: (seg(b, r), h)),
        out_shape=jax.ShapeDtypeStruct((dims.n_tok, BRANCH_WIDTH), BF16),
        compiler_params=_cparams(("arbitrary", "arbitrary", "arbitrary")), name="attention",
    )(qn, kn, kn, vn, vn)


def _conv_rows(prev8, cur, next8, w_ref, b_ref):
    rows = cur.shape[0]
    ext = jnp.concatenate([prev8, cur, next8], axis=0)
    y = b_ref[...] + w_ref[0:1, :] * ext[7:7 + rows]
    y = y + w_ref[1:2, :] * cur
    y = y + w_ref[2:3, :] * ext[9:9 + rows]
    y = y + w_ref[3:4, :] * ext[10:10 + rows]
    return y


def _conv_chunk(src_ref, r0, seg_len, rows, w_ref, b_ref):
    cur = src_ref[pl.ds(r0, rows), :]
    p0 = pl.multiple_of(jnp.maximum(r0 - SUBLANES, 0), SUBLANES)
    n0 = pl.multiple_of(jnp.minimum(r0 + rows, seg_len - SUBLANES), SUBLANES)
    prev8 = src_ref[pl.ds(p0, SUBLANES), :] * jnp.where(r0 > 0, 1.0, 0.0)
    next8 = src_ref[pl.ds(n0, SUBLANES), :] * jnp.where(r0 + rows < seg_len, 1.0, 0.0)
    return _conv_rows(prev8, cur, next8, w_ref, b_ref)


def _rglru_kernel(rxl_ref, rxc_ref, rgl_ref, rgc_ref, cw_ref, cb_ref, wg_ref, bg_ref, lam_ref,
                  o_ref, af_s, uf_s, ab_s, ub_s, hf_s, hb_s, *, seq, ctx):
    r = pl.program_id(2)
    tot = seq + ctx
    rows = ROW_TILE
    w = RNN_BLOCK_DIM

    @pl.when(r == 0)
    def _():
        sp_f = _softplus(-lam_ref[0, :, 0:w])
        sp_b = _softplus(-lam_ref[0, :, w:2 * w])

        def gates(x, base):
            z = _dot(x.astype(BF16), wg_ref[0]) + bg_ref[0]
            sg = _sigmoid(z)
            for d, (sp, a_s, u_s) in enumerate(((sp_f, af_s, uf_s), (sp_b, ab_s, ub_s))):
                rg = sg[:, (2 * d) * w:(2 * d + 1) * w]
                ig = sg[:, (2 * d + 1) * w:(2 * d + 2) * w]
                a = jnp.exp(-LRU_C * rg * sp)
                a_s[pl.ds(base, rows), :] = a
                u_s[pl.ds(base, rows), :] = jnp.sqrt(1.0 - a * a) * (ig * x)

        for c0 in range(0, ctx, rows):
            gates(_conv_chunk(rxc_ref, c0, ctx, rows, cw_ref, cb_ref), c0)

        def lat_body(c, carry):
            r0 = pl.multiple_of(c * rows, rows)
            gates(_conv_chunk(rxl_ref, r0, seq, rows, cw_ref, cb_ref), pl.multiple_of(ctx + r0, rows))
            return carry
        lax.fori_loop(0, seq // rows, lat_body, 0)

        row = lax.broadcasted_iota(jnp.int32, (SUBLANES, w), 0)
        blk = SCAN_GROUPS * SUBLANES

        def scan_block(base, carry, a_s, u_s, out_s, reverse):
            a_blk = a_s[pl.ds(base, blk), :]
            u_blk = u_s[pl.ds(base, blk), :]
            hs = [None] * SCAN_GROUPS
            for g in (range(SCAN_GROUPS - 1, -1, -1) if reverse else range(SCAN_GROUPS)):
                a = a_blk[g * SUBLANES:(g + 1) * SUBLANES]
                u = u_blk[g * SUBLANES:(g + 1) * SUBLANES]
                for k in (1, 2, 4):
                    sh = SUBLANES - k if reverse else k
                    m = (row < SUBLANES - k) if reverse else (row >= k)
                    u = u + a * jnp.where(m, pltpu.roll(u, sh, 0), 0.0)
                    a = a * jnp.where(m, pltpu.roll(a, sh, 0), 1.0)
                hs[g] = u + a * carry
                e = 0 if reverse else SUBLANES - 1
                carry = (jnp.broadcast_to(u[e:e + 1, :], (SUBLANES, w))
                         + jnp.broadcast_to(a[e:e + 1, :], (SUBLANES, w)) * carry)
            out_s[pl.ds(base, blk), :] = jnp.concatenate(hs, axis=0)
            return carry

        def both(i, carry, f0, b0):
            cf, cb = carry
            cf = scan_block(pl.multiple_of(f0 + i * blk, blk), cf, af_s, uf_s, hf_s, False)
            cb = scan_block(pl.multiple_of(b0 - (i + 1) * blk, blk), cb, ab_s, ub_s, hb_s, True)
            return cf, cb

        zero = jnp.zeros((SUBLANES, w), F32)
        carry = lax.fori_loop(0, ctx // blk, lambda i, c: both(i, c, 0, ctx), (zero, zero))
        lax.fori_loop(0, seq // blk, lambda i, c: both(i, c, ctx, tot), carry)

        for c0 in range(0, ctx, rows):
            sl = pl.ds(c0, rows)
            hf_s[sl, :] = _gelu_tanh(rgc_ref[sl, :]) * (hf_s[sl, :] + hb_s[sl, :])

        def out_body(c, carry):
            r0 = pl.multiple_of(c * rows, rows)
            sl = pl.ds(pl.multiple_of(ctx + r0, rows), rows)
            hf_s[sl, :] = _gelu_tanh(rgl_ref[pl.ds(r0, rows), :]) * (hf_s[sl, :] + hb_s[sl, :])
            return carry
        lax.fori_loop(0, seq // rows, out_body, 0)

    lat_pieces = seq // rows
    src = jnp.where(r < lat_pieces, ctx + r * rows, (r - lat_pieces) * rows)
    o_ref[...] = hf_s[pl.ds(pl.multiple_of(src, rows), rows), :].astype(o_ref.dtype)


def rglru_branch(dims, p, conv_w, conv_b, w_gates, b_gates, lam):
    rows = ROW_TILE
    w = RNN_BLOCK_DIM
    pieces = (dims.seq + dims.ctx) // rows
    seg = _seg_block(dims, rows)
    ctx_blk = dims.n_lat // dims.ctx
    rx0, rg0 = RX_OFF // w, RG_OFF // w
    tot = dims.seq + dims.ctx
    return pl.pallas_call(
        functools.partial(_rglru_kernel, seq=dims.seq, ctx=dims.ctx),
        grid=(dims.batch, RNN_BLOCKS, pieces),
        in_specs=[pl.BlockSpec((dims.seq, w), lambda b, n, r: (b, rx0 + n)),
                  pl.BlockSpec((dims.ctx, w), lambda b, n, r: (ctx_blk + b, rx0 + n)),
                  pl.BlockSpec((dims.seq, w), lambda b, n, r: (b, rg0 + n)),
                  pl.BlockSpec((dims.ctx, w), lambda b, n, r: (ctx_blk + b, rg0 + n)),
                  pl.BlockSpec((CONV_W, w), lambda b, n, r: (0, n)),
                  pl.BlockSpec((1, w), lambda b, n, r: (0, n)),
                  pl.BlockSpec((1, w, 4 * w), lambda b, n, r: (n, 0, 0)),
                  pl.BlockSpec((1, 1, 4 * w), lambda b, n, r: (n, 0, 0)),
                  pl.BlockSpec((1, 1, 2 * w), lambda b, n, r: (n, 0, 0))],
        out_specs=pl.BlockSpec((rows, w), lambda b, n, r: (seg(b, r), n)),
        out_shape=jax.ShapeDtypeStruct((dims.n_tok, RNN_WIDTH), BF16),
        scratch_shapes=[pltpu.VMEM((tot, w), F32) for _ in range(6)],
        compiler_params=_cparams(("arbitrary", "arbitrary", "arbitrary")), name="rglru",
    )(p, p, p, p, conv_w, conv_b.reshape(1, RNN_WIDTH), w_gates, b_gates, lam)


def _ssd_prep_kernel(cur_ref, prev_ref, next_ref, w_ref, b_ref, o_ref, *, n_lat_tiles, per_batch):
    i = pl.program_id(0)
    j = i % per_batch
    lat = i < n_lat_tiles
    pv = jnp.where(jnp.logical_and(lat, j != 0), 1.0, 0.0)
    nv = jnp.where(jnp.logical_and(lat, j != per_batch - 1), 1.0, 0.0)
    y = _conv_rows(prev_ref[...] * pv, cur_ref[...], next_ref[...] * nv, w_ref, b_ref)
    o_ref[...] = _silu(y)


def ssd_prep(dims, p, conv_w, conv_b):
    tm = ROW_TILE
    assert dims.ctx == tm
    m = dims.n_tok
    hb = tm // SUBLANES
    last8 = m // SUBLANES - 1
    c0 = SX_OFF // SSD_XBC
    return pl.pallas_call(
        functools.partial(_ssd_prep_kernel, n_lat_tiles=dims.n_lat // tm, per_batch=dims.seq // tm),
        grid=(m // tm,),
        in_specs=[pl.BlockSpec((tm, SSD_XBC), lambda i: (i, c0)),
                  pl.BlockSpec((SUBLANES, SSD_XBC), lambda i: (jnp.maximum(i * hb - 1, 0), c0)),
                  pl.BlockSpec((SUBLANES, SSD_XBC), lambda i: (jnp.minimum((i + 1) * hb, last8), c0)),
                  pl.BlockSpec((CONV_W, SSD_XBC), lambda i: (0, 0)),
                  pl.BlockSpec((1, SSD_XBC), lambda i: (0, 0))],
        out_specs=pl.BlockSpec((tm, SSD_XBC), lambda i: (i, 0)),
        out_shape=jax.ShapeDtypeStruct((m, SSD_XBC), F32),
        compiler_params=_cparams(("arbitrary",)), name="ssd_prep",
    )(p, p, p, conv_w, conv_b.reshape(1, SSD_XBC))


def _ssd_chunk(x_ref, b_ref, c_ref, dt_ref, dtb_ref, alog_ref, s_ref, y_ref, reverse):
    q = SSD_CHUNK
    col0 = HEADS_PER_GROUP if reverse else 0
    ri = lax.broadcasted_iota(jnp.int32, (q, q), 0)
    ci = lax.broadcasted_iota(jnp.int32, (q, q), 1)
    tri = (ri <= ci) if reverse else (ri >= ci)
    cum = jnp.where(tri, 1.0, 0.0).astype(BF16)
    ones = jnp.ones((q, q), BF16)
    left = ci < SSD_HEAD_DIM
    top = ri < SSD_HEAD_DIM

    dtc = _softplus(dt_ref[...] + dtb_ref[0])
    a = dtc * (-jnp.exp(alog_ref[0]))
    a_hi = a.astype(BF16)
    r1 = a - a_hi.astype(F32)
    a_mid = r1.astype(BF16)
    a_lo = (r1 - a_mid.astype(F32)).astype(BF16)
    acum = _dot(cum, a_hi) + _dot(cum, a_mid) + _dot(cum, a_lo)
    atot = _dot(ones, a_hi) + _dot(ones, a_mid) + _dot(ones, a_lo)
    acum_t = acum.T

    bm = b_ref[...].astype(BF16)
    cm = c_ref[...].astype(BF16)
    cb = _dot_nt(cm, bm)

    for pair in range(HEADS_PER_GROUP // 2):
        ca = col0 + 2 * pair
        lanes = slice(pair * LANES, (pair + 1) * LANES)
        col_a, col_b = acum[:, ca:ca + 1], acum[:, ca + 1:ca + 2]
        row_a, row_b = acum_t[ca:ca + 1, :], acum_t[ca + 1:ca + 2, :]
        l_a = jnp.exp(jnp.where(tri, col_a - row_a, NEG_BIG))
        l_b = jnp.exp(jnp.where(tri, col_b - row_b, NEG_BIG))
        xdt = x_ref[:, lanes] * jnp.where(left, dtc[:, ca:ca + 1], dtc[:, ca + 1:ca + 2])
        xdt_b = xdt.astype(BF16)
        y_in = jnp.where(left, _dot((cb * l_a).astype(BF16), xdt_b), _dot((cb * l_b).astype(BF16), xdt_b))
        s_old = s_ref[lanes, :]
        y_st = _dot_nt(cm, s_old.astype(BF16)) * jnp.where(left, jnp.exp(col_a), jnp.exp(col_b))
        y_ref[:, lanes] = y_in + y_st
        tot_a, tot_b = atot[:, ca:ca + 1], atot[:, ca + 1:ca + 2]
        xw = xdt * jnp.where(left, jnp.exp(tot_a - col_a), jnp.exp(tot_b - col_b))
        s_new = _dot(xw.T.astype(BF16), bm)
        s_ref[lanes, :] = s_old * jnp.where(top, jnp.exp(tot_a), jnp.exp(tot_b)) + s_new


def _ssd_scan_kernel(xf_ref, bf_ref, cf_ref, dtf_ref, xb_ref, bb_ref, cb_ref, dtb_ref_,
                     bias_ref, alog_ref, yf_ref, yb_ref, sf_s, sb_s):
    @pl.when(pl.program_id(2) == 0)
    def _():
        sf_s[...] = jnp.zeros_like(sf_s)
        sb_s[...] = jnp.zeros_like(sb_s)

    _ssd_chunk(xf_ref, bf_ref, cf_ref, dtf_ref, bias_ref, alog_ref, sf_s, yf_ref, False)
    _ssd_chunk(xb_ref, bb_ref, cb_ref, dtb_ref_, bias_ref, alog_ref, sb_s, yb_ref, True)


def ssd_scan(dims, xbc, dt, dt_bias, a_log):
    q = SSD_CHUNK
    lat_c, ctx_c = dims.seq // q, dims.ctx // q
    ctx_base = dims.n_lat // q
    steps = lat_c + ctx_c

    def cf(b, s):
        return jnp.where(s < ctx_c, ctx_base + b * ctx_c + s, b * lat_c + (s - ctx_c))

    def cbk(b, s):
        return jnp.where(s < ctx_c, ctx_base + b * ctx_c + (ctx_c - 1 - s),
                         b * lat_c + (lat_c - 1 - (s - ctx_c)))

    gw = GROUP_WIDTH
    bcol, ccol = SSD_WIDTH // SSD_STATE, (SSD_WIDTH + SSD_BC) // SSD_STATE

    def specs(cfun):
        return [pl.BlockSpec((q, gw), lambda b, g, s: (cfun(b, s), g)),
                pl.BlockSpec((q, SSD_STATE), lambda b, g, s: (cfun(b, s), bcol + g)),
                pl.BlockSpec((q, SSD_STATE), lambda b, g, s: (cfun(b, s), ccol + g)),
                pl.BlockSpec((q, LANES), lambda b, g, s: (cfun(b, s), g))]

    par_spec = pl.BlockSpec((1, 1, LANES), lambda b, g, s: (g, 0, 0))
    yshape = jax.ShapeDtypeStruct((dims.n_tok, SSD_WIDTH), F32)
    return pl.pallas_call(
        _ssd_scan_kernel, grid=(dims.batch, SSD_GROUPS, steps),
        in_specs=specs(cf) + specs(cbk) + [par_spec, par_spec],
        out_specs=[pl.BlockSpec((q, gw), lambda b, g, s: (cf(b, s), g)),
                   pl.BlockSpec((q, gw), lambda b, g, s: (cbk(b, s), g))],
        out_shape=[yshape, yshape],
        scratch_shapes=[pltpu.VMEM((gw, SSD_STATE), F32), pltpu.VMEM((gw, SSD_STATE), F32)],
        compiler_params=_cparams(("arbitrary", "arbitrary", "arbitrary")), name="ssd_scan",
    )(xbc, xbc, xbc, dt, xbc, xbc, xbc, dt, dt_bias, a_log)


def _ssd_finish_kernel(yf_ref, yb_ref, x_ref, z_ref, d_ref, g_ref, o_ref):
    y = d_ref[...] * x_ref[...] + yf_ref[...] + yb_ref[...]
    gt = y * _silu(z_ref[...])
    o_ref[...] = (gt * lax.rsqrt(jnp.mean(gt * gt, axis=-1, keepdims=True) + EPS) * g_ref[...]).astype(o_ref.dtype)


def ssd_finish(dims, yf, yb, xbc, p, d_chan, norm_g):
    tm = ROW_TILE
    gw = GROUP_WIDTH
    z0 = SZ_OFF // gw
    blk = pl.BlockSpec((tm, gw), lambda i, g: (i, g))
    vec = pl.BlockSpec((1, gw), lambda i, g: (0, g))
    return pl.pallas_call(
        _ssd_finish_kernel, grid=(dims.n_tok // tm, SSD_GROUPS),
        in_specs=[blk, blk, blk, pl.BlockSpec((tm, gw), lambda i, g: (i, z0 + g)), vec, vec],
        out_specs=blk,
        out_shape=jax.ShapeDtypeStruct((dims.n_tok, SSD_WIDTH), BF16),
        compiler_params=_cparams(("arbitrary", "arbitrary")), name="ssd_finish",
    )(yf, yb, xbc, p, d_chan, norm_g.reshape(1, SSD_WIDTH))


def _route_kernel(lg_ref, rb_ref, cw_ref, dest_ref, tab_ref, *, m, tile):
    score = _sigmoid(lg_ref[...])
    sel = score + rb_ref[...]
    v = [sel[e:e + 1, :] for e in range(N_EXPERTS)]
    sc = [score[e:e + 1, :] for e in range(N_EXPERTS)]
    best, best_g = None, None
    for g in range(N_EXPERT_GROUPS):
        vg = v[4 * g:4 * g + 4]
        gs = vg[0] + vg[1]
        for (i, j) in ((0, 2), (0, 3), (1, 2), (1, 3), (2, 3)):
            gs = jnp.maximum(gs, vg[i] + vg[j])
        if g == 0:
            best, best_g = gs, jnp.zeros_like(gs, dtype=jnp.int32)
        else:
            better = gs > best
            best_g = jnp.where(better, g, best_g)
            best = jnp.where(better, gs, best)
    wts = []
    for e in range(N_EXPERTS):
        g = e // EXPERTS_PER_GROUP
        rank = jnp.zeros_like(best_g)
        for k in range(4 * g, 4 * g + 4):
            if k == e:
                continue
            ahead = (v[k] >= v[e]) if k < e else (v[k] > v[e])
            rank = rank + jnp.where(ahead, 1, 0)
        chosen = jnp.logical_and(best_g == g, rank < 2)
        wts.append(jnp.where(chosen, sc[e], 0.0))
    tot = wts[0]
    for e in range(1, N_EXPERTS):
        tot = tot + wts[e]
    for j in range(EXPERTS_PER_GROUP):
        wj = wts[j]
        for g in range(1, N_EXPERT_GROUPS):
            wj = wj + wts[EXPERTS_PER_GROUP * g + j]
        cw_ref[j:j + 1, :] = wj / tot
    cw_ref[EXPERTS_PER_GROUP:SUBLANES, :] = jnp.zeros((SUBLANES - EXPERTS_PER_GROUP, m), F32)

    row8 = lax.broadcasted_iota(jnp.int32, (SUBLANES, m), 0)
    onehot = jnp.where(row8 == best_g, 1.0, 0.0)
    counts = jnp.sum(onehot, axis=-1, keepdims=True)
    padded = jnp.floor((counts + (tile - 1)) * (1.0 / tile)) * tile
    pb = jnp.broadcast_to(padded, (SUBLANES, LANES))
    r128 = lax.broadcasted_iota(jnp.int32, (SUBLANES, LANES), 0)
    off_b = jnp.zeros((SUBLANES, LANES), F32)
    for k in range(1, N_EXPERT_GROUPS):
        off_b = off_b + jnp.where(r128 >= k, pltpu.roll(pb, k, 0), 0.0)
    off = off_b[:, 0:1]
    ri = lax.broadcasted_iota(jnp.int32, (RANK_BLOCK, RANK_BLOCK), 0)
    ci = lax.broadcasted_iota(jnp.int32, (RANK_BLOCK, RANK_BLOCK), 1)
    before = jnp.where(ri < ci, 1.0, 0.0).astype(BF16)
    carry = jnp.zeros((SUBLANES, 1), F32)
    for b in range(m // RANK_BLOCK):
        sl = slice(b * RANK_BLOCK, (b + 1) * RANK_BLOCK)
        ob = onehot[:, sl]
        pre = _dot(ob.astype(BF16), before) + carry
        dest_ref[:, sl] = jnp.sum(ob * (pre + off), axis=0, keepdims=True).astype(jnp.int32)
        carry = carry + jnp.sum(ob, axis=-1, keepdims=True)
    lane = lax.broadcasted_iota(jnp.int32, (SUBLANES, LANES), 1).astype(F32)
    ends = off + padded
    passed = jnp.where(jnp.logical_and(lane * tile >= ends, r128 < N_EXPERT_GROUPS), 1.0, 0.0)
    tile_group = jnp.sum(passed, axis=0, keepdims=True)
    n_used = jnp.sum(pb, axis=0, keepdims=True) * (1.0 / tile)
    group_end = jnp.sum(jnp.where(r128.astype(F32) == tile_group, off + counts, 0.0), axis=0, keepdims=True)
    n_valid = jnp.clip(group_end - lane[0:1, :] * tile, 0.0, float(tile))
    tab = jnp.where(r128 == 0, tile_group,
                    jnp.where(r128 == 1, n_used, jnp.where(r128 == 2, n_valid, 0.0)))
    tab_ref[...] = tab.astype(jnp.int32)


def route(logits_t, router_b):
    m = logits_t.shape[1]
    return pl.pallas_call(
        functools.partial(_route_kernel, m=m, tile=MOE_TILE), grid=(1,),
        in_specs=[pl.BlockSpec((N_EXPERTS, m), lambda i: (0, 0)),
                  pl.BlockSpec((N_EXPERTS, 1), lambda i: (0, 0))],
        out_specs=[pl.BlockSpec((SUBLANES, m), lambda i: (0, 0)),
                   pl.BlockSpec((1, m), lambda i: (0, 0)),
                   pl.BlockSpec((SUBLANES, LANES), lambda i: (0, 0))],
        out_shape=[jax.ShapeDtypeStruct((SUBLANES, m), F32),
                   jax.ShapeDtypeStruct((1, m), jnp.int32),
                   jax.ShapeDtypeStruct((SUBLANES, LANES), jnp.int32)],
        compiler_params=_cparams(("arbitrary",)), name="route",
    )(logits_t, router_b.reshape(N_EXPERTS, 1))


def _invert_kernel(dest_ref, src_ref):
    def clear(s, c):
        src_ref[s] = 0
        return c
    lax.fori_loop(0, src_ref.shape[0], clear, 0, unroll=8)

    def place(t, c):
        src_ref[dest_ref[t]] = t
        return c
    lax.fori_loop(0, dest_ref.shape[0], place, 0, unroll=8)


def invert_slots(dest, n_slots):
    smem = pl.BlockSpec(memory_space=pltpu.SMEM)
    return pl.pallas_call(
        _invert_kernel, in_specs=[smem], out_specs=smem,
        out_shape=jax.ShapeDtypeStruct((n_slots,), jnp.int32),
        name="moe_invert",
    )(dest)


def _moe_group_kernel(tg_ref, nu_ref, nv_ref, src_ref, h_ref, c_ref, wg_ref, wu_ref, wd_ref, o_ref,
                      xbuf, cbuf, xb_s, sem):
    tm = o_ref.shape[0]
    i = pl.program_id(0)
    j = pl.program_id(1)
    f = pl.program_id(2)
    n_used = nu_ref[0]
    active = i < n_used
    first = jnp.logical_and(j == 0, f == 0)
    slot = i % 2

    def gather(tile, buf_slot, wait):
        def body(r, c):
            s = src_ref[tile * tm + r]
            cps = (pltpu.make_async_copy(h_ref.at[pl.ds(s, 1)], xbuf.at[buf_slot, pl.ds(r, 1)], sem.at[0, buf_slot]),
                   pltpu.make_async_copy(c_ref.at[pl.ds(s, 1)], cbuf.at[buf_slot, pl.ds(r, 1)], sem.at[1, buf_slot]))
            for cp in cps:
                if wait:
                    cp.wait()
                else:
                    cp.start()
            return c
        lax.fori_loop(0, tm, body, 0, unroll=8)

    @pl.when(jnp.logical_and(first, i == 0))
    def _():
        gather(0, 0, False)

    @pl.when(jnp.logical_and(first, i + 1 < n_used))
    def _():
        gather(i + 1, (i + 1) % 2, False)

    @pl.when(jnp.logical_and(first, active))
    def _():
        gather(i, slot, True)
        xb_s[...] = xbuf[slot].astype(BF16)

    @pl.when(first)
    def _():
        o_ref[...] = jnp.zeros_like(o_ref)

    @pl.when(active)
    def _():
        xb = xb_s[...]
        g = _dot(xb, wg_ref[0, 0])
        u = _dot(xb, wu_ref[0, 0])
        c = cbuf[slot]
        lane = lax.broadcasted_iota(jnp.int32, c.shape, 1)
        rowi = lax.broadcasted_iota(jnp.int32, c.shape, 0)
        keep = jnp.logical_and(lane == j, rowi < nv_ref[i])
        cw = jnp.sum(jnp.where(keep, c, 0.0), axis=-1, keepdims=True)
        hid = _silu(g) * u * cw
        o_ref[...] += _dot(hid.astype(BF16), wd_ref[0, 0])


def moe_grouped(h, cw_rows, tile_group, n_used, n_valid, src, w_gate_all, w_up_all, w_down_all, l, n_tiles):
    tm = MOE_TILE
    tf = EXPERT_FF // 2
    last_f = EXPERT_FF // tf - 1

    def tile(i, nu):
        return jnp.minimum(i, nu[0] - 1)

    def expert(i, j, tg, nu):
        jj = jnp.where(i < nu[0], j, EXPERTS_PER_GROUP - 1)
        return tg[tile(i, nu)] * EXPERTS_PER_GROUP + jj

    def fblk(i, f, nu):
        return jnp.where(i < nu[0], f, last_f)

    any_spec = pl.BlockSpec(memory_space=pl.ANY)
    return pl.pallas_call(
        _moe_group_kernel,
        grid_spec=pltpu.PrefetchScalarGridSpec(
            num_scalar_prefetch=4, grid=(n_tiles, EXPERTS_PER_GROUP, EXPERT_FF // tf),
            in_specs=[any_spec, any_spec,
                      pl.BlockSpec((1, 1, D_MODEL, tf),
                                   lambda i, j, f, tg, nu, nv, sr: (l, expert(i, j, tg, nu), 0, fblk(i, f, nu))),
                      pl.BlockSpec((1, 1, D_MODEL, tf),
                                   lambda i, j, f, tg, nu, nv, sr: (l, expert(i, j, tg, nu), 0, fblk(i, f, nu))),
                      pl.BlockSpec((1, 1, tf, D_MODEL),
                                   lambda i, j, f, tg, nu, nv, sr: (l, expert(i, j, tg, nu), fblk(i, f, nu), 0))],
            out_specs=pl.BlockSpec((tm, D_MODEL), lambda i, j, f, tg, nu, nv, sr: (i, 0)),
            scratch_shapes=[pltpu.VMEM((2, tm, D_MODEL), F32), pltpu.VMEM((2, tm, LANES), F32),
                            pltpu.VMEM((tm, D_MODEL), BF16), pltpu.SemaphoreType.DMA((2, 2))]),
        out_shape=jax.ShapeDtypeStruct((n_tiles * tm, D_MODEL), F32),
        compiler_params=_cparams(("arbitrary", "arbitrary", "arbitrary")), name="moe_grouped",
    )(tile_group, n_used, n_valid, src, h, cw_rows, w_gate_all, w_up_all, w_down_all)


def _combine_kernel(dest_ref, x_ref, g_ref, ys_ref, o_ref, buf, sem):
    tm = x_ref.shape[0]
    i = pl.program_id(0)
    n = pl.num_programs(0)

    def gather(tile, slot, wait):
        def body(r, c):
            d = dest_ref[tile * tm + r]
            cp = pltpu.make_async_copy(ys_ref.at[pl.ds(d, 1)], buf.at[slot, pl.ds(r, 1)], sem.at[slot])
            if wait:
                cp.wait()
            else:
                cp.start()
            return c
        lax.fori_loop(0, tm, body, 0, unroll=8)

    @pl.when(i == 0)
    def _():
        gather(0, 0, False)

    @pl.when(i + 1 < n)
    def _():
        gather(i + 1, (i + 1) % 2, False)

    slot = i % 2
    gather(i, slot, True)
    o_ref[...] = x_ref[...] + g_ref[0] * buf[slot]


def combine(dims, dest, x, ys, modtab, k, m_rows):
    tm = ROW_TILE
    mrow = _mod_row(dims, tm, k)
    return pl.pallas_call(
        _combine_kernel,
        grid_spec=pltpu.PrefetchScalarGridSpec(
            num_scalar_prefetch=1, grid=(m_rows // tm,),
            in_specs=[pl.BlockSpec((tm, D_MODEL), lambda i, d: (i, 0)),
                      pl.BlockSpec((1, 1, D_MODEL), lambda i, d: (mrow(i), 0, 0)),
                      pl.BlockSpec(memory_space=pl.ANY)],
            out_specs=pl.BlockSpec((tm, D_MODEL), lambda i, d: (i, 0)),
            scratch_shapes=[pltpu.VMEM((2, tm, D_MODEL), F32), pltpu.SemaphoreType.DMA((2,))]),
        out_shape=jax.ShapeDtypeStruct((m_rows, D_MODEL), F32),
        compiler_params=_cparams(("arbitrary",)), name="moe_combine",
    )(dest, x, modtab, ys)


def rope_tables(dims):
    s = dims.seq
    rows = s // GRID_W
    row = jnp.repeat(jnp.arange(rows, dtype=F32), GRID_W)
    col = jnp.tile(jnp.arange(GRID_W, dtype=F32), rows)
    inv = ROPE_THETA ** (-jnp.arange(N_ROPE_FREQ, dtype=F32) / N_ROPE_FREQ)
    ang = jnp.stack([row[:, None] * inv, col[:, None] * inv], axis=1)
    cos, sin = jnp.cos(ang), jnp.sin(ang)
    zero = jnp.zeros_like(sin)
    c = jnp.stack([cos, cos], axis=2).reshape(s, HEAD_DIM)
    s0 = jnp.stack([-sin, zero], axis=2).reshape(s, HEAD_DIM)
    s1 = jnp.stack([zero, sin], axis=2).reshape(s, HEAD_DIM)
    ident = jnp.ones((ROW_TILE, HEAD_DIM), F32)
    zpad = jnp.zeros((ROW_TILE, HEAD_DIM), F32)
    return (jnp.concatenate([c, ident], 0), jnp.concatenate([s0, zpad], 0), jnp.concatenate([s1, zpad], 0))


def _dt_cols():
    cols = []
    for g in range(SSD_GROUPS):
        for d in range(2):
            cols += [SDT_OFF + d * SSD_HEADS + g * HEADS_PER_GROUP + hh for hh in range(HEADS_PER_GROUP)]
    return cols


def _group_dt_param(v):
    out = []
    for g in range(SSD_GROUPS):
        hs = slice(g * HEADS_PER_GROUP, (g + 1) * HEADS_PER_GROUP)
        row = jnp.concatenate([v[0, hs], v[1, hs], jnp.zeros((LANES - 2 * HEADS_PER_GROUP,), v.dtype)])
        out.append(row.reshape(1, LANES))
    return jnp.stack(out, 0)


def _transpose_cast_kernel(w_ref, o_ref):
    o_ref[0] = w_ref[0].T.astype(o_ref.dtype)


def transpose_cast(w_t, first_row, n_rows, block, out_dtype):
    nl, _, k = w_t.shape
    assert first_row % (2 * SUBLANES) == 0 and n_rows % block == 0
    return pl.pallas_call(
        _transpose_cast_kernel, grid=(nl, n_rows // block),
        in_specs=[pl.BlockSpec((pl.Element(1), pl.Element(block), pl.Element(k)),
                               lambda l, j: (l, pl.multiple_of(first_row + j * block, 2 * SUBLANES), 0))],
        out_specs=pl.BlockSpec((1, k, block), lambda l, j: (l, 0, j)),
        out_shape=jax.ShapeDtypeStruct((nl, k, n_rows), out_dtype),
        compiler_params=_cparams(("arbitrary", "arbitrary")), name="transpose_cast",
    )(w_t)


def split_w_in(w_in):
    w_t = jnp.swapaxes(w_in, 1, 2)
    n_gate = w_in.shape[2] - MIX_COLS
    return (transpose_cast(w_t, 0, SDT_OFF, 256, BF16),
            transpose_cast(w_t, MIX_COLS, n_gate, 256, BF16),
            transpose_cast(w_t, SDT_OFF, LANES, LANES, F32))


def _dt_weight(dt_block):
    nl, d = dt_block.shape[:2]
    pad = jnp.zeros((nl, d, LANES - 2 * HEADS_PER_GROUP), dt_block.dtype)
    dtw = dt_block[:, :, :MIX_COLS - SDT_OFF].reshape(nl, d, 2, SSD_GROUPS, HEADS_PER_GROUP)
    parts = []
    for g in range(SSD_GROUPS):
        parts += [dtw[:, :, :, g, :].reshape(nl, d, 2 * HEADS_PER_GROUP), pad]
    return jnp.concatenate(parts, axis=2).astype(BF16)


def run_model(dims, x, c, ctx, c_ctx, w_mod_a, w_mod_b, b_mod, g_mix, g_ffn, w_in, w_up, w_o, q_norm, k_norm,
              rnn_conv_w, rnn_conv_b, rnn_lambda, rnn_w_r, rnn_b_r, rnn_w_i, rnn_b_i,
              ssd_conv_w, ssd_conv_b, ssd_dt_bias, ssd_a_log, ssd_d, ssd_norm,
              router_w, router_b, moe_w_gate, moe_w_up, moe_w_down, g_final):
    depth = w_in.shape[0]
    bsz = dims.batch
    xs = jnp.concatenate([x.reshape(dims.n_lat, D_MODEL), ctx.reshape(bsz * dims.ctx, D_MODEL)], axis=0)

    cond = jnp.zeros((SUBLANES, D_MODEL), F32).at[0].set(c_ctx).at[1:1 + bsz].set(c)
    mod_all = adaln_all(cond, w_mod_a, w_mod_b, b_mod).reshape(depth, SUBLANES * N_MOD, 1, D_MODEL)
    rope_c, rope_s0, rope_s1 = rope_tables(dims)
    router_wt = router_w.T.astype(BF16)

    w_mix_all, w_gate_all, dt_block = split_w_in(w_in)
    w_dt_all = _dt_weight(dt_block)
    w_up_all = w_up.astype(BF16)
    w_o_all = w_o.astype(BF16)
    moe_g_all = moe_w_gate.astype(BF16)
    moe_u_all = moe_w_up.astype(BF16)
    moe_d_all = moe_w_down.astype(BF16)

    for l in range(depth):
        last = l == depth - 1
        m_rows = dims.n_lat if last else dims.n_tok
        modtab = mod_all[l]

        h = norm_mod(dims, xs, g_mix[l], modtab, 0, dims.n_tok)
        p = matmul(h, w_mix_all, l, 768, F32)
        dt = matmul(h, w_dt_all, l, SSD_GROUPS * LANES, F32)

        qn, kn, vn = qkv_prep(dims, p, rope_c, rope_s0, rope_s1, q_norm[l], k_norm[l])
        ya = attention(dims, qn, kn, vn)

        w_gates = jnp.concatenate([rnn_w_r[l, 0], rnn_w_i[l, 0], rnn_w_r[l, 1], rnn_w_i[l, 1]], axis=-1).astype(BF16)
        b_gates = jnp.concatenate(
            [v.reshape(RNN_BLOCKS, 1, RNN_BLOCK_DIM) for v in (rnn_b_r[l, 0], rnn_b_i[l, 0], rnn_b_r[l, 1], rnn_b_i[l, 1])],
            axis=-1)
        lam = jnp.concatenate([rnn_lambda[l, d].reshape(RNN_BLOCKS, 1, RNN_BLOCK_DIM) for d in range(2)], axis=-1)
        yr = rglru_branch(dims, p, rnn_conv_w[l], rnn_conv_b[l], w_gates, b_gates, lam)

        xbc = ssd_prep(dims, p, ssd_conv_w[l], ssd_conv_b[l])
        yf, yb = ssd_scan(dims, xbc, dt, _group_dt_param(ssd_dt_bias[l]), _group_dt_param(ssd_a_log[l]))
        d_chan = jnp.repeat(ssd_d[l], SSD_HEAD_DIM).reshape(1, SSD_WIDTH)
        ys = ssd_finish(dims, yf, yb, xbc, p, d_chan, ssd_norm[l])

        merged = merge_branches(h, ya, yr, ys, w_gate_all, w_up_all, l, m_rows)
        xs = matmul_residual(dims, merged, w_o_all, l, xs, modtab, 2, m_rows)

        h2, logits_t = norm_mod(dims, xs, g_ffn[l], modtab, 3, m_rows, router_wt)
        cw_t, dest, tab = route(logits_t, router_b)
        dest = dest.reshape(m_rows)
        cw_rows = jnp.pad(cw_t.T, ((0, 0), (0, LANES - SUBLANES)))
        n_tiles = m_rows // MOE_TILE + N_EXPERT_GROUPS
        src = invert_slots(dest, n_tiles * MOE_TILE)
        ys2 = moe_grouped(h2, cw_rows, tab[0], tab[1, :1], tab[2], src, moe_g_all, moe_u_all, moe_d_all, l, n_tiles)
        xs = combine(dims, dest, xs, ys2, modtab, 5, m_rows)

    out = final_norm(xs, g_final, dims.n_lat)
    return out.reshape(bsz, dims.seq, D_MODEL)


def kernel(x, c, ctx, c_ctx, w_mod_a, w_mod_b, b_mod, g_mix, g_ffn, w_in, w_up, w_o, q_norm, k_norm, rnn_conv_w, rnn_conv_b, rnn_lambda, rnn_w_r, rnn_b_r, rnn_w_i, rnn_b_i, ssd_conv_w, ssd_conv_b, ssd_dt_bias, ssd_a_log, ssd_d, ssd_norm, router_w, router_b, moe_w_gate, moe_w_up, moe_w_down, g_final):
    dims = Dims(batch=x.shape[0], seq=x.shape[1], ctx=ctx.shape[1])
    return run_model(dims, x, c, ctx, c_ctx, w_mod_a, w_mod_b, b_mod, g_mix, g_ffn, w_in, w_up, w_o, q_norm, k_norm,
                     rnn_conv_w, rnn_conv_b, rnn_lambda, rnn_w_r, rnn_b_r, rnn_w_i, rnn_b_i,
                     ssd_conv_w, ssd_conv_b, ssd_dt_bias, ssd_a_log, ssd_d, ssd_norm,
                     router_w, router_b, moe_w_gate, moe_w_up, moe_w_down, g_final)
```

```python
import functools
from typing import NamedTuple

import jax
import jax.numpy as jnp
from jax import lax
from jax.experimental import pallas as pl
from jax.experimental.pallas import tpu as pltpu

F32 = jnp.float32
BF16 = jnp.bfloat16

D_MODEL = 4096
DEPTH = 4
GRID_W = 64
N_BRANCH = 3
BRANCH_WIDTH = 1024
HEAD_DIM = 128
N_Q_HEADS = 8
N_KV_HEADS = 2
Q_PER_KV = 4
KV_WIDTH = 256
N_ROPE_FREQ = 32
ROPE_THETA = 10000.0
RNN_WIDTH = 1024
RNN_BLOCKS = 8
RNN_BLOCK_DIM = 128
LRU_C = 8.0
CONV_W = 4
SSD_WIDTH = 1024
SSD_HEAD_DIM = 64
SSD_HEADS = 16
SSD_GROUPS = 2
SSD_STATE = 128
SSD_CHUNK = 128
SSD_BC = 256
SSD_XBC = 1536
HEADS_PER_GROUP = SSD_HEADS // SSD_GROUPS
GROUP_WIDTH = SSD_WIDTH // SSD_GROUPS
Q_OFF = 0
K_OFF = 1024
V_OFF = 1280
RX_OFF = 1536
RG_OFF = 2560
SZ_OFF = 3584
SX_OFF = 4608
SDT_OFF = 6144
MIX_COLS = 6176
MOD_RANK = 256
N_MOD = 6
N_EXPERTS = 16
N_EXPERT_GROUPS = 4
EXPERTS_PER_GROUP = 4
EXPERT_FF = 512
EPS = 1e-6

LANES = 128
SUBLANES = 8
VMEM_LIMIT_BYTES = 56 * 1024 * 1024
ROW_TILE = 256
SCAN_GROUPS = 4
MOE_TILE = 512
RANK_BLOCK = 256
ATTN_KEY_CHUNK = 1024
MM_TILE_M = 512
NEG_BIG = -1e30


class Dims(NamedTuple):
    batch: int
    seq: int
    ctx: int

    @property
    def n_lat(self):
        return self.batch * self.seq

    @property
    def n_tok(self):
        return self.batch * (self.seq + self.ctx)


def _cparams(sem):
    return pltpu.CompilerParams(dimension_semantics=sem, vmem_limit_bytes=VMEM_LIMIT_BYTES)


def _sigmoid(x):
    return 1.0 / (1.0 + jnp.exp(-x))


def _silu(x):
    return x * _sigmoid(x)


def _softplus(x):
    return jnp.maximum(x, 0.0) + jnp.log(1.0 + jnp.exp(-jnp.abs(x)))


def _gelu_tanh(x):
    return 0.5 * x * (1.0 + jnp.tanh(0.7978845608028654 * (x + 0.044715 * (x * x * x))))


def _dot(a, b):
    return jnp.dot(a, b, preferred_element_type=F32)


def _dot_nt(a, b):
    return lax.dot_general(a, b, (((1,), (1,)), ((), ())), preferred_element_type=F32)


def _mod_row(dims, tile, k):
    n_lat_tiles = dims.n_lat // tile
    per_batch = dims.seq // tile

    def f(i):
        return jnp.where(i < n_lat_tiles, 1 + i // per_batch, 0) * N_MOD + k
    return f


def _seg_block(dims, rows):
    lat_pb = dims.seq // rows
    ctx_pb = dims.ctx // rows
    ctx_base = dims.n_lat // rows

    def f(b, r):
        return jnp.where(r < lat_pb, b * lat_pb + r, ctx_base + b * ctx_pb + (r - lat_pb))
    return f


def _adaln_a_kernel(c_ref, w_ref, o_ref):
    c = c_ref[...]
    o_ref[0] = _dot(_silu(c).astype(BF16), w_ref[0].astype(BF16))


def _adaln_b_kernel(a_ref, w_ref, b_ref, o_ref):
    o_ref[0] = _dot(a_ref[0].astype(BF16), w_ref[0].astype(BF16)) + b_ref[0]


def adaln_all(cond, w_a, w_b, b):
    nl = w_a.shape[0]
    a = pl.pallas_call(
        _adaln_a_kernel,
        grid=(nl,),
        in_specs=[pl.BlockSpec((SUBLANES, D_MODEL), lambda l: (0, 0)),
                  pl.BlockSpec((1, D_MODEL, MOD_RANK), lambda l: (l, 0, 0))],
        out_specs=pl.BlockSpec((1, SUBLANES, MOD_RANK), lambda l: (l, 0, 0)),
        out_shape=jax.ShapeDtypeStruct((nl, SUBLANES, MOD_RANK), F32),
        compiler_params=_cparams(("arbitrary",)),
        name="adaln_a",
    )(cond, w_a)
    tn = 4096
    ncol = N_MOD * D_MODEL
    return pl.pallas_call(
        _adaln_b_kernel,
        grid=(nl, ncol // tn),
        in_specs=[pl.BlockSpec((1, SUBLANES, MOD_RANK), lambda l, j: (l, 0, 0)),
                  pl.BlockSpec((1, MOD_RANK, tn), lambda l, j: (l, 0, j)),
                  pl.BlockSpec((1, 1, tn), lambda l, j: (l, 0, j))],
        out_specs=pl.BlockSpec((1, SUBLANES, tn), lambda l, j: (l, 0, j)),
        out_shape=jax.ShapeDtypeStruct((nl, SUBLANES, ncol), F32),
        compiler_params=_cparams(("arbitrary", "arbitrary")),
        name="adaln_b",
    )(a, w_b, b.reshape(nl, 1, ncol))


def _norm_mod_kernel(x_ref, g_ref, sh_ref, sc_ref, o_ref):
    x = x_ref[...]
    y = x * lax.rsqrt(jnp.mean(x * x, axis=-1, keepdims=True) + EPS) * g_ref[...]
    o_ref[...] = (y * (1.0 + sc_ref[0]) + sh_ref[0]).astype(o_ref.dtype)


def _norm_mod_router_kernel(x_ref, g_ref, sh_ref, sc_ref, rw_ref, o_ref, lg_ref):
    x = x_ref[...]
    y = x * lax.rsqrt(jnp.mean(x * x, axis=-1, keepdims=True) + EPS) * g_ref[...]
    h = y * (1.0 + sc_ref[0]) + sh_ref[0]
    o_ref[...] = h
    lg_ref[...] = _dot_nt(rw_ref[...], h.astype(BF16))


def norm_mod(dims, x, gain, modtab, k, m_rows, router_wt=None):
    tm = ROW_TILE
    mrow = _mod_row(dims, tm, k)
    mrow1 = _mod_row(dims, tm, k + 1)
    in_specs = [pl.BlockSpec((tm, D_MODEL), lambda i: (i, 0)),
                pl.BlockSpec((1, D_MODEL), lambda i: (0, 0)),
                pl.BlockSpec((1, 1, D_MODEL), lambda i: (mrow(i), 0, 0)),
                pl.BlockSpec((1, 1, D_MODEL), lambda i: (mrow1(i), 0, 0))]
    h_spec = pl.BlockSpec((tm, D_MODEL), lambda i: (i, 0))
    if router_wt is None:
        return pl.pallas_call(
            _norm_mod_kernel, grid=(m_rows // tm,), in_specs=in_specs, out_specs=h_spec,
            out_shape=jax.ShapeDtypeStruct((m_rows, D_MODEL), BF16),
            compiler_params=_cparams(("arbitrary",)), name="norm_mod",
        )(x, gain.reshape(1, D_MODEL), modtab, modtab)
    return pl.pallas_call(
        _norm_mod_router_kernel, grid=(m_rows // tm,),
        in_specs=in_specs + [pl.BlockSpec((N_EXPERTS, D_MODEL), lambda i: (0, 0))],
        out_specs=[h_spec, pl.BlockSpec((N_EXPERTS, tm), lambda i: (0, i))],
        out_shape=[jax.ShapeDtypeStruct((m_rows, D_MODEL), F32),
                   jax.ShapeDtypeStruct((N_EXPERTS, m_rows), F32)],
        compiler_params=_cparams(("arbitrary",)), name="norm_mod_router",
    )(x, gain.reshape(1, D_MODEL), modtab, modtab, router_wt)


def _final_norm_kernel(x_ref, g_ref, o_ref):
    x = x_ref[...]
    o_ref[...] = x * lax.rsqrt(jnp.mean(x * x, axis=-1, keepdims=True) + EPS) * g_ref[...]


def final_norm(x, gain, m_rows):
    tm = ROW_TILE
    return pl.pallas_call(
        _final_norm_kernel, grid=(m_rows // tm,),
        in_specs=[pl.BlockSpec((tm, D_MODEL), lambda i: (i, 0)),
                  pl.BlockSpec((1, D_MODEL), lambda i: (0, 0))],
        out_specs=pl.BlockSpec((tm, D_MODEL), lambda i: (i, 0)),
        out_shape=jax.ShapeDtypeStruct((m_rows, D_MODEL), F32),
        compiler_params=_cparams(("arbitrary",)), name="final_norm",
    )(x, gain.reshape(1, D_MODEL))


def _mm_kernel(a_ref, w_ref, o_ref):
    o_ref[...] = _dot(a_ref[...], w_ref[0]).astype(o_ref.dtype)


def matmul(a, w_all, l, tn, out_dtype):
    m, k = a.shape
    n = w_all.shape[2]
    tm = MM_TILE_M
    return pl.pallas_call(
        _mm_kernel, grid=(n // tn, m // tm),
        in_specs=[pl.BlockSpec((tm, k), lambda j, i: (i, 0)),
                  pl.BlockSpec((1, k, tn), lambda j, i: (l, 0, j))],
        out_specs=pl.BlockSpec((tm, tn), lambda j, i: (i, j)),
        out_shape=jax.ShapeDtypeStruct((m, n), out_dtype),
        compiler_params=_cparams(("arbitrary", "arbitrary")), name="mm_in",
    )(a, w_all)


def _mm_res_kernel(a_ref, w_ref, x_ref, g_ref, o_ref):
    o_ref[...] = x_ref[...] + g_ref[0] * _dot(a_ref[...], w_ref[0])


def matmul_residual(dims, a, w_all, l, x, modtab, k, m_rows):
    kk = a.shape[1]
    n = w_all.shape[2]
    tm, tn = MM_TILE_M, 1024
    mrow = _mod_row(dims, tm, k)
    return pl.pallas_call(
        _mm_res_kernel, grid=(n // tn, m_rows // tm),
        in_specs=[pl.BlockSpec((tm, kk), lambda j, i: (i, 0)),
                  pl.BlockSpec((1, kk, tn), lambda j, i: (l, 0, j)),
                  pl.BlockSpec((tm, tn), lambda j, i: (i, j)),
                  pl.BlockSpec((1, 1, tn), lambda j, i: (mrow(i), 0, j))],
        out_specs=pl.BlockSpec((tm, tn), lambda j, i: (i, j)),
        out_shape=jax.ShapeDtypeStruct((m_rows, n), F32),
        compiler_params=_cparams(("arbitrary", "arbitrary")), name="mm_out_res",
    )(a, w_all, x, modtab)


def _merge_kernel(h_ref, ya_ref, yr_ref, ys_ref, wg0_ref, wg1_ref, wg2_ref,
                  wu0_ref, wu1_ref, wu2_ref, o_ref):
    h = h_ref[...]
    acc = _sigmoid(_dot(h, wg0_ref[0])) * _dot(ya_ref[...], wu0_ref[0, 0])
    acc += _sigmoid(_dot(h, wg1_ref[0])) * _dot(yr_ref[...], wu1_ref[0, 0])
    acc += _sigmoid(_dot(h, wg2_ref[0])) * _dot(ys_ref[...], wu2_ref[0, 0])
    o_ref[...] = acc.astype(o_ref.dtype)


def merge_branches(h, ya, yr, ys, w_gate_all, w_up_all, l, m_rows):
    tm, tn = MM_TILE_M, 512
    nj = D_MODEL // tn
    y_spec = pl.BlockSpec((tm, BRANCH_WIDTH), lambda j, i: (i, 0))

    def wg_spec(n):
        return pl.BlockSpec((1, D_MODEL, tn), lambda j, i: (l, 0, n * nj + j))

    def wu_spec(n):
        return pl.BlockSpec((1, 1, BRANCH_WIDTH, tn), lambda j, i: (l, n, 0, j))

    return pl.pallas_call(
        _merge_kernel, grid=(nj, m_rows // tm),
        in_specs=[pl.BlockSpec((tm, D_MODEL), lambda j, i: (i, 0)), y_spec, y_spec, y_spec,
                  wg_spec(0), wg_spec(1), wg_spec(2), wu_spec(0), wu_spec(1), wu_spec(2)],
        out_specs=pl.BlockSpec((tm, tn), lambda j, i: (i, j)),
        out_shape=jax.ShapeDtypeStruct((m_rows, D_MODEL), BF16),
        compiler_params=_cparams(("arbitrary", "arbitrary")), name="merge",
    )(h, ya, yr, ys, w_gate_all, w_gate_all, w_gate_all, w_up_all, w_up_all, w_up_all)


def _qkv_prep_kernel(q_ref, k_ref, v_ref, c_ref, s0_ref, s1_ref, qg_ref, kg_ref,
                     qo_ref, ko_ref, vo_ref):
    c, s0, s1 = c_ref[...], s0_ref[...], s1_ref[...]

    def head(xh, gain, scale):
        y = xh * lax.rsqrt(jnp.mean(xh * xh, axis=-1, keepdims=True) + EPS) * gain
        r = y * c + pltpu.roll(y, 96, 1) * s0 + pltpu.roll(y, 32, 1) * s1
        return r * scale

    for hh in range(N_Q_HEADS):
        sl = slice(hh * HEAD_DIM, (hh + 1) * HEAD_DIM)
        qo_ref[:, sl] = head(q_ref[:, sl], qg_ref[...], HEAD_DIM ** -0.5).astype(qo_ref.dtype)
    for hh in range(N_KV_HEADS):
        sl = slice(hh * HEAD_DIM, (hh + 1) * HEAD_DIM)
        ko_ref[:, sl] = head(k_ref[:, sl], kg_ref[...], 1.0).astype(ko_ref.dtype)
    vo_ref[...] = v_ref[...].astype(vo_ref.dtype)


def qkv_prep(dims, p, rope_c, rope_s0, rope_s1, q_gain, k_gain):
    tm = ROW_TILE
    n_lat_tiles = dims.n_lat // tm
    per_batch = dims.seq // tm

    def tab(i):
        return (jnp.where(i < n_lat_tiles, i % per_batch, per_batch), 0)

    m = dims.n_tok
    tab_spec = pl.BlockSpec((tm, HEAD_DIM), tab)
    g_spec = pl.BlockSpec((1, HEAD_DIM), lambda i: (0, 0))
    return pl.pallas_call(
        _qkv_prep_kernel, grid=(m // tm,),
        in_specs=[pl.BlockSpec((tm, BRANCH_WIDTH), lambda i: (i, Q_OFF // BRANCH_WIDTH)),
                  pl.BlockSpec((tm, KV_WIDTH), lambda i: (i, K_OFF // KV_WIDTH)),
                  pl.BlockSpec((tm, KV_WIDTH), lambda i: (i, V_OFF // KV_WIDTH)),
                  tab_spec, tab_spec, tab_spec, g_spec, g_spec],
        out_specs=[pl.BlockSpec((tm, BRANCH_WIDTH), lambda i: (i, 0)),
                   pl.BlockSpec((tm, KV_WIDTH), lambda i: (i, 0)),
                   pl.BlockSpec((tm, KV_WIDTH), lambda i: (i, 0))],
        out_shape=[jax.ShapeDtypeStruct((m, BRANCH_WIDTH), BF16),
                   jax.ShapeDtypeStruct((m, KV_WIDTH), BF16),
                   jax.ShapeDtypeStruct((m, KV_WIDTH), BF16)],
        compiler_params=_cparams(("arbitrary",)), name="qkv_prep",
    )(p, p, p, rope_c, rope_s0, rope_s1, q_gain.reshape(1, HEAD_DIM), k_gain.reshape(1, HEAD_DIM))


def _attn_kernel(q_ref, kl_ref, kc_ref, vl_ref, vc_ref, o_ref, sa_ref, sb_ref, *, lat_tiles, seq, ctx, kc):
    qi = pl.program_id(2)

    chunks = [(kl_ref, vl_ref, c * kc, kc, c * kc) for c in range(seq // kc)] + [(kc_ref, vc_ref, 0, ctx, seq)]

    def scores(g, chunk, s_ref):
        k_ref, _, r0, n, col = chunk
        s = _dot_nt(q_ref[:, g * HEAD_DIM:(g + 1) * HEAD_DIM], k_ref[r0:r0 + n, :])
        s_ref[:, col:col + n] = s
        return jnp.max(s, axis=-1, keepdims=True)

    def weighted(chunk, s_ref, mx):
        _, v_ref, r0, n, col = chunk
        p = jnp.exp(s_ref[:, col:col + n] - mx)
        return jnp.sum(p, axis=-1, keepdims=True), _dot(p.astype(BF16), v_ref[r0:r0 + n, :])

    @pl.when(qi < lat_tiles)
    def _():
        bufs = (sa_ref, sb_ref)
        mx = None
        for ch in chunks:
            cm = scores(0, ch, bufs[0])
            mx = cm if mx is None else jnp.maximum(mx, cm)
        for g in range(Q_PER_KV):
            cur, nxt = bufs[g % 2], bufs[(g + 1) % 2]
            den, acc, mx_next = None, None, None
            for ch in chunks:
                if g + 1 < Q_PER_KV:
                    cm = scores(g + 1, ch, nxt)
                    mx_next = cm if mx_next is None else jnp.maximum(mx_next, cm)
                ds, pv = weighted(ch, cur, mx)
                den = ds if den is None else den + ds
                acc = pv if acc is None else acc + pv
            o_ref[:, g * HEAD_DIM:(g + 1) * HEAD_DIM] = (acc / den).astype(o_ref.dtype)
            mx = mx_next

    @pl.when(qi >= lat_tiles)
    def _():
        for g in range(Q_PER_KV):
            sl = slice(g * HEAD_DIM, (g + 1) * HEAD_DIM)
            s_c = _dot_nt(q_ref[:, sl], kc_ref[...])
            p_c = jnp.exp(s_c - jnp.max(s_c, axis=-1, keepdims=True))
            den = jnp.sum(p_c, axis=-1, keepdims=True)
            o_ref[:, sl] = (_dot(p_c.astype(BF16), vc_ref[...]) / den).astype(o_ref.dtype)


def attention(dims, qn, kn, vn):
    tq = ROW_TILE
    assert dims.ctx == tq
    lat_tiles = dims.seq // tq
    seg = _seg_block(dims, tq)
    ctx_blk = dims.n_lat // dims.ctx
    gw = Q_PER_KV * HEAD_DIM
    kc = min(ATTN_KEY_CHUNK, dims.seq)
    assert dims.seq % kc == 0
    n_keys = dims.seq + dims.ctx
    return pl.pallas_call(
        functools.partial(_attn_kernel, lat_tiles=lat_tiles, seq=dims.seq, ctx=dims.ctx, kc=kc),
        grid=(dims.batch, N_KV_HEADS, lat_tiles + 1),
        scratch_shapes=[pltpu.VMEM((tq, n_keys), F32), pltpu.VMEM((tq, n_keys), F32)],
        in_specs=[pl.BlockSpec((tq, gw), lambda b, h, r: (seg(b, r), h)),
                  pl.BlockSpec((dims.seq, HEAD_DIM), lambda b, h, r: (b, h)),
                  pl.BlockSpec((dims.ctx, HEAD_DIM), lambda b, h, r: (ctx_blk + b, h)),
                  pl.BlockSpec((dims.seq, HEAD_DIM), lambda b, h, r: (b, h)),
                  pl.BlockSpec((dims.ctx, HEAD_DIM), lambda b, h, r: (ctx_blk + b, h))],
        out_specs=pl.BlockSpec((tq, gw), lambda b, h, r: (seg(b, r), h)),
        out_shape=jax.ShapeDtypeStruct((dims.n_tok, BRANCH_WIDTH), BF16),
        compiler_params=_cparams(("arbitrary", "arbitrary", "arbitrary")), name="attention",
    )(qn, kn, kn, vn, vn)


def _conv_rows(prev8, cur, next8, w_ref, b_ref):
    rows = cur.shape[0]
    ext = jnp.concatenate([prev8, cur, next8], axis=0)
    y = b_ref[...] + w_ref[0:1, :] * ext[7:7 + rows]
    y = y + w_ref[1:2, :] * cur
    y = y + w_ref[2:3, :] * ext[9:9 + rows]
    y = y + w_ref[3:4, :] * ext[10:10 + rows]
    return y


def _conv_chunk(src_ref, r0, seg_len, rows, w_ref, b_ref):
    cur = src_ref[pl.ds(r0, rows), :]
    p0 = pl.multiple_of(jnp.maximum(r0 - SUBLANES, 0), SUBLANES)
    n0 = pl.multiple_of(jnp.minimum(r0 + rows, seg_len - SUBLANES), SUBLANES)
    prev8 = src_ref[pl.ds(p0, SUBLANES), :] * jnp.where(r0 > 0, 1.0, 0.0)
    next8 = src_ref[pl.ds(n0, SUBLANES), :] * jnp.where(r0 + rows < seg_len, 1.0, 0.0)
    return _conv_rows(prev8, cur, next8, w_ref, b_ref)


def _rglru_kernel(rxl_ref, rxc_ref, rgl_ref, rgc_ref, cw_ref, cb_ref, wg_ref, bg_ref, lam_ref,
                  o_ref, af_s, uf_s, ab_s, ub_s, hf_s, hb_s, *, seq, ctx):
    r = pl.program_id(2)
    tot = seq + ctx
    rows = ROW_TILE
    w = RNN_BLOCK_DIM

    @pl.when(r == 0)
    def _():
        sp_f = _softplus(-lam_ref[0, :, 0:w])
        sp_b = _softplus(-lam_ref[0, :, w:2 * w])

        def gates(x, base):
            z = _dot(x.astype(BF16), wg_ref[0]) + bg_ref[0]
            sg = _sigmoid(z)
            for d, (sp, a_s, u_s) in enumerate(((sp_f, af_s, uf_s), (sp_b, ab_s, ub_s))):
                rg = sg[:, (2 * d) * w:(2 * d + 1) * w]
                ig = sg[:, (2 * d + 1) * w:(2 * d + 2) * w]
                a = jnp.exp(-LRU_C * rg * sp)
                a_s[pl.ds(base, rows), :] = a
                u_s[pl.ds(base, rows), :] = jnp.sqrt(1.0 - a * a) * (ig * x)

        for c0 in range(0, ctx, rows):
            gates(_conv_chunk(rxc_ref, c0, ctx, rows, cw_ref, cb_ref), c0)

        def lat_body(c, carry):
            r0 = pl.multiple_of(c * rows, rows)
            gates(_conv_chunk(rxl_ref, r0, seq, rows, cw_ref, cb_ref), pl.multiple_of(ctx + r0, rows))
            return carry
        lax.fori_loop(0, seq // rows, lat_body, 0)

        row = lax.broadcasted_iota(jnp.int32, (SUBLANES, w), 0)
        blk = SCAN_GROUPS * SUBLANES

        def scan_block(base, carry, a_s, u_s, out_s, reverse):
            a_blk = a_s[pl.ds(base, blk), :]
            u_blk = u_s[pl.ds(base, blk), :]
            hs = [None] * SCAN_GROUPS
            for g in (range(SCAN_GROUPS - 1, -1, -1) if reverse else range(SCAN_GROUPS)):
                a = a_blk[g * SUBLANES:(g + 1) * SUBLANES]
                u = u_blk[g * SUBLANES:(g + 1) * SUBLANES]
                for k in (1, 2, 4):
                    sh = SUBLANES - k if reverse else k
                    m = (row < SUBLANES - k) if reverse else (row >= k)
                    u = u + a * jnp.where(m, pltpu.roll(u, sh, 0), 0.0)
                    a = a * jnp.where(m, pltpu.roll(a, sh, 0), 1.0)
                hs[g] = u + a * carry
                e = 0 if reverse else SUBLANES - 1
                carry = (jnp.broadcast_to(u[e:e + 1, :], (SUBLANES, w))
                         + jnp.broadcast_to(a[e:e + 1, :], (SUBLANES, w)) * carry)
            out_s[pl.ds(base, blk), :] = jnp.concatenate(hs, axis=0)
            return carry

        def both(i, carry, f0, b0):
            cf, cb = carry
            cf = scan_block(pl.multiple_of(f0 + i * blk, blk), cf, af_s, uf_s, hf_s, False)
            cb = scan_block(pl.multiple_of(b0 - (i + 1) * blk, blk), cb, ab_s, ub_s, hb_s, True)
            return cf, cb

        zero = jnp.zeros((SUBLANES, w), F32)
        carry = lax.fori_loop(0, ctx // blk, lambda i, c: both(i, c, 0, ctx), (zero, zero))
        lax.fori_loop(0, seq // blk, lambda i, c: both(i, c, ctx, tot), carry)

        for c0 in range(0, ctx, rows):
            sl = pl.ds(c0, rows)
            hf_s[sl, :] = _gelu_tanh(rgc_ref[sl, :]) * (hf_s[sl, :] + hb_s[sl, :])

        def out_body(c, carry):
            r0 = pl.multiple_of(c * rows, rows)
            sl = pl.ds(pl.multiple_of(ctx + r0, rows), rows)
            hf_s[sl, :] = _gelu_tanh(rgl_ref[pl.ds(r0, rows), :]) * (hf_s[sl, :] + hb_s[sl, :])
            return carry
        lax.fori_loop(0, seq // rows, out_body, 0)

    lat_pieces = seq // rows
    src = jnp.where(r < lat_pieces, ctx + r * rows, (r - lat_pieces) * rows)
    o_ref[...] = hf_s[pl.ds(pl.multiple_of(src, rows), rows), :].astype(o_ref.dtype)


def rglru_branch(dims, p, conv_w, conv_b, w_gates, b_gates, lam):
    rows = ROW_TILE
    w = RNN_BLOCK_DIM
    pieces = (dims.seq + dims.ctx) // rows
    seg = _seg_block(dims, rows)
    ctx_blk = dims.n_lat // dims.ctx
    rx0, rg0 = RX_OFF // w, RG_OFF // w
    tot = dims.seq + dims.ctx
    return pl.pallas_call(
        functools.partial(_rglru_kernel, seq=dims.seq, ctx=dims.ctx),
        grid=(dims.batch, RNN_BLOCKS, pieces),
        in_specs=[pl.BlockSpec((dims.seq, w), lambda b, n, r: (b, rx0 + n)),
                  pl.BlockSpec((dims.ctx, w), lambda b, n, r: (ctx_blk + b, rx0 + n)),
                  pl.BlockSpec((dims.seq, w), lambda b, n, r: (b, rg0 + n)),
                  pl.BlockSpec((dims.ctx, w), lambda b, n, r: (ctx_blk + b, rg0 + n)),
                  pl.BlockSpec((CONV_W, w), lambda b, n, r: (0, n)),
                  pl.BlockSpec((1, w), lambda b, n, r: (0, n)),
                  pl.BlockSpec((1, w, 4 * w), lambda b, n, r: (n, 0, 0)),
                  pl.BlockSpec((1, 1, 4 * w), lambda b, n, r: (n, 0, 0)),
                  pl.BlockSpec((1, 1, 2 * w), lambda b, n, r: (n, 0, 0))],
        out_specs=pl.BlockSpec((rows, w), lambda b, n, r: (seg(b, r), n)),
        out_shape=jax.ShapeDtypeStruct((dims.n_tok, RNN_WIDTH), BF16),
        scratch_shapes=[pltpu.VMEM((tot, w), F32) for _ in range(6)],
        compiler_params=_cparams(("arbitrary", "arbitrary", "arbitrary")), name="rglru",
    )(p, p, p, p, conv_w, conv_b.reshape(1, RNN_WIDTH), w_gates, b_gates, lam)


def _ssd_prep_kernel(cur_ref, prev_ref, next_ref, w_ref, b_ref, o_ref, *, n_lat_tiles, per_batch):
    i = pl.program_id(0)
    j = i % per_batch
    lat = i < n_lat_tiles
    pv = jnp.where(jnp.logical_and(lat, j != 0), 1.0, 0.0)
    nv = jnp.where(jnp.logical_and(lat, j != per_batch - 1), 1.0, 0.0)
    y = _conv_rows(prev_ref[...] * pv, cur_ref[...], next_ref[...] * nv, w_ref, b_ref)
    o_ref[...] = _silu(y)


def ssd_prep(dims, p, conv_w, conv_b):
    tm = ROW_TILE
    assert dims.ctx == tm
    m = dims.n_tok
    hb = tm // SUBLANES
    last8 = m // SUBLANES - 1
    c0 = SX_OFF // SSD_XBC
    return pl.pallas_call(
        functools.partial(_ssd_prep_kernel, n_lat_tiles=dims.n_lat // tm, per_batch=dims.seq // tm),
        grid=(m // tm,),
        in_specs=[pl.BlockSpec((tm, SSD_XBC), lambda i: (i, c0)),
                  pl.BlockSpec((SUBLANES, SSD_XBC), lambda i: (jnp.maximum(i * hb - 1, 0), c0)),
                  pl.BlockSpec((SUBLANES, SSD_XBC), lambda i: (jnp.minimum((i + 1) * hb, last8), c0)),
                  pl.BlockSpec((CONV_W, SSD_XBC), lambda i: (0, 0)),
                  pl.BlockSpec((1, SSD_XBC), lambda i: (0, 0))],
        out_specs=pl.BlockSpec((tm, SSD_XBC), lambda i: (i, 0)),
        out_shape=jax.ShapeDtypeStruct((m, SSD_XBC), F32),
        compiler_params=_cparams(("arbitrary",)), name="ssd_prep",
    )(p, p, p, conv_w, conv_b.reshape(1, SSD_XBC))


def _ssd_chunk(x_ref, b_ref, c_ref, dt_ref, dtb_ref, alog_ref, s_ref, y_ref, reverse):
    q = SSD_CHUNK
    col0 = HEADS_PER_GROUP if reverse else 0
    ri = lax.broadcasted_iota(jnp.int32, (q, q), 0)
    ci = lax.broadcasted_iota(jnp.int32, (q, q), 1)
    tri = (ri <= ci) if reverse else (ri >= ci)
    cum = jnp.where(tri, 1.0, 0.0).astype(BF16)
    ones = jnp.ones((q, q), BF16)
    left = ci < SSD_HEAD_DIM
    top = ri < SSD_HEAD_DIM

    dtc = _softplus(dt_ref[...] + dtb_ref[0])
    a = dtc * (-jnp.exp(alog_ref[0]))
    a_hi = a.astype(BF16)
    r1 = a - a_hi.astype(F32)
    a_mid = r1.astype(BF16)
    a_lo = (r1 - a_mid.astype(F32)).astype(BF16)
    acum = _dot(cum, a_hi) + _dot(cum, a_mid) + _dot(cum, a_lo)
    atot = _dot(ones, a_hi) + _dot(ones, a_mid) + _dot(ones, a_lo)
    acum_t = acum.T

    bm = b_ref[...].astype(BF16)
    cm = c_ref[...].astype(BF16)
    cb = _dot_nt(cm, bm)

    for pair in range(HEADS_PER_GROUP // 2):
        ca = col0 + 2 * pair
        lanes = slice(pair * LANES, (pair + 1) * LANES)
        col_a, col_b = acum[:, ca:ca + 1], acum[:, ca + 1:ca + 2]
        row_a, row_b = acum_t[ca:ca + 1, :], acum_t[ca + 1:ca + 2, :]
        l_a = jnp.exp(jnp.where(tri, col_a - row_a, NEG_BIG))
        l_b = jnp.exp(jnp.where(tri, col_b - row_b, NEG_BIG))
        xdt = x_ref[:, lanes] * jnp.where(left, dtc[:, ca:ca + 1], dtc[:, ca + 1:ca + 2])
        xdt_b = xdt.astype(BF16)
        y_in = jnp.where(left, _dot((cb * l_a).astype(BF16), xdt_b), _dot((cb * l_b).astype(BF16), xdt_b))
        s_old = s_ref[lanes, :]
        y_st = _dot_nt(cm, s_old.astype(BF16)) * jnp.where(left, jnp.exp(col_a), jnp.exp(col_b))
        y_ref[:, lanes] = y_in + y_st
        tot_a, tot_b = atot[:, ca:ca + 1], atot[:, ca + 1:ca + 2]
        xw = xdt * jnp.where(left, jnp.exp(tot_a - col_a), jnp.exp(tot_b - col_b))
        s_new = _dot(xw.T.astype(BF16), bm)
        s_ref[lanes, :] = s_old * jnp.where(top, jnp.exp(tot_a), jnp.exp(tot_b)) + s_new


def _ssd_scan_kernel(xf_ref, bf_ref, cf_ref, dtf_ref, xb_ref, bb_ref, cb_ref, dtb_ref_,
                     bias_ref, alog_ref, yf_ref, yb_ref, sf_s, sb_s):
    @pl.when(pl.program_id(2) == 0)
    def _():
        sf_s[...] = jnp.zeros_like(sf_s)
        sb_s[...] = jnp.zeros_like(sb_s)

    _ssd_chunk(xf_ref, bf_ref, cf_ref, dtf_ref, bias_ref, alog_ref, sf_s, yf_ref, False)
    _ssd_chunk(xb_ref, bb_ref, cb_ref, dtb_ref_, bias_ref, alog_ref, sb_s, yb_ref, True)


def ssd_scan(dims, xbc, dt, dt_bias, a_log):
    q = SSD_CHUNK
    lat_c, ctx_c = dims.seq // q, dims.ctx // q
    ctx_base = dims.n_lat // q
    steps = lat_c + ctx_c

    def cf(b, s):
        return jnp.where(s < ctx_c, ctx_base + b * ctx_c + s, b * lat_c + (s - ctx_c))

    def cbk(b, s):
        return jnp.where(s < ctx_c, ctx_base + b * ctx_c + (ctx_c - 1 - s),
                         b * lat_c + (lat_c - 1 - (s - ctx_c)))

    gw = GROUP_WIDTH
    bcol, ccol = SSD_WIDTH // SSD_STATE, (SSD_WIDTH + SSD_BC) // SSD_STATE

    def specs(cfun):
        return [pl.BlockSpec((q, gw), lambda b, g, s: (cfun(b, s), g)),
                pl.BlockSpec((q, SSD_STATE), lambda b, g, s: (cfun(b, s), bcol + g)),
                pl.BlockSpec((q, SSD_STATE), lambda b, g, s: (cfun(b, s), ccol + g)),
                pl.BlockSpec((q, LANES), lambda b, g, s: (cfun(b, s), g))]

    par_spec = pl.BlockSpec((1, 1, LANES), lambda b, g, s: (g, 0, 0))
    yshape = jax.ShapeDtypeStruct((dims.n_tok, SSD_WIDTH), F32)
    return pl.pallas_call(
        _ssd_scan_kernel, grid=(dims.batch, SSD_GROUPS, steps),
        in_specs=specs(cf) + specs(cbk) + [par_spec, par_spec],
        out_specs=[pl.BlockSpec((q, gw), lambda b, g, s: (cf(b, s), g)),
                   pl.BlockSpec((q, gw), lambda b, g, s: (cbk(b, s), g))],
        out_shape=[yshape, yshape],
        scratch_shapes=[pltpu.VMEM((gw, SSD_STATE), F32), pltpu.VMEM((gw, SSD_STATE), F32)],
        compiler_params=_cparams(("arbitrary", "arbitrary", "arbitrary")), name="ssd_scan",
    )(xbc, xbc, xbc, dt, xbc, xbc, xbc, dt, dt_bias, a_log)


def _ssd_finish_kernel(yf_ref, yb_ref, x_ref, z_ref, d_ref, g_ref, o_ref):
    y = d_ref[...] * x_ref[...] + yf_ref[...] + yb_ref[...]
    gt = y * _silu(z_ref[...])
    o_ref[...] = (gt * lax.rsqrt(jnp.mean(gt * gt, axis=-1, keepdims=True) + EPS) * g_ref[...]).astype(o_ref.dtype)


def ssd_finish(dims, yf, yb, xbc, p, d_chan, norm_g):
    tm = ROW_TILE
    gw = GROUP_WIDTH
    z0 = SZ_OFF // gw
    blk = pl.BlockSpec((tm, gw), lambda i, g: (i, g))
    vec = pl.BlockSpec((1, gw), lambda i, g: (0, g))
    return pl.pallas_call(
        _ssd_finish_kernel, grid=(dims.n_tok // tm, SSD_GROUPS),
        in_specs=[blk, blk, blk, pl.BlockSpec((tm, gw), lambda i, g: (i, z0 + g)), vec, vec],
        out_specs=blk,
        out_shape=jax.ShapeDtypeStruct((dims.n_tok, SSD_WIDTH), BF16),
        compiler_params=_cparams(("arbitrary", "arbitrary")), name="ssd_finish",
    )(yf, yb, xbc, p, d_chan, norm_g.reshape(1, SSD_WIDTH))


def _route_kernel(lg_ref, rb_ref, cw_ref, dest_ref, tab_ref, *, m, tile):
    score = _sigmoid(lg_ref[...])
    sel = score + rb_ref[...]
    v = [sel[e:e + 1, :] for e in range(N_EXPERTS)]
    sc = [score[e:e + 1, :] for e in range(N_EXPERTS)]
    best, best_g = None, None
    for g in range(N_EXPERT_GROUPS):
        vg = v[4 * g:4 * g + 4]
        gs = vg[0] + vg[1]
        for (i, j) in ((0, 2), (0, 3), (1, 2), (1, 3), (2, 3)):
            gs = jnp.maximum(gs, vg[i] + vg[j])
        if g == 0:
            best, best_g = gs, jnp.zeros_like(gs, dtype=jnp.int32)
        else:
            better = gs > best
            best_g = jnp.where(better, g, best_g)
            best = jnp.where(better, gs, best)
    wts = []
    for e in range(N_EXPERTS):
        g = e // EXPERTS_PER_GROUP
        rank = jnp.zeros_like(best_g)
        for k in range(4 * g, 4 * g + 4):
            if k == e:
                continue
            ahead = (v[k] >= v[e]) if k < e else (v[k] > v[e])
            rank = rank + jnp.where(ahead, 1, 0)
        chosen = jnp.logical_and(best_g == g, rank < 2)
        wts.append(jnp.where(chosen, sc[e], 0.0))
    tot = wts[0]
    for e in range(1, N_EXPERTS):
        tot = tot + wts[e]
    for j in range(EXPERTS_PER_GROUP):
        wj = wts[j]
        for g in range(1, N_EXPERT_GROUPS):
            wj = wj + wts[EXPERTS_PER_GROUP * g + j]
        cw_ref[j:j + 1, :] = wj / tot
    cw_ref[EXPERTS_PER_GROUP:SUBLANES, :] = jnp.zeros((SUBLANES - EXPERTS_PER_GROUP, m), F32)

    row8 = lax.broadcasted_iota(jnp.int32, (SUBLANES, m), 0)
    onehot = jnp.where(row8 == best_g, 1.0, 0.0)
    counts = jnp.sum(onehot, axis=-1, keepdims=True)
    padded = jnp.floor((counts + (tile - 1)) * (1.0 / tile)) * tile
    pb = jnp.broadcast_to(padded, (SUBLANES, LANES))
    r128 = lax.broadcasted_iota(jnp.int32, (SUBLANES, LANES), 0)
    off_b = jnp.zeros((SUBLANES, LANES), F32)
    for k in range(1, N_EXPERT_GROUPS):
        off_b = off_b + jnp.where(r128 >= k, pltpu.roll(pb, k, 0), 0.0)
    off = off_b[:, 0:1]
    ri = lax.broadcasted_iota(jnp.int32, (RANK_BLOCK, RANK_BLOCK), 0)
    ci = lax.broadcasted_iota(jnp.int32, (RANK_BLOCK, RANK_BLOCK), 1)
    before = jnp.where(ri < ci, 1.0, 0.0).astype(BF16)
    carry = jnp.zeros((SUBLANES, 1), F32)
    for b in range(m // RANK_BLOCK):
        sl = slice(b * RANK_BLOCK, (b + 1) * RANK_BLOCK)
        ob = onehot[:, sl]
        pre = _dot(ob.astype(BF16), before) + carry
        dest_ref[:, sl] = jnp.sum(ob * (pre + off), axis=0, keepdims=True).astype(jnp.int32)
        carry = carry + jnp.sum(ob, axis=-1, keepdims=True)
    lane = lax.broadcasted_iota(jnp.int32, (SUBLANES, LANES), 1).astype(F32)
    ends = off + padded
    passed = jnp.where(jnp.logical_and(lane * tile >= ends, r128 < N_EXPERT_GROUPS), 1.0, 0.0)
    tile_group = jnp.sum(passed, axis=0, keepdims=True)
    n_used = jnp.sum(pb, axis=0, keepdims=True) * (1.0 / tile)
    group_end = jnp.sum(jnp.where(r128.astype(F32) == tile_group, off + counts, 0.0), axis=0, keepdims=True)
    n_valid = jnp.clip(group_end - lane[0:1, :] * tile, 0.0, float(tile))
    tab = jnp.where(r128 == 0, tile_group,
                    jnp.where(r128 == 1, n_used, jnp.where(r128 == 2, n_valid, 0.0)))
    tab_ref[...] = tab.astype(jnp.int32)


def route(logits_t, router_b):
    m = logits_t.shape[1]
    return pl.pallas_call(
        functools.partial(_route_kernel, m=m, tile=MOE_TILE), grid=(1,),
        in_specs=[pl.BlockSpec((N_EXPERTS, m), lambda i: (0, 0)),
                  pl.BlockSpec((N_EXPERTS, 1), lambda i: (0, 0))],
        out_specs=[pl.BlockSpec((SUBLANES, m), lambda i: (0, 0)),
                   pl.BlockSpec((1, m), lambda i: (0, 0)),
                   pl.BlockSpec((SUBLANES, LANES), lambda i: (0, 0))],
        out_shape=[jax.ShapeDtypeStruct((SUBLANES, m), F32),
                   jax.ShapeDtypeStruct((1, m), jnp.int32),
                   jax.ShapeDtypeStruct((SUBLANES, LANES), jnp.int32)],
        compiler_params=_cparams(("arbitrary",)), name="route",
    )(logits_t, router_b.reshape(N_EXPERTS, 1))


def _invert_kernel(dest_ref, src_ref):
    def clear(s, c):
        src_ref[s] = 0
        return c
    lax.fori_loop(0, src_ref.shape[0], clear, 0, unroll=8)

    def place(t, c):
        src_ref[dest_ref[t]] = t
        return c
    lax.fori_loop(0, dest_ref.shape[0], place, 0, unroll=8)


def invert_slots(dest, n_slots):
    smem = pl.BlockSpec(memory_space=pltpu.SMEM)
    return pl.pallas_call(
        _invert_kernel, in_specs=[smem], out_specs=smem,
        out_shape=jax.ShapeDtypeStruct((n_slots,), jnp.int32),
        name="moe_invert",
    )(dest)


def _moe_group_kernel(tg_ref, nu_ref, nv_ref, src_ref, h_ref, c_ref, wg_ref, wu_ref, wd_ref, o_ref,
                      xbuf, cbuf, xb_s, sem):
    tm = o_ref.shape[0]
    i = pl.program_id(0)
    j = pl.program_id(1)
    f = pl.program_id(2)
    n_used = nu_ref[0]
    active = i < n_used
    first = jnp.logical_and(j == 0, f == 0)
    slot = i % 2

    def gather(tile, buf_slot, wait):
        def body(r, c):
            s = src_ref[tile * tm + r]
            cps = (pltpu.make_async_copy(h_ref.at[pl.ds(s, 1)], xbuf.at[buf_slot, pl.ds(r, 1)], sem.at[0, buf_slot]),
                   pltpu.make_async_copy(c_ref.at[pl.ds(s, 1)], cbuf.at[buf_slot, pl.ds(r, 1)], sem.at[1, buf_slot]))
            for cp in cps:
                if wait:
                    cp.wait()
                else:
                    cp.start()
            return c
        lax.fori_loop(0, tm, body, 0, unroll=8)

    @pl.when(jnp.logical_and(first, i == 0))
    def _():
        gather(0, 0, False)

    @pl.when(jnp.logical_and(first, i + 1 < n_used))
    def _():
        gather(i + 1, (i + 1) % 2, False)

    @pl.when(jnp.logical_and(first, active))
    def _():
        gather(i, slot, True)
        xb_s[...] = xbuf[slot].astype(BF16)

    @pl.when(first)
    def _():
        o_ref[...] = jnp.zeros_like(o_ref)

    @pl.when(active)
    def _():
        xb = xb_s[...]
        g = _dot(xb, wg_ref[0, 0])
        u = _dot(xb, wu_ref[0, 0])
        c = cbuf[slot]
        lane = lax.broadcasted_iota(jnp.int32, c.shape, 1)
        rowi = lax.broadcasted_iota(jnp.int32, c.shape, 0)
        keep = jnp.logical_and(lane == j, rowi < nv_ref[i])
        cw = jnp.sum(jnp.where(keep, c, 0.0), axis=-1, keepdims=True)
        hid = _silu(g) * u * cw
        o_ref[...] += _dot(hid.astype(BF16), wd_ref[0, 0])


def moe_grouped(h, cw_rows, tile_group, n_used, n_valid, src, w_gate_all, w_up_all, w_down_all, l, n_tiles):
    tm = MOE_TILE
    tf = EXPERT_FF // 2
    last_f = EXPERT_FF // tf - 1

    def tile(i, nu):
        return jnp.minimum(i, nu[0] - 1)

    def expert(i, j, tg, nu):
        jj = jnp.where(i < nu[0], j, EXPERTS_PER_GROUP - 1)
        return tg[tile(i, nu)] * EXPERTS_PER_GROUP + jj

    def fblk(i, f, nu):
        return jnp.where(i < nu[0], f, last_f)

    any_spec = pl.BlockSpec(memory_space=pl.ANY)
    return pl.pallas_call(
        _moe_group_kernel,
        grid_spec=pltpu.PrefetchScalarGridSpec(
            num_scalar_prefetch=4, grid=(n_tiles, EXPERTS_PER_GROUP, EXPERT_FF // tf),
            in_specs=[any_spec, any_spec,
                      pl.BlockSpec((1, 1, D_MODEL, tf),
                                   lambda i, j, f, tg, nu, nv, sr: (l, expert(i, j, tg, nu), 0, fblk(i, f, nu))),
                      pl.BlockSpec((1, 1, D_MODEL, tf),
                                   lambda i, j, f, tg, nu, nv, sr: (l, expert(i, j, tg, nu), 0, fblk(i, f, nu))),
                      pl.BlockSpec((1, 1, tf, D_MODEL),
                                   lambda i, j, f, tg, nu, nv, sr: (l, expert(i, j, tg, nu), fblk(i, f, nu), 0))],
            out_specs=pl.BlockSpec((tm, D_MODEL), lambda i, j, f, tg, nu, nv, sr: (i, 0)),
            scratch_shapes=[pltpu.VMEM((2, tm, D_MODEL), F32), pltpu.VMEM((2, tm, LANES), F32),
                            pltpu.VMEM((tm, D_MODEL), BF16), pltpu.SemaphoreType.DMA((2, 2))]),
        out_shape=jax.ShapeDtypeStruct((n_tiles * tm, D_MODEL), F32),
        compiler_params=_cparams(("arbitrary", "arbitrary", "arbitrary")), name="moe_grouped",
    )(tile_group, n_used, n_valid, src, h, cw_rows, w_gate_all, w_up_all, w_down_all)


def _combine_kernel(dest_ref, x_ref, g_ref, ys_ref, o_ref, buf, sem):
    tm = x_ref.shape[0]
    i = pl.program_id(0)
    n = pl.num_programs(0)

    def gather(tile, slot, wait):
        def body(r, c):
            d = dest_ref[tile * tm + r]
            cp = pltpu.make_async_copy(ys_ref.at[pl.ds(d, 1)], buf.at[slot, pl.ds(r, 1)], sem.at[slot])
            if wait:
                cp.wait()
            else:
                cp.start()
            return c
        lax.fori_loop(0, tm, body, 0, unroll=8)

    @pl.when(i == 0)
    def _():
        gather(0, 0, False)

    @pl.when(i + 1 < n)
    def _():
        gather(i + 1, (i + 1) % 2, False)

    slot = i % 2
    gather(i, slot, True)
    o_ref[...] = x_ref[...] + g_ref[0] * buf[slot]


def combine(dims, dest, x, ys, modtab, k, m_rows):
    tm = ROW_TILE
    mrow = _mod_row(dims, tm, k)
    return pl.pallas_call(
        _combine_kernel,
        grid_spec=pltpu.PrefetchScalarGridSpec(
            num_scalar_prefetch=1, grid=(m_rows // tm,),
            in_specs=[pl.BlockSpec((tm, D_MODEL), lambda i, d: (i, 0)),
                      pl.BlockSpec((1, 1, D_MODEL), lambda i, d: (mrow(i), 0, 0)),
                      pl.BlockSpec(memory_space=pl.ANY)],
            out_specs=pl.BlockSpec((tm, D_MODEL), lambda i, d: (i, 0)),
            scratch_shapes=[pltpu.VMEM((2, tm, D_MODEL), F32), pltpu.SemaphoreType.DMA((2,))]),
        out_shape=jax.ShapeDtypeStruct((m_rows, D_MODEL), F32),
        compiler_params=_cparams(("arbitrary",)), name="moe_combine",
    )(dest, x, modtab, ys)


def rope_tables(dims):
    s = dims.seq
    rows = s // GRID_W
    row = jnp.repeat(jnp.arange(rows, dtype=F32), GRID_W)
    col = jnp.tile(jnp.arange(GRID_W, dtype=F32), rows)
    inv = ROPE_THETA ** (-jnp.arange(N_ROPE_FREQ, dtype=F32) / N_ROPE_FREQ)
    ang = jnp.stack([row[:, None] * inv, col[:, None] * inv], axis=1)
    cos, sin = jnp.cos(ang), jnp.sin(ang)
    zero = jnp.zeros_like(sin)
    c = jnp.stack([cos, cos], axis=2).reshape(s, HEAD_DIM)
    s0 = jnp.stack([-sin, zero], axis=2).reshape(s, HEAD_DIM)
    s1 = jnp.stack([zero, sin], axis=2).reshape(s, HEAD_DIM)
    ident = jnp.ones((ROW_TILE, HEAD_DIM), F32)
    zpad = jnp.zeros((ROW_TILE, HEAD_DIM), F32)
    return (jnp.concatenate([c, ident], 0), jnp.concatenate([s0, zpad], 0), jnp.concatenate([s1, zpad], 0))


def _dt_cols():
    cols = []
    for g in range(SSD_GROUPS):
        for d in range(2):
            cols += [SDT_OFF + d * SSD_HEADS + g * HEADS_PER_GROUP + hh for hh in range(HEADS_PER_GROUP)]
    return cols


def _group_dt_param(v):
    out = []
    for g in range(SSD_GROUPS):
        hs = slice(g * HEADS_PER_GROUP, (g + 1) * HEADS_PER_GROUP)
        row = jnp.concatenate([v[0, hs], v[1, hs], jnp.zeros((LANES - 2 * HEADS_PER_GROUP,), v.dtype)])
        out.append(row.reshape(1, LANES))
    return jnp.stack(out, 0)


def _transpose_cast_kernel(w_ref, o_ref):
    o_ref[0] = w_ref[0].T.astype(o_ref.dtype)


def transpose_cast(w_t, first_row, n_rows, block, out_dtype):
    nl, _, k = w_t.shape
    assert first_row % (2 * SUBLANES) == 0 and n_rows % block == 0
    return pl.pallas_call(
        _transpose_cast_kernel, grid=(nl, n_rows // block),
        in_specs=[pl.BlockSpec((pl.Element(1), pl.Element(block), pl.Element(k)),
                               lambda l, j: (l, pl.multiple_of(first_row + j * block, 2 * SUBLANES), 0))],
        out_specs=pl.BlockSpec((1, k, block), lambda l, j: (l, 0, j)),
        out_shape=jax.ShapeDtypeStruct((nl, k, n_rows), out_dtype),
        compiler_params=_cparams(("arbitrary", "arbitrary")), name="transpose_cast",
    )(w_t)


def split_w_in(w_in):
    w_t = jnp.swapaxes(w_in, 1, 2)
    n_gate = w_in.shape[2] - MIX_COLS
    return (transpose_cast(w_t, 0, SDT_OFF, 256, BF16),
            transpose_cast(w_t, MIX_COLS, n_gate, 256, BF16),
            transpose_cast(w_t, SDT_OFF, LANES, LANES, F32))


def _dt_weight(dt_block):
    nl, d = dt_block.shape[:2]
    pad = jnp.zeros((nl, d, LANES - 2 * HEADS_PER_GROUP), dt_block.dtype)
    dtw = dt_block[:, :, :MIX_COLS - SDT_OFF].reshape(nl, d, 2, SSD_GROUPS, HEADS_PER_GROUP)
    parts = []
    for g in range(SSD_GROUPS):
        parts += [dtw[:, :, :, g, :].reshape(nl, d, 2 * HEADS_PER_GROUP), pad]
    return jnp.concatenate(parts, axis=2).astype(BF16)


def run_model(dims, x, c, ctx, c_ctx, w_mod_a, w_mod_b, b_mod, g_mix, g_ffn, w_in, w_up, w_o, q_norm, k_norm,
              rnn_conv_w, rnn_conv_b, rnn_lambda, rnn_w_r, rnn_b_r, rnn_w_i, rnn_b_i,
              ssd_conv_w, ssd_conv_b, ssd_dt_bias, ssd_a_log, ssd_d, ssd_norm,
              router_w, router_b, moe_w_gate, moe_w_up, moe_w_down, g_final):
    depth = w_in.shape[0]
    bsz = dims.batch
    xs = jnp.concatenate([x.reshape(dims.n_lat, D_MODEL), ctx.reshape(bsz * dims.ctx, D_MODEL)], axis=0)

    cond = jnp.zeros((SUBLANES, D_MODEL), F32).at[0].set(c_ctx).at[1:1 + bsz].set(c)
    mod_all = adaln_all(cond, w_mod_a, w_mod_b, b_mod).reshape(depth, SUBLANES * N_MOD, 1, D_MODEL)
    rope_c, rope_s0, rope_s1 = rope_tables(dims)
    router_wt = router_w.T.astype(BF16)

    w_mix_all, w_gate_all, dt_block = split_w_in(w_in)
    w_dt_all = _dt_weight(dt_block)
    w_up_all = w_up.astype(BF16)
    w_o_all = w_o.astype(BF16)
    moe_g_all = moe_w_gate.astype(BF16)
    moe_u_all = moe_w_up.astype(BF16)
    moe_d_all = moe_w_down.astype(BF16)

    for l in range(depth):
        last = l == depth - 1
        m_rows = dims.n_lat if last else dims.n_tok
        modtab = mod_all[l]

        h = norm_mod(dims, xs, g_mix[l], modtab, 0, dims.n_tok)
        p = matmul(h, w_mix_all, l, 768, F32)
        dt = matmul(h, w_dt_all, l, SSD_GROUPS * LANES, F32)

        qn, kn, vn = qkv_prep(dims, p, rope_c, rope_s0, rope_s1, q_norm[l], k_norm[l])
        ya = attention(dims, qn, kn, vn)

        w_gates = jnp.concatenate([rnn_w_r[l, 0], rnn_w_i[l, 0], rnn_w_r[l, 1], rnn_w_i[l, 1]], axis=-1).astype(BF16)
        b_gates = jnp.concatenate(
            [v.reshape(RNN_BLOCKS, 1, RNN_BLOCK_DIM) for v in (rnn_b_r[l, 0], rnn_b_i[l, 0], rnn_b_r[l, 1], rnn_b_i[l, 1])],
            axis=-1)
        lam = jnp.concatenate([rnn_lambda[l, d].reshape(RNN_BLOCKS, 1, RNN_BLOCK_DIM) for d in range(2)], axis=-1)
        yr = rglru_branch(dims, p, rnn_conv_w[l], rnn_conv_b[l], w_gates, b_gates, lam)

        xbc = ssd_prep(dims, p, ssd_conv_w[l], ssd_conv_b[l])
        yf, yb = ssd_scan(dims, xbc, dt, _group_dt_param(ssd_dt_bias[l]), _group_dt_param(ssd_a_log[l]))
        d_chan = jnp.repeat(ssd_d[l], SSD_HEAD_DIM).reshape(1, SSD_WIDTH)
        ys = ssd_finish(dims, yf, yb, xbc, p, d_chan, ssd_norm[l])

        merged = merge_branches(h, ya, yr, ys, w_gate_all, w_up_all, l, m_rows)
        xs = matmul_residual(dims, merged, w_o_all, l, xs, modtab, 2, m_rows)

        h2, logits_t = norm_mod(dims, xs, g_ffn[l], modtab, 3, m_rows, router_wt)
        cw_t, dest, tab = route(logits_t, router_b)
        dest = dest.reshape(m_rows)
        cw_rows = jnp.pad(cw_t.T, ((0, 0), (0, LANES - SUBLANES)))
        n_tiles = m_rows // MOE_TILE + N_EXPERT_GROUPS
        src = invert_slots(dest, n_tiles * MOE_TILE)
        ys2 = moe_grouped(h2, cw_rows, tab[0], tab[1, :1], tab[2], src, moe_g_all, moe_u_all, moe_d_all, l, n_tiles)
        xs = combine(dims, dest, xs, ys2, modtab, 5, m_rows)

    out = final_norm(xs, g_final, dims.n_lat)
    return out.reshape(bsz, dims.seq, D_MODEL)


def kernel(x, c, ctx, c_ctx, w_mod_a, w_mod_b, b_mod, g_mix, g_ffn, w_in, w_up, w_o, q_norm, k_norm, rnn_conv_w, rnn_conv_b, rnn_lambda, rnn_w_r, rnn_b_r, rnn_w_i, rnn_b_i, ssd_conv_w, ssd_conv_b, ssd_dt_bias, ssd_a_log, ssd_d, ssd_norm, router_w, router_b, moe_w_gate, moe_w_up, moe_w_down, g_final):
    dims = Dims(batch=x.shape[0], seq=x.shape[1], ctx=ctx.shape[1])
    return run_model(dims, x, c, ctx, c_ctx, w_mod_a, w_mod_b, b_mod, g_mix, g_ffn, w_in, w_up, w_o, q_norm, k_norm,
                     rnn_conv_w, rnn_conv_b, rnn_lambda, rnn_w_r, rnn_b_r, rnn_w_i, rnn_b_i,
                     ssd_conv_w, ssd_conv_b, ssd_dt_bias, ssd_a_log, ssd_d, ssd_norm,
                     router_w, router_b, moe_w_gate, moe_w_up, moe_w_down, g_final)
```

```python
import functools
from typing import NamedTuple

import jax
import jax.numpy as jnp
from jax import lax
from jax.experimental import pallas as pl
from jax.experimental.pallas import tpu as pltpu

F32 = jnp.float32
BF16 = jnp.bfloat16

D_MODEL = 4096
DEPTH = 4
GRID_W = 64
N_BRANCH = 3
BRANCH_WIDTH = 1024
HEAD_DIM = 128
N_Q_HEADS = 8
N_KV_HEADS = 2
Q_PER_KV = 4
KV_WIDTH = 256
N_ROPE_FREQ = 32
ROPE_THETA = 10000.0
RNN_WIDTH = 1024
RNN_BLOCKS = 8
RNN_BLOCK_DIM = 128
LRU_C = 8.0
CONV_W = 4
SSD_WIDTH = 1024
SSD_HEAD_DIM = 64
SSD_HEADS = 16
SSD_GROUPS = 2
SSD_STATE = 128
SSD_CHUNK = 128
SSD_BC = 256
SSD_XBC = 1536
HEADS_PER_GROUP = SSD_HEADS // SSD_GROUPS
GROUP_WIDTH = SSD_WIDTH // SSD_GROUPS
Q_OFF = 0
K_OFF = 1024
V_OFF = 1280
RX_OFF = 1536
RG_OFF = 2560
SZ_OFF = 3584
SX_OFF = 4608
SDT_OFF = 6144
MIX_COLS = 6176
MOD_RANK = 256
N_MOD = 6
N_EXPERTS = 16
N_EXPERT_GROUPS = 4
EXPERTS_PER_GROUP = 4
EXPERT_FF = 512
EPS = 1e-6

LANES = 128
SUBLANES = 8
VMEM_LIMIT_BYTES = 56 * 1024 * 1024
ROW_TILE = 256
SCAN_GROUPS = 4
MOE_TILE = 512
PAIRS = ((0, 1), (0, 2), (0, 3), (1, 3), (1, 2), (2, 3))
N_PAIRS = len(PAIRS)
ATTN_KEY_CHUNK = 1024
MM_TILE_M = 512
NEG_BIG = -1e30


class Dims(NamedTuple):
    batch: int
    seq: int
    ctx: int

    @property
    def n_lat(self):
        return self.batch * self.seq

    @property
    def n_tok(self):
        return self.batch * (self.seq + self.ctx)


def _cparams(sem):
    return pltpu.CompilerParams(dimension_semantics=sem, vmem_limit_bytes=VMEM_LIMIT_BYTES)


def _sigmoid(x):
    return 1.0 / (1.0 + jnp.exp(-x))


def _silu(x):
    return x * _sigmoid(x)


def _softplus(x):
    return jnp.maximum(x, 0.0) + jnp.log(1.0 + jnp.exp(-jnp.abs(x)))


def _gelu_tanh(x):
    return 0.5 * x * (1.0 + jnp.tanh(0.7978845608028654 * (x + 0.044715 * (x * x * x))))


def _dot(a, b):
    return jnp.dot(a, b, preferred_element_type=F32)


def _dot_nt(a, b):
    return lax.dot_general(a, b, (((1,), (1,)), ((), ())), preferred_element_type=F32)


def _mod_row(dims, tile, k):
    n_lat_tiles = dims.n_lat // tile
    per_batch = dims.seq // tile

    def f(i):
        return jnp.where(i < n_lat_tiles, 1 + i // per_batch, 0) * N_MOD + k
    return f


def _seg_block(dims, rows):
    lat_pb = dims.seq // rows
    ctx_pb = dims.ctx // rows
    ctx_base = dims.n_lat // rows

    def f(b, r):
        return jnp.where(r < lat_pb, b * lat_pb + r, ctx_base + b * ctx_pb + (r - lat_pb))
    return f


def _adaln_a_kernel(c_ref, w_ref, o_ref):
    c = c_ref[...]
    o_ref[0] = _dot(_silu(c).astype(BF16), w_ref[0].astype(BF16))


def _adaln_b_kernel(a_ref, w_ref, b_ref, o_ref):
    o_ref[0] = _dot(a_ref[0].astype(BF16), w_ref[0].astype(BF16)) + b_ref[0]


def adaln_all(cond, w_a, w_b, b):
    nl = w_a.shape[0]
    a = pl.pallas_call(
        _adaln_a_kernel,
        grid=(nl,),
        in_specs=[pl.BlockSpec((SUBLANES, D_MODEL), lambda l: (0, 0)),
                  pl.BlockSpec((1, D_MODEL, MOD_RANK), lambda l: (l, 0, 0))],
        out_specs=pl.BlockSpec((1, SUBLANES, MOD_RANK), lambda l: (l, 0, 0)),
        out_shape=jax.ShapeDtypeStruct((nl, SUBLANES, MOD_RANK), F32),
        compiler_params=_cparams(("arbitrary",)),
        name="adaln_a",
    )(cond, w_a)
    tn = 4096
    ncol = N_MOD * D_MODEL
    return pl.pallas_call(
        _adaln_b_kernel,
        grid=(nl, ncol // tn),
        in_specs=[pl.BlockSpec((1, SUBLANES, MOD_RANK), lambda l, j: (l, 0, 0)),
                  pl.BlockSpec((1, MOD_RANK, tn), lambda l, j: (l, 0, j)),
                  pl.BlockSpec((1, 1, tn), lambda l, j: (l, 0, j))],
        out_specs=pl.BlockSpec((1, SUBLANES, tn), lambda l, j: (l, 0, j)),
        out_shape=jax.ShapeDtypeStruct((nl, SUBLANES, ncol), F32),
        compiler_params=_cparams(("arbitrary", "arbitrary")),
        name="adaln_b",
    )(a, w_b, b.reshape(nl, 1, ncol))


def _norm_mod_kernel(x_ref, g_ref, sh_ref, sc_ref, o_ref):
    x = x_ref[...]
    y = x * lax.rsqrt(jnp.mean(x * x, axis=-1, keepdims=True) + EPS) * g_ref[...]
    o_ref[...] = (y * (1.0 + sc_ref[0]) + sh_ref[0]).astype(o_ref.dtype)


def _norm_mod_router_kernel(x_ref, g_ref, sh_ref, sc_ref, rw_ref, o_ref, lg_ref):
    x = x_ref[...]
    y = x * lax.rsqrt(jnp.mean(x * x, axis=-1, keepdims=True) + EPS) * g_ref[...]
    h = y * (1.0 + sc_ref[0]) + sh_ref[0]
    o_ref[...] = h
    lg_ref[...] = _dot_nt(rw_ref[...], h.astype(BF16))


def norm_mod(dims, x, gain, modtab, k, m_rows, router_wt=None):
    tm = ROW_TILE
    mrow = _mod_row(dims, tm, k)
    mrow1 = _mod_row(dims, tm, k + 1)
    in_specs = [pl.BlockSpec((tm, D_MODEL), lambda i: (i, 0)),
                pl.BlockSpec((1, D_MODEL), lambda i: (0, 0)),
                pl.BlockSpec((1, 1, D_MODEL), lambda i: (mrow(i), 0, 0)),
                pl.BlockSpec((1, 1, D_MODEL), lambda i: (mrow1(i), 0, 0))]
    h_spec = pl.BlockSpec((tm, D_MODEL), lambda i: (i, 0))
    if router_wt is None:
        return pl.pallas_call(
            _norm_mod_kernel, grid=(m_rows // tm,), in_specs=in_specs, out_specs=h_spec,
            out_shape=jax.ShapeDtypeStruct((m_rows, D_MODEL), BF16),
            compiler_params=_cparams(("arbitrary",)), name="norm_mod",
        )(x, gain.reshape(1, D_MODEL), modtab, modtab)
    return pl.pallas_call(
        _norm_mod_router_kernel, grid=(m_rows // tm,),
        in_specs=in_specs + [pl.BlockSpec((N_EXPERTS, D_MODEL), lambda i: (0, 0))],
        out_specs=[h_spec, pl.BlockSpec((N_EXPERTS, tm), lambda i: (0, i))],
        out_shape=[jax.ShapeDtypeStruct((m_rows, D_MODEL), F32),
                   jax.ShapeDtypeStruct((N_EXPERTS, m_rows), F32)],
        compiler_params=_cparams(("arbitrary",)), name="norm_mod_router",
    )(x, gain.reshape(1, D_MODEL), modtab, modtab, router_wt)


def _final_norm_kernel(x_ref, g_ref, o_ref):
    x = x_ref[...]
    o_ref[...] = x * lax.rsqrt(jnp.mean(x * x, axis=-1, keepdims=True) + EPS) * g_ref[...]


def final_norm(x, gain, m_rows):
    tm = ROW_TILE
    return pl.pallas_call(
        _final_norm_kernel, grid=(m_rows // tm,),
        in_specs=[pl.BlockSpec((tm, D_MODEL), lambda i: (i, 0)),
                  pl.BlockSpec((1, D_MODEL), lambda i: (0, 0))],
        out_specs=pl.BlockSpec((tm, D_MODEL), lambda i: (i, 0)),
        out_shape=jax.ShapeDtypeStruct((m_rows, D_MODEL), F32),
        compiler_params=_cparams(("arbitrary",)), name="final_norm",
    )(x, gain.reshape(1, D_MODEL))


def _mm_kernel(a_ref, w_ref, o_ref):
    o_ref[...] = _dot(a_ref[...], w_ref[0]).astype(o_ref.dtype)


def matmul(a, w_all, l, tn, out_dtype):
    m, k = a.shape
    n = w_all.shape[2]
    tm = MM_TILE_M
    return pl.pallas_call(
        _mm_kernel, grid=(n // tn, m // tm),
        in_specs=[pl.BlockSpec((tm, k), lambda j, i: (i, 0)),
                  pl.BlockSpec((1, k, tn), lambda j, i: (l, 0, j))],
        out_specs=pl.BlockSpec((tm, tn), lambda j, i: (i, j)),
        out_shape=jax.ShapeDtypeStruct((m, n), out_dtype),
        compiler_params=_cparams(("arbitrary", "arbitrary")), name="mm_in",
    )(a, w_all)


def _mm_res_kernel(a_ref, w_ref, x_ref, g_ref, o_ref):
    o_ref[...] = x_ref[...] + g_ref[0] * _dot(a_ref[...], w_ref[0])


def matmul_residual(dims, a, w_all, l, x, modtab, k, m_rows):
    kk = a.shape[1]
    n = w_all.shape[2]
    tm, tn = MM_TILE_M, 1024
    mrow = _mod_row(dims, tm, k)
    return pl.pallas_call(
        _mm_res_kernel, grid=(n // tn, m_rows // tm),
        in_specs=[pl.BlockSpec((tm, kk), lambda j, i: (i, 0)),
                  pl.BlockSpec((1, kk, tn), lambda j, i: (l, 0, j)),
                  pl.BlockSpec((tm, tn), lambda j, i: (i, j)),
                  pl.BlockSpec((1, 1, tn), lambda j, i: (mrow(i), 0, j))],
        out_specs=pl.BlockSpec((tm, tn), lambda j, i: (i, j)),
        out_shape=jax.ShapeDtypeStruct((m_rows, n), F32),
        compiler_params=_cparams(("arbitrary", "arbitrary")), name="mm_out_res",
    )(a, w_all, x, modtab)


def _merge_kernel(h_ref, ya_ref, yr_ref, ys_ref, wg0_ref, wg1_ref, wg2_ref,
                  wu0_ref, wu1_ref, wu2_ref, o_ref):
    h = h_ref[...]
    acc = _sigmoid(_dot(h, wg0_ref[0])) * _dot(ya_ref[...], wu0_ref[0, 0])
    acc += _sigmoid(_dot(h, wg1_ref[0])) * _dot(yr_ref[...], wu1_ref[0, 0])
    acc += _sigmoid(_dot(h, wg2_ref[0])) * _dot(ys_ref[...], wu2_ref[0, 0])
    o_ref[...] = acc.astype(o_ref.dtype)


def merge_branches(h, ya, yr, ys, w_gate_all, w_up_all, l, m_rows):
    tm, tn = MM_TILE_M, 512
    nj = D_MODEL // tn
    y_spec = pl.BlockSpec((tm, BRANCH_WIDTH), lambda j, i: (i, 0))

    def wg_spec(n):
        return pl.BlockSpec((1, D_MODEL, tn), lambda j, i: (l, 0, n * nj + j))

    def wu_spec(n):
        return pl.BlockSpec((1, 1, BRANCH_WIDTH, tn), lambda j, i: (l, n, 0, j))

    return pl.pallas_call(
        _merge_kernel, grid=(nj, m_rows // tm),
        in_specs=[pl.BlockSpec((tm, D_MODEL), lambda j, i: (i, 0)), y_spec, y_spec, y_spec,
                  wg_spec(0), wg_spec(1), wg_spec(2), wu_spec(0), wu_spec(1), wu_spec(2)],
        out_specs=pl.BlockSpec((tm, tn), lambda j, i: (i, j)),
        out_shape=jax.ShapeDtypeStruct((m_rows, D_MODEL), BF16),
        compiler_params=_cparams(("arbitrary", "arbitrary")), name="merge",
    )(h, ya, yr, ys, w_gate_all, w_gate_all, w_gate_all, w_up_all, w_up_all, w_up_all)


def _qkv_prep_kernel(q_ref, k_ref, v_ref, c_ref, s0_ref, s1_ref, qg_ref, kg_ref,
                     qo_ref, ko_ref, vo_ref):
    c, s0, s1 = c_ref[...], s0_ref[...], s1_ref[...]

    def head(xh, gain, scale):
        y = xh * lax.rsqrt(jnp.mean(xh * xh, axis=-1, keepdims=True) + EPS) * gain
        r = y * c + pltpu.roll(y, 96, 1) * s0 + pltpu.roll(y, 32, 1) * s1
        return r * scale

    for hh in range(N_Q_HEADS):
        sl = slice(hh * HEAD_DIM, (hh + 1) * HEAD_DIM)
        qo_ref[:, sl] = head(q_ref[:, sl], qg_ref[...], HEAD_DIM ** -0.5).astype(qo_ref.dtype)
    for hh in range(N_KV_HEADS):
        sl = slice(hh * HEAD_DIM, (hh + 1) * HEAD_DIM)
        ko_ref[:, sl] = head(k_ref[:, sl], kg_ref[...], 1.0).astype(ko_ref.dtype)
    vo_ref[...] = v_ref[...].astype(vo_ref.dtype)


def qkv_prep(dims, p, rope_c, rope_s0, rope_s1, q_gain, k_gain):
    tm = ROW_TILE
    n_lat_tiles = dims.n_lat // tm
    per_batch = dims.seq // tm

    def tab(i):
        return (jnp.where(i < n_lat_tiles, i % per_batch, per_batch), 0)

    m = dims.n_tok
    tab_spec = pl.BlockSpec((tm, HEAD_DIM), tab)
    g_spec = pl.BlockSpec((1, HEAD_DIM), lambda i: (0, 0))
    return pl.pallas_call(
        _qkv_prep_kernel, grid=(m // tm,),
        in_specs=[pl.BlockSpec((tm, BRANCH_WIDTH), lambda i: (i, Q_OFF // BRANCH_WIDTH)),
                  pl.BlockSpec((tm, KV_WIDTH), lambda i: (i, K_OFF // KV_WIDTH)),
                  pl.BlockSpec((tm, KV_WIDTH), lambda i: (i, V_OFF // KV_WIDTH)),
                  tab_spec, tab_spec, tab_spec, g_spec, g_spec],
        out_specs=[pl.BlockSpec((tm, BRANCH_WIDTH), lambda i: (i, 0)),
                   pl.BlockSpec((tm, KV_WIDTH), lambda i: (i, 0)),
                   pl.BlockSpec((tm, KV_WIDTH), lambda i: (i, 0))],
        out_shape=[jax.ShapeDtypeStruct((m, BRANCH_WIDTH), BF16),
                   jax.ShapeDtypeStruct((m, KV_WIDTH), BF16),
                   jax.ShapeDtypeStruct((m, KV_WIDTH), BF16)],
        compiler_params=_cparams(("arbitrary",)), name="qkv_prep",
    )(p, p, p, rope_c, rope_s0, rope_s1, q_gain.reshape(1, HEAD_DIM), k_gain.reshape(1, HEAD_DIM))


def _attn_kernel(q_ref, kl_ref, kc_ref, vl_ref, vc_ref, o_ref, sa_ref, sb_ref, *, lat_tiles, seq, ctx, kc):
    qi = pl.program_id(2)

    chunks = [(kl_ref, vl_ref, c * kc, kc, c * kc) for c in range(seq // kc)] + [(kc_ref, vc_ref, 0, ctx, seq)]

    def scores(g, chunk, s_ref):
        k_ref, _, r0, n, col = chunk
        s = _dot_nt(q_ref[:, g * HEAD_DIM:(g + 1) * HEAD_DIM], k_ref[r0:r0 + n, :])
        s_ref[:, col:col + n] = s
        return jnp.max(s, axis=-1, keepdims=True)

    def weighted(chunk, s_ref, mx):
        _, v_ref, r0, n, col = chunk
        p = jnp.exp(s_ref[:, col:col + n] - mx)
        return jnp.sum(p, axis=-1, keepdims=True), _dot(p.astype(BF16), v_ref[r0:r0 + n, :])

    @pl.when(qi < lat_tiles)
    def _():
        bufs = (sa_ref, sb_ref)
        mx = None
        for ch in chunks:
            cm = scores(0, ch, bufs[0])
            mx = cm if mx is None else jnp.maximum(mx, cm)
        for g in range(Q_PER_KV):
            cur, nxt = bufs[g % 2], bufs[(g + 1) % 2]
            den, acc, mx_next = None, None, None
            for ch in chunks:
                if g + 1 < Q_PER_KV:
                    cm = scores(g + 1, ch, nxt)
                    mx_next = cm if mx_next is None else jnp.maximum(mx_next, cm)
                ds, pv = weighted(ch, cur, mx)
                den = ds if den is None else den + ds
                acc = pv if acc is None else acc + pv
            o_ref[:, g * HEAD_DIM:(g + 1) * HEAD_DIM] = (acc / den).astype(o_ref.dtype)
            mx = mx_next

    @pl.when(qi >= lat_tiles)
    def _():
        for g in range(Q_PER_KV):
            sl = slice(g * HEAD_DIM, (g + 1) * HEAD_DIM)
            s_c = _dot_nt(q_ref[:, sl], kc_ref[...])
            p_c = jnp.exp(s_c - jnp.max(s_c, axis=-1, keepdims=True))
            den = jnp.sum(p_c, axis=-1, keepdims=True)
            o_ref[:, sl] = (_dot(p_c.astype(BF16), vc_ref[...]) / den).astype(o_ref.dtype)


def attention(dims, qn, kn, vn):
    tq = ROW_TILE
    assert dims.ctx == tq
    lat_tiles = dims.seq // tq
    seg = _seg_block(dims, tq)
    ctx_blk = dims.n_lat // dims.ctx
    gw = Q_PER_KV * HEAD_DIM
    kc = min(ATTN_KEY_CHUNK, dims.seq)
    assert dims.seq % kc == 0
    n_keys = dims.seq + dims.ctx
    return pl.pallas_call(
        functools.partial(_attn_kernel, lat_tiles=lat_tiles, seq=dims.seq, ctx=dims.ctx, kc=kc),
        grid=(dims.batch, N_KV_HEADS, lat_tiles + 1),
        scratch_shapes=[pltpu.VMEM((tq, n_keys), F32), pltpu.VMEM((tq, n_keys), F32)],
        in_specs=[pl.BlockSpec((tq, gw), lambda b, h, r: (seg(b, r), h)),
                  pl.BlockSpec((dims.seq, HEAD_DIM), lambda b, h, r: (b, h)),
                  pl.BlockSpec((dims.ctx, HEAD_DIM), lambda b, h, r: (ctx_blk + b, h)),
                  pl.BlockSpec((dims.seq, HEAD_DIM), lambda b, h, r: (b, h)),
                  pl.BlockSpec((dims.ctx, HEAD_DIM), lambda b, h, r: (ctx_blk + b, h))],
        out_specs=pl.BlockSpec((tq, gw), lambda b, h, r: (seg(b, r), h)),
        out_shape=jax.ShapeDtypeStruct((dims.n_tok, BRANCH_WIDTH), BF16),
        compiler_params=_cparams(("arbitrary", "arbitrary", "arbitrary")), name="attention",
    )(qn, kn, kn, vn, vn)


def _conv_rows(prev8, cur, next8, w_ref, b_ref):
    rows = cur.shape[0]
    ext = jnp.concatenate([prev8, cur, next8], axis=0)
    y = b_ref[...] + w_ref[0:1, :] * ext[7:7 + rows]
    y = y + w_ref[1:2, :] * cur
    y = y + w_ref[2:3, :] * ext[9:9 + rows]
    y = y + w_ref[3:4, :] * ext[10:10 + rows]
    return y


def _conv_chunk(src_ref, r0, seg_len, rows, w_ref, b_ref):
    cur = src_ref[pl.ds(r0, rows), :]
    p0 = pl.multiple_of(jnp.maximum(r0 - SUBLANES, 0), SUBLANES)
    n0 = pl.multiple_of(jnp.minimum(r0 + rows, seg_len - SUBLANES), SUBLANES)
    prev8 = src_ref[pl.ds(p0, SUBLANES), :] * jnp.where(r0 > 0, 1.0, 0.0)
    next8 = src_ref[pl.ds(n0, SUBLANES), :] * jnp.where(r0 + rows < seg_len, 1.0, 0.0)
    return _conv_rows(prev8, cur, next8, w_ref, b_ref)


def _rglru_kernel(rxl_ref, rxc_ref, rgl_ref, rgc_ref, cw_ref, cb_ref, wg_ref, bg_ref, lam_ref,
                  o_ref, af_s, uf_s, ab_s, ub_s, hf_s, hb_s, *, seq, ctx):
    r = pl.program_id(2)
    tot = seq + ctx
    rows = ROW_TILE
    w = RNN_BLOCK_DIM

    @pl.when(r == 0)
    def _():
        sp_f = _softplus(-lam_ref[0, :, 0:w])
        sp_b = _softplus(-lam_ref[0, :, w:2 * w])

        def gates(x, base):
            z = _dot(x.astype(BF16), wg_ref[0]) + bg_ref[0]
            sg = _sigmoid(z)
            for d, (sp, a_s, u_s) in enumerate(((sp_f, af_s, uf_s), (sp_b, ab_s, ub_s))):
                rg = sg[:, (2 * d) * w:(2 * d + 1) * w]
                ig = sg[:, (2 * d + 1) * w:(2 * d + 2) * w]
                a = jnp.exp(-LRU_C * rg * sp)
                a_s[pl.ds(base, rows), :] = a
                u_s[pl.ds(base, rows), :] = jnp.sqrt(1.0 - a * a) * (ig * x)

        for c0 in range(0, ctx, rows):
            gates(_conv_chunk(rxc_ref, c0, ctx, rows, cw_ref, cb_ref), c0)

        def lat_body(c, carry):
            r0 = pl.multiple_of(c * rows, rows)
            gates(_conv_chunk(rxl_ref, r0, seq, rows, cw_ref, cb_ref), pl.multiple_of(ctx + r0, rows))
            return carry
        lax.fori_loop(0, seq // rows, lat_body, 0)

        row = lax.broadcasted_iota(jnp.int32, (SUBLANES, w), 0)
        blk = SCAN_GROUPS * SUBLANES

        def scan_block(base, carry, a_s, u_s, out_s, reverse):
            a_blk = a_s[pl.ds(base, blk), :]
            u_blk = u_s[pl.ds(base, blk), :]
            hs = [None] * SCAN_GROUPS
            for g in (range(SCAN_GROUPS - 1, -1, -1) if reverse else range(SCAN_GROUPS)):
                a = a_blk[g * SUBLANES:(g + 1) * SUBLANES]
                u = u_blk[g * SUBLANES:(g + 1) * SUBLANES]
                for k in (1, 2, 4):
                    sh = SUBLANES - k if reverse else k
                    m = (row < SUBLANES - k) if reverse else (row >= k)
                    u = u + a * jnp.where(m, pltpu.roll(u, sh, 0), 0.0)
                    a = a * jnp.where(m, pltpu.roll(a, sh, 0), 1.0)
                hs[g] = u + a * carry
                e = 0 if reverse else SUBLANES - 1
                carry = (jnp.broadcast_to(u[e:e + 1, :], (SUBLANES, w))
                         + jnp.broadcast_to(a[e:e + 1, :], (SUBLANES, w)) * carry)
            out_s[pl.ds(base, blk), :] = jnp.concatenate(hs, axis=0)
            return carry

        def both(i, carry, f0, b0):
            cf, cb = carry
            cf = scan_block(pl.multiple_of(f0 + i * blk, blk), cf, af_s, uf_s, hf_s, False)
            cb = scan_block(pl.multiple_of(b0 - (i + 1) * blk, blk), cb, ab_s, ub_s, hb_s, True)
            return cf, cb

        zero = jnp.zeros((SUBLANES, w), F32)
        carry = lax.fori_loop(0, ctx // blk, lambda i, c: both(i, c, 0, ctx), (zero, zero))
        lax.fori_loop(0, seq // blk, lambda i, c: both(i, c, ctx, tot), carry)

        for c0 in range(0, ctx, rows):
            sl = pl.ds(c0, rows)
            hf_s[sl, :] = _gelu_tanh(rgc_ref[sl, :]) * (hf_s[sl, :] + hb_s[sl, :])

        def out_body(c, carry):
            r0 = pl.multiple_of(c * rows, rows)
            sl = pl.ds(pl.multiple_of(ctx + r0, rows), rows)
            hf_s[sl, :] = _gelu_tanh(rgl_ref[pl.ds(r0, rows), :]) * (hf_s[sl, :] + hb_s[sl, :])
            return carry
        lax.fori_loop(0, seq // rows, out_body, 0)

    lat_pieces = seq // rows
    src = jnp.where(r < lat_pieces, ctx + r * rows, (r - lat_pieces) * rows)
    o_ref[...] = hf_s[pl.ds(pl.multiple_of(src, rows), rows), :].astype(o_ref.dtype)


def rglru_branch(dims, p, conv_w, conv_b, w_gates, b_gates, lam):
    rows = ROW_TILE
    w = RNN_BLOCK_DIM
    pieces = (dims.seq + dims.ctx) // rows
    seg = _seg_block(dims, rows)
    ctx_blk = dims.n_lat // dims.ctx
    rx0, rg0 = RX_OFF // w, RG_OFF // w
    tot = dims.seq + dims.ctx
    return pl.pallas_call(
        functools.partial(_rglru_kernel, seq=dims.seq, ctx=dims.ctx),
        grid=(dims.batch, RNN_BLOCKS, pieces),
        in_specs=[pl.BlockSpec((dims.seq, w), lambda b, n, r: (b, rx0 + n)),
                  pl.BlockSpec((dims.ctx, w), lambda b, n, r: (ctx_blk + b, rx0 + n)),
                  pl.BlockSpec((dims.seq, w), lambda b, n, r: (b, rg0 + n)),
                  pl.BlockSpec((dims.ctx, w), lambda b, n, r: (ctx_blk + b, rg0 + n)),
                  pl.BlockSpec((CONV_W, w), lambda b, n, r: (0, n)),
                  pl.BlockSpec((1, w), lambda b, n, r: (0, n)),
                  pl.BlockSpec((1, w, 4 * w), lambda b, n, r: (n, 0, 0)),
                  pl.BlockSpec((1, 1, 4 * w), lambda b, n, r: (n, 0, 0)),
                  pl.BlockSpec((1, 1, 2 * w), lambda b, n, r: (n, 0, 0))],
        out_specs=pl.BlockSpec((rows, w), lambda b, n, r: (seg(b, r), n)),
        out_shape=jax.ShapeDtypeStruct((dims.n_tok, RNN_WIDTH), BF16),
        scratch_shapes=[pltpu.VMEM((tot, w), F32) for _ in range(6)],
        compiler_params=_cparams(("arbitrary", "arbitrary", "arbitrary")), name="rglru",
    )(p, p, p, p, conv_w, conv_b.reshape(1, RNN_WIDTH), w_gates, b_gates, lam)


def _ssd_prep_kernel(cur_ref, prev_ref, next_ref, w_ref, b_ref, o_ref, *, n_lat_tiles, per_batch):
    i = pl.program_id(0)
    j = i % per_batch
    lat = i < n_lat_tiles
    pv = jnp.where(jnp.logical_and(lat, j != 0), 1.0, 0.0)
    nv = jnp.where(jnp.logical_and(lat, j != per_batch - 1), 1.0, 0.0)
    y = _conv_rows(prev_ref[...] * pv, cur_ref[...], next_ref[...] * nv, w_ref, b_ref)
    o_ref[...] = _silu(y)


def ssd_prep(dims, p, conv_w, conv_b):
    tm = ROW_TILE
    assert dims.ctx == tm
    m = dims.n_tok
    hb = tm // SUBLANES
    last8 = m // SUBLANES - 1
    c0 = SX_OFF // SSD_XBC
    return pl.pallas_call(
        functools.partial(_ssd_prep_kernel, n_lat_tiles=dims.n_lat // tm, per_batch=dims.seq // tm),
        grid=(m // tm,),
        in_specs=[pl.BlockSpec((tm, SSD_XBC), lambda i: (i, c0)),
                  pl.BlockSpec((SUBLANES, SSD_XBC), lambda i: (jnp.maximum(i * hb - 1, 0), c0)),
                  pl.BlockSpec((SUBLANES, SSD_XBC), lambda i: (jnp.minimum((i + 1) * hb, last8), c0)),
                  pl.BlockSpec((CONV_W, SSD_XBC), lambda i: (0, 0)),
                  pl.BlockSpec((1, SSD_XBC), lambda i: (0, 0))],
        out_specs=pl.BlockSpec((tm, SSD_XBC), lambda i: (i, 0)),
        out_shape=jax.ShapeDtypeStruct((m, SSD_XBC), F32),
        compiler_params=_cparams(("arbitrary",)), name="ssd_prep",
    )(p, p, p, conv_w, conv_b.reshape(1, SSD_XBC))


def _ssd_chunk(x_ref, b_ref, c_ref, dt_ref, dtb_ref, alog_ref, s_ref, y_ref, reverse):
    q = SSD_CHUNK
    col0 = HEADS_PER_GROUP if reverse else 0
    ri = lax.broadcasted_iota(jnp.int32, (q, q), 0)
    ci = lax.broadcasted_iota(jnp.int32, (q, q), 1)
    tri = (ri <= ci) if reverse else (ri >= ci)
    cum = jnp.where(tri, 1.0, 0.0).astype(BF16)
    ones = jnp.ones((q, q), BF16)
    left = ci < SSD_HEAD_DIM
    top = ri < SSD_HEAD_DIM

    dtc = _softplus(dt_ref[...] + dtb_ref[0])
    a = dtc * (-jnp.exp(alog_ref[0]))
    a_hi = a.astype(BF16)
    r1 = a - a_hi.astype(F32)
    a_mid = r1.astype(BF16)
    a_lo = (r1 - a_mid.astype(F32)).astype(BF16)
    acum = _dot(cum, a_hi) + _dot(cum, a_mid) + _dot(cum, a_lo)
    atot = _dot(ones, a_hi) + _dot(ones, a_mid) + _dot(ones, a_lo)
    acum_t = acum.T

    bm = b_ref[...].astype(BF16)
    cm = c_ref[...].astype(BF16)
    cb = _dot_nt(cm, bm)

    for pair in range(HEADS_PER_GROUP // 2):
        ca = col0 + 2 * pair
        lanes = slice(pair * LANES, (pair + 1) * LANES)
        col_a, col_b = acum[:, ca:ca + 1], acum[:, ca + 1:ca + 2]
        row_a, row_b = acum_t[ca:ca + 1, :], acum_t[ca + 1:ca + 2, :]
        l_a = jnp.exp(jnp.where(tri, col_a - row_a, NEG_BIG))
        l_b = jnp.exp(jnp.where(tri, col_b - row_b, NEG_BIG))
        xdt = x_ref[:, lanes] * jnp.where(left, dtc[:, ca:ca + 1], dtc[:, ca + 1:ca + 2])
        xdt_b = xdt.astype(BF16)
        y_in = jnp.where(left, _dot((cb * l_a).astype(BF16), xdt_b), _dot((cb * l_b).astype(BF16), xdt_b))
        s_old = s_ref[lanes, :]
        y_st = _dot_nt(cm, s_old.astype(BF16)) * jnp.where(left, jnp.exp(col_a), jnp.exp(col_b))
        y_ref[:, lanes] = y_in + y_st
        tot_a, tot_b = atot[:, ca:ca + 1], atot[:, ca + 1:ca + 2]
        xw = xdt * jnp.where(left, jnp.exp(tot_a - col_a), jnp.exp(tot_b - col_b))
        s_new = _dot(xw.T.astype(BF16), bm)
        s_ref[lanes, :] = s_old * jnp.where(top, jnp.exp(tot_a), jnp.exp(tot_b)) + s_new


def _ssd_scan_kernel(xf_ref, bf_ref, cf_ref, dtf_ref, xb_ref, bb_ref, cb_ref, dtb_ref_,
                     bias_ref, alog_ref, yf_ref, yb_ref, sf_s, sb_s):
    @pl.when(pl.program_id(2) == 0)
    def _():
        sf_s[...] = jnp.zeros_like(sf_s)
        sb_s[...] = jnp.zeros_like(sb_s)

    _ssd_chunk(xf_ref, bf_ref, cf_ref, dtf_ref, bias_ref, alog_ref, sf_s, yf_ref, False)
    _ssd_chunk(xb_ref, bb_ref, cb_ref, dtb_ref_, bias_ref, alog_ref, sb_s, yb_ref, True)


def ssd_scan(dims, xbc, dt, dt_bias, a_log):
    q = SSD_CHUNK
    lat_c, ctx_c = dims.seq // q, dims.ctx // q
    ctx_base = dims.n_lat // q
    steps = lat_c + ctx_c

    def cf(b, s):
        return jnp.where(s < ctx_c, ctx_base + b * ctx_c + s, b * lat_c + (s - ctx_c))

    def cbk(b, s):
        return jnp.where(s < ctx_c, ctx_base + b * ctx_c + (ctx_c - 1 - s),
                         b * lat_c + (lat_c - 1 - (s - ctx_c)))

    gw = GROUP_WIDTH
    bcol, ccol = SSD_WIDTH // SSD_STATE, (SSD_WIDTH + SSD_BC) // SSD_STATE

    def specs(cfun):
        return [pl.BlockSpec((q, gw), lambda b, g, s: (cfun(b, s), g)),
                pl.BlockSpec((q, SSD_STATE), lambda b, g, s: (cfun(b, s), bcol + g)),
                pl.BlockSpec((q, SSD_STATE), lambda b, g, s: (cfun(b, s), ccol + g)),
                pl.BlockSpec((q, LANES), lambda b, g, s: (cfun(b, s), g))]

    par_spec = pl.BlockSpec((1, 1, LANES), lambda b, g, s: (g, 0, 0))
    yshape = jax.ShapeDtypeStruct((dims.n_tok, SSD_WIDTH), F32)
    return pl.pallas_call(
        _ssd_scan_kernel, grid=(dims.batch, SSD_GROUPS, steps),
        in_specs=specs(cf) + specs(cbk) + [par_spec, par_spec],
        out_specs=[pl.BlockSpec((q, gw), lambda b, g, s: (cf(b, s), g)),
                   pl.BlockSpec((q, gw), lambda b, g, s: (cbk(b, s), g))],
        out_shape=[yshape, yshape],
        scratch_shapes=[pltpu.VMEM((gw, SSD_STATE), F32), pltpu.VMEM((gw, SSD_STATE), F32)],
        compiler_params=_cparams(("arbitrary", "arbitrary", "arbitrary")), name="ssd_scan",
    )(xbc, xbc, xbc, dt, xbc, xbc, xbc, dt, dt_bias, a_log)


def _ssd_finish_kernel(yf_ref, yb_ref, x_ref, z_ref, d_ref, g_ref, o_ref):
    y = d_ref[...] * x_ref[...] + yf_ref[...] + yb_ref[...]
    gt = y * _silu(z_ref[...])
    o_ref[...] = (gt * lax.rsqrt(jnp.mean(gt * gt, axis=-1, keepdims=True) + EPS) * g_ref[...]).astype(o_ref.dtype)


def ssd_finish(dims, yf, yb, xbc, p, d_chan, norm_g):
    tm = ROW_TILE
    gw = GROUP_WIDTH
    z0 = SZ_OFF // gw
    blk = pl.BlockSpec((tm, gw), lambda i, g: (i, g))
    vec = pl.BlockSpec((1, gw), lambda i, g: (0, g))
    return pl.pallas_call(
        _ssd_finish_kernel, grid=(dims.n_tok // tm, SSD_GROUPS),
        in_specs=[blk, blk, blk, pl.BlockSpec((tm, gw), lambda i, g: (i, z0 + g)), vec, vec],
        out_specs=blk,
        out_shape=jax.ShapeDtypeStruct((dims.n_tok, SSD_WIDTH), BF16),
        compiler_params=_cparams(("arbitrary", "arbitrary")), name="ssd_finish",
    )(yf, yb, xbc, p, d_chan, norm_g.reshape(1, SSD_WIDTH))


def _route_kernel(lg_ref, rb_ref, cw_ref, dest_ref, tab_ref, st_s, *, m, tile):
    score = _sigmoid(lg_ref[...])
    sel = score + rb_ref[...]
    v = [sel[e:e + 1, :] for e in range(N_EXPERTS)]
    sc = [score[e:e + 1, :] for e in range(N_EXPERTS)]
    best, best_g = None, None
    for g in range(N_EXPERT_GROUPS):
        vg = v[4 * g:4 * g + 4]
        gs = vg[0] + vg[1]
        for (i, j) in ((0, 2), (0, 3), (1, 2), (1, 3), (2, 3)):
            gs = jnp.maximum(gs, vg[i] + vg[j])
        if g == 0:
            best, best_g = gs, jnp.zeros_like(gs, dtype=jnp.int32)
        else:
            better = gs > best
            best_g = jnp.where(better, g, best_g)
            best = jnp.where(better, gs, best)
    wts, wsel = [], []
    for e in range(N_EXPERTS):
        g = e // EXPERTS_PER_GROUP
        rank = jnp.zeros_like(best_g)
        for k in range(4 * g, 4 * g + 4):
            if k == e:
                continue
            ahead = (v[k] >= v[e]) if k < e else (v[k] > v[e])
            rank = rank + jnp.where(ahead, 1, 0)
        chosen = jnp.logical_and(best_g == g, rank < 2)
        wts.append(jnp.where(chosen, sc[e], 0.0))
        wsel.append(jnp.where(chosen, 1.0, 0.0))
    tot = wts[0]
    for e in range(1, N_EXPERTS):
        tot = tot + wts[e]
    for j in range(EXPERTS_PER_GROUP):
        wj = wts[j]
        for g in range(1, N_EXPERT_GROUPS):
            wj = wj + wts[EXPERTS_PER_GROUP * g + j]
        cw_ref[j:j + 1, :] = wj / tot
    cw_ref[EXPERTS_PER_GROUP:SUBLANES, :] = jnp.zeros((SUBLANES - EXPERTS_PER_GROUP, m), F32)

    pos = []
    for j in range(EXPERTS_PER_GROUP):
        cj = wsel[j]
        for g in range(1, N_EXPERT_GROUPS):
            cj = cj + wsel[EXPERTS_PER_GROUP * g + j]
        pos.append(cj > 0.5)
    first = jnp.where(pos[0], 0.0, jnp.where(pos[1], 1.0, 2.0))
    second = jnp.where(pos[3], 3.0, jnp.where(pos[2], 2.0, 1.0))
    lex = 3.0 * first - 0.5 * first * (first - 1.0) + (second - first - 1.0)
    pair = jnp.where(lex == 3.0, 4.0, jnp.where(lex == 4.0, 3.0, lex))
    key = best_g.astype(F32) * float(N_PAIRS) + pair
    n_key = N_EXPERT_GROUPS * N_PAIRS
    onehot = jnp.where(lax.broadcasted_iota(jnp.int32, (n_key, m), 0).astype(F32) == key, 1.0, 0.0)
    cb = jnp.broadcast_to(jnp.sum(onehot, axis=-1, keepdims=True), (n_key, LANES))
    starts, ends_valid, ends_pad = [], [], []
    g_start = jnp.zeros((1, LANES), F32)
    for g in range(N_EXPERT_GROUPS):
        cur = g_start
        for p in range(N_PAIRS):
            starts.append(cur)
            cur = cur + cb[N_PAIRS * g + p:N_PAIRS * g + p + 1, :]
        ends_valid.append(cur)
        g_start = g_start + jnp.floor((cur - g_start + (tile - 1)) * (1.0 / tile)) * tile
        ends_pad.append(g_start)
    for k in range(n_key):
        st_s[k:k + 1, :] = starts[k]
    st = st_s[...]
    ri = lax.broadcasted_iota(jnp.int32, (LANES, LANES), 0)
    ci = lax.broadcasted_iota(jnp.int32, (LANES, LANES), 1)
    before = jnp.where(ri < ci, 1.0, 0.0).astype(BF16)
    carry = jnp.zeros((n_key, 1), F32)
    for b in range(m // LANES):
        sl = slice(b * LANES, (b + 1) * LANES)
        ob = onehot[:, sl]
        slot = _dot(ob.astype(BF16), before) + carry + st
        dest_ref[:, sl] = jnp.sum(ob * slot, axis=0, keepdims=True).astype(jnp.int32)
        carry = carry + jnp.sum(ob, axis=-1, keepdims=True)
    t0 = lax.broadcasted_iota(jnp.int32, (1, LANES), 1).astype(F32) * tile
    t1 = t0 + tile
    tile_group = jnp.zeros((1, LANES), F32)
    for g in range(N_EXPERT_GROUPS):
        tile_group = tile_group + jnp.where(t0 >= ends_pad[g], 1.0, 0.0)
    group_end = jnp.zeros((1, LANES), F32)
    for g in range(N_EXPERT_GROUPS):
        group_end = group_end + jnp.where(tile_group == float(g), ends_valid[g], 0.0)
    rows = [tile_group, g_start * (1.0 / tile), jnp.clip(group_end - t0, 0.0, float(tile))]
    for j in range(EXPERTS_PER_GROUP):
        used = jnp.zeros((1, LANES), F32)
        for k in range(n_key):
            if j in PAIRS[k % N_PAIRS]:
                cnt = cb[k:k + 1, :]
                hit = jnp.logical_and(starts[k] < t1, starts[k] + cnt > t0)
                used = jnp.maximum(used, jnp.where(hit, 1.0, 0.0))
        rows.append(used)
    rows.append(jnp.zeros((1, LANES), F32))
    for r, row in enumerate(rows):
        tab_ref[r:r + 1, :] = row.astype(jnp.int32)


def route(logits_t, router_b):
    m = logits_t.shape[1]
    return pl.pallas_call(
        functools.partial(_route_kernel, m=m, tile=MOE_TILE), grid=(1,),
        in_specs=[pl.BlockSpec((N_EXPERTS, m), lambda i: (0, 0)),
                  pl.BlockSpec((N_EXPERTS, 1), lambda i: (0, 0))],
        out_specs=[pl.BlockSpec((SUBLANES, m), lambda i: (0, 0)),
                   pl.BlockSpec((1, m), lambda i: (0, 0)),
                   pl.BlockSpec((SUBLANES, LANES), lambda i: (0, 0))],
        out_shape=[jax.ShapeDtypeStruct((SUBLANES, m), F32),
                   jax.ShapeDtypeStruct((1, m), jnp.int32),
                   jax.ShapeDtypeStruct((SUBLANES, LANES), jnp.int32)],
        scratch_shapes=[pltpu.VMEM((N_EXPERT_GROUPS * N_PAIRS, LANES), F32)],
        compiler_params=_cparams(("arbitrary",)), name="route",
    )(logits_t, router_b.reshape(N_EXPERTS, 1))


def _invert_kernel(dest_ref, src_ref):
    def clear(s, c):
        src_ref[s] = 0
        return c
    lax.fori_loop(0, src_ref.shape[0], clear, 0, unroll=8)

    def place(t, c):
        src_ref[dest_ref[t]] = t
        return c
    lax.fori_loop(0, dest_ref.shape[0], place, 0, unroll=8)


def invert_slots(dest, n_slots):
    smem = pl.BlockSpec(memory_space=pltpu.SMEM)
    return pl.pallas_call(
        _invert_kernel, in_specs=[smem], out_specs=smem,
        out_shape=jax.ShapeDtypeStruct((n_slots,), jnp.int32),
        name="moe_invert",
    )(dest)


def _moe_group_kernel(tg_ref, nu_ref, nv_ref, used_ref, src_ref, h_ref, c_ref, wg_ref, wu_ref, wd_ref, o_ref,
                      xbuf, cbuf, xb_s, sem):
    tm = o_ref.shape[0]
    i = pl.program_id(0)
    j = pl.program_id(1)
    f = pl.program_id(2)
    n_used = nu_ref[0]
    active = i < n_used
    first = jnp.logical_and(j == 0, f == 0)
    slot = i % 2

    def gather(tile, buf_slot, wait):
        def body(r, c):
            s = src_ref[tile * tm + r]
            cps = (pltpu.make_async_copy(h_ref.at[pl.ds(s, 1)], xbuf.at[buf_slot, pl.ds(r, 1)], sem.at[0, buf_slot]),
                   pltpu.make_async_copy(c_ref.at[pl.ds(s, 1)], cbuf.at[buf_slot, pl.ds(r, 1)], sem.at[1, buf_slot]))
            for cp in cps:
                if wait:
                    cp.wait()
                else:
                    cp.start()
            return c
        lax.fori_loop(0, tm, body, 0, unroll=8)

    @pl.when(jnp.logical_and(first, i == 0))
    def _():
        gather(0, 0, False)

    @pl.when(jnp.logical_and(first, i + 1 < n_used))
    def _():
        gather(i + 1, (i + 1) % 2, False)

    @pl.when(jnp.logical_and(first, active))
    def _():
        gather(i, slot, True)
        xb_s[...] = xbuf[slot].astype(BF16)

    @pl.when(first)
    def _():
        o_ref[...] = jnp.zeros_like(o_ref)

    @pl.when(jnp.logical_and(active, used_ref[j * LANES + i] > 0))
    def _():
        xb = xb_s[...]
        g = _dot(xb, wg_ref[0, 0])
        u = _dot(xb, wu_ref[0, 0])
        c = cbuf[slot]
        lane = lax.broadcasted_iota(jnp.int32, c.shape, 1)
        rowi = lax.broadcasted_iota(jnp.int32, c.shape, 0)
        keep = jnp.logical_and(lane == j, rowi < nv_ref[i])
        cw = jnp.sum(jnp.where(keep, c, 0.0), axis=-1, keepdims=True)
        hid = _silu(g) * u * cw
        o_ref[...] += _dot(hid.astype(BF16), wd_ref[0, 0])


def moe_grouped(h, cw_rows, tile_group, n_used, n_valid, used, src, w_gate_all, w_up_all, w_down_all, l, n_tiles):
    tm = MOE_TILE
    tf = EXPERT_FF // 2
    last_f = EXPERT_FF // tf - 1

    def tile(i, nu):
        return jnp.minimum(i, nu[0] - 1)

    def expert(i, j, tg, nu):
        jj = jnp.where(i < nu[0], j, EXPERTS_PER_GROUP - 1)
        return tg[tile(i, nu)] * EXPERTS_PER_GROUP + jj

    def fblk(i, f, nu):
        return jnp.where(i < nu[0], f, last_f)

    any_spec = pl.BlockSpec(memory_space=pl.ANY)
    return pl.pallas_call(
        _moe_group_kernel,
        grid_spec=pltpu.PrefetchScalarGridSpec(
            num_scalar_prefetch=5, grid=(n_tiles, EXPERTS_PER_GROUP, EXPERT_FF // tf),
            in_specs=[any_spec, any_spec,
                      pl.BlockSpec((1, 1, D_MODEL, tf),
                                   lambda i, j, f, tg, nu, nv, us, sr: (l, expert(i, j, tg, nu), 0, fblk(i, f, nu))),
                      pl.BlockSpec((1, 1, D_MODEL, tf),
                                   lambda i, j, f, tg, nu, nv, us, sr: (l, expert(i, j, tg, nu), 0, fblk(i, f, nu))),
                      pl.BlockSpec((1, 1, tf, D_MODEL),
                                   lambda i, j, f, tg, nu, nv, us, sr: (l, expert(i, j, tg, nu), fblk(i, f, nu), 0))],
            out_specs=pl.BlockSpec((tm, D_MODEL), lambda i, j, f, tg, nu, nv, us, sr: (i, 0)),
            scratch_shapes=[pltpu.VMEM((2, tm, D_MODEL), F32), pltpu.VMEM((2, tm, LANES), F32),
                            pltpu.VMEM((tm, D_MODEL), BF16), pltpu.SemaphoreType.DMA((2, 2))]),
        out_shape=jax.ShapeDtypeStruct((n_tiles * tm, D_MODEL), F32),
        compiler_params=_cparams(("arbitrary", "arbitrary", "arbitrary")), name="moe_grouped",
    )(tile_group, n_used, n_valid, used, src, h, cw_rows, w_gate_all, w_up_all, w_down_all)


def _combine_kernel(dest_ref, x_ref, g_ref, ys_ref, o_ref, buf, sem):
    tm = x_ref.shape[0]
    i = pl.program_id(0)
    n = pl.num_programs(0)

    def gather(tile, slot, wait):
        def body(r, c):
            d = dest_ref[tile * tm + r]
            cp = pltpu.make_async_copy(ys_ref.at[pl.ds(d, 1)], buf.at[slot, pl.ds(r, 1)], sem.at[slot])
            if wait:
                cp.wait()
            else:
                cp.start()
            return c
        lax.fori_loop(0, tm, body, 0, unroll=8)

    @pl.when(i == 0)
    def _():
        gather(0, 0, False)

    @pl.when(i + 1 < n)
    def _():
        gather(i + 1, (i + 1) % 2, False)

    slot = i % 2
    gather(i, slot, True)
    o_ref[...] = x_ref[...] + g_ref[0] * buf[slot]


def combine(dims, dest, x, ys, modtab, k, m_rows):
    tm = ROW_TILE
    mrow = _mod_row(dims, tm, k)
    return pl.pallas_call(
        _combine_kernel,
        grid_spec=pltpu.PrefetchScalarGridSpec(
            num_scalar_prefetch=1, grid=(m_rows // tm,),
            in_specs=[pl.BlockSpec((tm, D_MODEL), lambda i, d: (i, 0)),
                      pl.BlockSpec((1, 1, D_MODEL), lambda i, d: (mrow(i), 0, 0)),
                      pl.BlockSpec(memory_space=pl.ANY)],
            out_specs=pl.BlockSpec((tm, D_MODEL), lambda i, d: (i, 0)),
            scratch_shapes=[pltpu.VMEM((2, tm, D_MODEL), F32), pltpu.SemaphoreType.DMA((2,))]),
        out_shape=jax.ShapeDtypeStruct((m_rows, D_MODEL), F32),
        compiler_params=_cparams(("arbitrary",)), name="moe_combine",
    )(dest, x, modtab, ys)


def rope_tables(dims):
    s = dims.seq
    rows = s // GRID_W
    row = jnp.repeat(jnp.arange(rows, dtype=F32), GRID_W)
    col = jnp.tile(jnp.arange(GRID_W, dtype=F32), rows)
    inv = ROPE_THETA ** (-jnp.arange(N_ROPE_FREQ, dtype=F32) / N_ROPE_FREQ)
    ang = jnp.stack([row[:, None] * inv, col[:, None] * inv], axis=1)
    cos, sin = jnp.cos(ang), jnp.sin(ang)
    zero = jnp.zeros_like(sin)
    c = jnp.stack([cos, cos], axis=2).reshape(s, HEAD_DIM)
    s0 = jnp.stack([-sin, zero], axis=2).reshape(s, HEAD_DIM)
    s1 = jnp.stack([zero, sin], axis=2).reshape(s, HEAD_DIM)
    ident = jnp.ones((ROW_TILE, HEAD_DIM), F32)
    zpad = jnp.zeros((ROW_TILE, HEAD_DIM), F32)
    return (jnp.concatenate([c, ident], 0), jnp.concatenate([s0, zpad], 0), jnp.concatenate([s1, zpad], 0))


def _dt_cols():
    cols = []
    for g in range(SSD_GROUPS):
        for d in range(2):
            cols += [SDT_OFF + d * SSD_HEADS + g * HEADS_PER_GROUP + hh for hh in range(HEADS_PER_GROUP)]
    return cols


def _group_dt_param(v):
    out = []
    for g in range(SSD_GROUPS):
        hs = slice(g * HEADS_PER_GROUP, (g + 1) * HEADS_PER_GROUP)
        row = jnp.concatenate([v[0, hs], v[1, hs], jnp.zeros((LANES - 2 * HEADS_PER_GROUP,), v.dtype)])
        out.append(row.reshape(1, LANES))
    return jnp.stack(out, 0)


def _transpose_cast_kernel(w_ref, o_ref):
    o_ref[0] = w_ref[0].T.astype(o_ref.dtype)


def transpose_cast(w_t, first_row, n_rows, block, out_dtype):
    nl, _, k = w_t.shape
    assert first_row % (2 * SUBLANES) == 0 and n_rows % block == 0
    return pl.pallas_call(
        _transpose_cast_kernel, grid=(nl, n_rows // block),
        in_specs=[pl.BlockSpec((pl.Element(1), pl.Element(block), pl.Element(k)),
                               lambda l, j: (l, pl.multiple_of(first_row + j * block, 2 * SUBLANES), 0))],
        out_specs=pl.BlockSpec((1, k, block), lambda l, j: (l, 0, j)),
        out_shape=jax.ShapeDtypeStruct((nl, k, n_rows), out_dtype),
        compiler_params=_cparams(("arbitrary", "arbitrary")), name="transpose_cast",
    )(w_t)


def split_w_in(w_in):
    w_t = jnp.swapaxes(w_in, 1, 2)
    n_gate = w_in.shape[2] - MIX_COLS
    return (transpose_cast(w_t, 0, SDT_OFF, 256, BF16),
            transpose_cast(w_t, MIX_COLS, n_gate, 256, BF16),
            transpose_cast(w_t, SDT_OFF, LANES, LANES, F32))


def _dt_weight(dt_block):
    nl, d = dt_block.shape[:2]
    pad = jnp.zeros((nl, d, LANES - 2 * HEADS_PER_GROUP), dt_block.dtype)
    dtw = dt_block[:, :, :MIX_COLS - SDT_OFF].reshape(nl, d, 2, SSD_GROUPS, HEADS_PER_GROUP)
    parts = []
    for g in range(SSD_GROUPS):
        parts += [dtw[:, :, :, g, :].reshape(nl, d, 2 * HEADS_PER_GROUP), pad]
    return jnp.concatenate(parts, axis=2).astype(BF16)


def run_model(dims, x, c, ctx, c_ctx, w_mod_a, w_mod_b, b_mod, g_mix, g_ffn, w_in, w_up, w_o, q_norm, k_norm,
              rnn_conv_w, rnn_conv_b, rnn_lambda, rnn_w_r, rnn_b_r, rnn_w_i, rnn_b_i,
              ssd_conv_w, ssd_conv_b, ssd_dt_bias, ssd_a_log, ssd_d, ssd_norm,
              router_w, router_b, moe_w_gate, moe_w_up, moe_w_down, g_final):
    depth = w_in.shape[0]
    bsz = dims.batch
    xs = jnp.concatenate([x.reshape(dims.n_lat, D_MODEL), ctx.reshape(bsz * dims.ctx, D_MODEL)], axis=0)

    cond = jnp.zeros((SUBLANES, D_MODEL), F32).at[0].set(c_ctx).at[1:1 + bsz].set(c)
    mod_all = adaln_all(cond, w_mod_a, w_mod_b, b_mod).reshape(depth, SUBLANES * N_MOD, 1, D_MODEL)
    rope_c, rope_s0, rope_s1 = rope_tables(dims)
    router_wt = router_w.T.astype(BF16)

    w_mix_all, w_gate_all, dt_block = split_w_in(w_in)
    w_dt_all = _dt_weight(dt_block)
    w_up_all = w_up.astype(BF16)
    w_o_all = w_o.astype(BF16)
    moe_g_all = moe_w_gate.astype(BF16)
    moe_u_all = moe_w_up.astype(BF16)
    moe_d_all = moe_w_down.astype(BF16)

    for l in range(depth):
        last = l == depth - 1
        m_rows = dims.n_lat if last else dims.n_tok
        modtab = mod_all[l]

        h = norm_mod(dims, xs, g_mix[l], modtab, 0, dims.n_tok)
        p = matmul(h, w_mix_all, l, 768, F32)
        dt = matmul(h, w_dt_all, l, SSD_GROUPS * LANES, F32)

        qn, kn, vn = qkv_prep(dims, p, rope_c, rope_s0, rope_s1, q_norm[l], k_norm[l])
        ya = attention(dims, qn, kn, vn)

        w_gates = jnp.concatenate([rnn_w_r[l, 0], rnn_w_i[l, 0], rnn_w_r[l, 1], rnn_w_i[l, 1]], axis=-1).astype(BF16)
        b_gates = jnp.concatenate(
            [v.reshape(RNN_BLOCKS, 1, RNN_BLOCK_DIM) for v in (rnn_b_r[l, 0], rnn_b_i[l, 0], rnn_b_r[l, 1], rnn_b_i[l, 1])],
            axis=-1)
        lam = jnp.concatenate([rnn_lambda[l, d].reshape(RNN_BLOCKS, 1, RNN_BLOCK_DIM) for d in range(2)], axis=-1)
        yr = rglru_branch(dims, p, rnn_conv_w[l], rnn_conv_b[l], w_gates, b_gates, lam)

        xbc = ssd_prep(dims, p, ssd_conv_w[l], ssd_conv_b[l])
        yf, yb = ssd_scan(dims, xbc, dt, _group_dt_param(ssd_dt_bias[l]), _group_dt_param(ssd_a_log[l]))
        d_chan = jnp.repeat(ssd_d[l], SSD_HEAD_DIM).reshape(1, SSD_WIDTH)
        ys = ssd_finish(dims, yf, yb, xbc, p, d_chan, ssd_norm[l])

        merged = merge_branches(h, ya, yr, ys, w_gate_all, w_up_all, l, m_rows)
        xs = matmul_residual(dims, merged, w_o_all, l, xs, modtab, 2, m_rows)

        h2, logits_t = norm_mod(dims, xs, g_ffn[l], modtab, 3, m_rows, router_wt)
        cw_t, dest, tab = route(logits_t, router_b)
        dest = dest.reshape(m_rows)
        cw_rows = jnp.pad(cw_t.T, ((0, 0), (0, LANES - SUBLANES)))
        n_tiles = m_rows // MOE_TILE + N_EXPERT_GROUPS
        src = invert_slots(dest, n_tiles * MOE_TILE)
        used = tab[3:3 + EXPERTS_PER_GROUP].reshape(EXPERTS_PER_GROUP * LANES)
        ys2 = moe_grouped(h2, cw_rows, tab[0], tab[1, :1], tab[2], used, src, moe_g_all, moe_u_all, moe_d_all, l, n_tiles)
        xs = combine(dims, dest, xs, ys2, modtab, 5, m_rows)

    out = final_norm(xs, g_final, dims.n_lat)
    return out.reshape(bsz, dims.seq, D_MODEL)


def kernel(x, c, ctx, c_ctx, w_mod_a, w_mod_b, b_mod, g_mix, g_ffn, w_in, w_up, w_o, q_norm, k_norm, rnn_conv_w, rnn_conv_b, rnn_lambda, rnn_w_r, rnn_b_r, rnn_w_i, rnn_b_i, ssd_conv_w, ssd_conv_b, ssd_dt_bias, ssd_a_log, ssd_d, ssd_norm, router_w, router_b, moe_w_gate, moe_w_up, moe_w_down, g_final):
    dims = Dims(batch=x.shape[0], seq=x.shape[1], ctx=ctx.shape[1])
    return run_model(dims, x, c, ctx, c_ctx, w_mod_a, w_mod_b, b_mod, g_mix, g_ffn, w_in, w_up, w_o, q_norm, k_norm,
                     rnn_conv_w, rnn_conv_b, rnn_lambda, rnn_w_r, rnn_b_r, rnn_w_i, rnn_b_i,
                     ssd_conv_w, ssd_conv_b, ssd_dt_bias, ssd_a_log, ssd_d, ssd_norm,
                     router_w, router_b, moe_w_gate, moe_w_up, moe_w_down, g_final)
```

```python
import functools
from typing import NamedTuple

import jax
import jax.numpy as jnp
from jax import lax
from jax.experimental import pallas as pl
from jax.experimental.pallas import tpu as pltpu

F32 = jnp.float32
BF16 = jnp.bfloat16

D_MODEL = 4096
DEPTH = 4
GRID_W = 64
N_BRANCH = 3
BRANCH_WIDTH = 1024
HEAD_DIM = 128
N_Q_HEADS = 8
N_KV_HEADS = 2
Q_PER_KV = 4
KV_WIDTH = 256
N_ROPE_FREQ = 32
ROPE_THETA = 10000.0
RNN_WIDTH = 1024
RNN_BLOCKS = 8
RNN_BLOCK_DIM = 128
LRU_C = 8.0
CONV_W = 4
SSD_WIDTH = 1024
SSD_HEAD_DIM = 64
SSD_HEADS = 16
SSD_GROUPS = 2
SSD_STATE = 128
SSD_CHUNK = 128
SSD_BC = 256
SSD_XBC = 1536
HEADS_PER_GROUP = SSD_HEADS // SSD_GROUPS
GROUP_WIDTH = SSD_WIDTH // SSD_GROUPS
Q_OFF = 0
K_OFF = 1024
V_OFF = 1280
RX_OFF = 1536
RG_OFF = 2560
SZ_OFF = 3584
SX_OFF = 4608
SDT_OFF = 6144
MIX_COLS = 6176
MOD_RANK = 256
N_MOD = 6
N_EXPERTS = 16
N_EXPERT_GROUPS = 4
EXPERTS_PER_GROUP = 4
EXPERT_FF = 512
EPS = 1e-6

LANES = 128
SUBLANES = 8
VMEM_LIMIT_BYTES = 56 * 1024 * 1024
ROW_TILE = 256
SCAN_GROUPS = 4
MOE_TILE = 512
PAIRS = ((0, 1), (0, 2), (0, 3), (1, 3), (1, 2), (2, 3))
N_PAIRS = len(PAIRS)
ATTN_KEY_CHUNK = 1024
MM_TILE_M = 512
NEG_BIG = -1e30


class Dims(NamedTuple):
    batch: int
    seq: int
    ctx: int

    @property
    def n_lat(self):
        return self.batch * self.seq

    @property
    def n_tok(self):
        return self.batch * (self.seq + self.ctx)


def _cparams(sem):
    return pltpu.CompilerParams(dimension_semantics=sem, vmem_limit_bytes=VMEM_LIMIT_BYTES)


def _sigmoid(x):
    return 1.0 / (1.0 + jnp.exp(-x))


def _silu(x):
    return x * _sigmoid(x)


def _softplus(x):
    return jnp.maximum(x, 0.0) + jnp.log(1.0 + jnp.exp(-jnp.abs(x)))


def _gelu_tanh(x):
    return 0.5 * x * (1.0 + jnp.tanh(0.7978845608028654 * (x + 0.044715 * (x * x * x))))


def _dot(a, b):
    return jnp.dot(a, b, preferred_element_type=F32)


def _dot_nt(a, b):
    return lax.dot_general(a, b, (((1,), (1,)), ((), ())), preferred_element_type=F32)


def _mod_row(dims, tile, k):
    n_lat_tiles = dims.n_lat // tile
    per_batch = dims.seq // tile

    def f(i):
        return jnp.where(i < n_lat_tiles, 1 + i // per_batch, 0) * N_MOD + k
    return f


def _seg_block(dims, rows):
    lat_pb = dims.seq // rows
    ctx_pb = dims.ctx // rows
    ctx_base = dims.n_lat // rows

    def f(b, r):
        return jnp.where(r < lat_pb, b * lat_pb + r, ctx_base + b * ctx_pb + (r - lat_pb))
    return f


def _adaln_a_kernel(c_ref, w_ref, o_ref):
    c = c_ref[...]
    o_ref[0] = _dot(_silu(c).astype(BF16), w_ref[0].astype(BF16))


def _adaln_b_kernel(a_ref, w_ref, b_ref, o_ref):
    o_ref[0] = _dot(a_ref[0].astype(BF16), w_ref[0].astype(BF16)) + b_ref[0]


def adaln_all(cond, w_a, w_b, b):
    nl = w_a.shape[0]
    a = pl.pallas_call(
        _adaln_a_kernel,
        grid=(nl,),
        in_specs=[pl.BlockSpec((SUBLANES, D_MODEL), lambda l: (0, 0)),
                  pl.BlockSpec((1, D_MODEL, MOD_RANK), lambda l: (l, 0, 0))],
        out_specs=pl.BlockSpec((1, SUBLANES, MOD_RANK), lambda l: (l, 0, 0)),
        out_shape=jax.ShapeDtypeStruct((nl, SUBLANES, MOD_RANK), F32),
        compiler_params=_cparams(("arbitrary",)),
        name="adaln_a",
    )(cond, w_a)
    tn = 4096
    ncol = N_MOD * D_MODEL
    return pl.pallas_call(
        _adaln_b_kernel,
        grid=(nl, ncol // tn),
        in_specs=[pl.BlockSpec((1, SUBLANES, MOD_RANK), lambda l, j: (l, 0, 0)),
                  pl.BlockSpec((1, MOD_RANK, tn), lambda l, j: (l, 0, j)),
                  pl.BlockSpec((1, 1, tn), lambda l, j: (l, 0, j))],
        out_specs=pl.BlockSpec((1, SUBLANES, tn), lambda l, j: (l, 0, j)),
        out_shape=jax.ShapeDtypeStruct((nl, SUBLANES, ncol), F32),
        compiler_params=_cparams(("arbitrary", "arbitrary")),
        name="adaln_b",
    )(a, w_b, b.reshape(nl, 1, ncol))


def _norm_mod_kernel(x_ref, g_ref, sh_ref, sc_ref, o_ref):
    x = x_ref[...]
    y = x * lax.rsqrt(jnp.mean(x * x, axis=-1, keepdims=True) + EPS) * g_ref[...]
    o_ref[...] = (y * (1.0 + sc_ref[0]) + sh_ref[0]).astype(o_ref.dtype)


def _norm_mod_router_kernel(x_ref, g_ref, sh_ref, sc_ref, rw_ref, o_ref, lg_ref):
    x = x_ref[...]
    y = x * lax.rsqrt(jnp.mean(x * x, axis=-1, keepdims=True) + EPS) * g_ref[...]
    h = y * (1.0 + sc_ref[0]) + sh_ref[0]
    o_ref[...] = h
    lg_ref[...] = _dot_nt(rw_ref[...], h.astype(BF16))


def norm_mod(dims, x, gain, modtab, k, m_rows, router_wt=None):
    tm = ROW_TILE
    mrow = _mod_row(dims, tm, k)
    mrow1 = _mod_row(dims, tm, k + 1)
    in_specs = [pl.BlockSpec((tm, D_MODEL), lambda i: (i, 0)),
                pl.BlockSpec((1, D_MODEL), lambda i: (0, 0)),
                pl.BlockSpec((1, 1, D_MODEL), lambda i: (mrow(i), 0, 0)),
                pl.BlockSpec((1, 1, D_MODEL), lambda i: (mrow1(i), 0, 0))]
    h_spec = pl.BlockSpec((tm, D_MODEL), lambda i: (i, 0))
    if router_wt is None:
        return pl.pallas_call(
            _norm_mod_kernel, grid=(m_rows // tm,), in_specs=in_specs, out_specs=h_spec,
            out_shape=jax.ShapeDtypeStruct((m_rows, D_MODEL), BF16),
            compiler_params=_cparams(("arbitrary",)), name="norm_mod",
        )(x, gain.reshape(1, D_MODEL), modtab, modtab)
    return pl.pallas_call(
        _norm_mod_router_kernel, grid=(m_rows // tm,),
        in_specs=in_specs + [pl.BlockSpec((N_EXPERTS, D_MODEL), lambda i: (0, 0))],
        out_specs=[h_spec, pl.BlockSpec((N_EXPERTS, tm), lambda i: (0, i))],
        out_shape=[jax.ShapeDtypeStruct((m_rows, D_MODEL), F32),
                   jax.ShapeDtypeStruct((N_EXPERTS, m_rows), F32)],
        compiler_params=_cparams(("arbitrary",)), name="norm_mod_router",
    )(x, gain.reshape(1, D_MODEL), modtab, modtab, router_wt)


def _final_norm_kernel(x_ref, g_ref, o_ref):
    x = x_ref[...]
    o_ref[...] = x * lax.rsqrt(jnp.mean(x * x, axis=-1, keepdims=True) + EPS) * g_ref[...]


def final_norm(x, gain, m_rows):
    tm = ROW_TILE
    return pl.pallas_call(
        _final_norm_kernel, grid=(m_rows // tm,),
        in_specs=[pl.BlockSpec((tm, D_MODEL), lambda i: (i, 0)),
                  pl.BlockSpec((1, D_MODEL), lambda i: (0, 0))],
        out_specs=pl.BlockSpec((tm, D_MODEL), lambda i: (i, 0)),
        out_shape=jax.ShapeDtypeStruct((m_rows, D_MODEL), F32),
        compiler_params=_cparams(("arbitrary",)), name="final_norm",
    )(x, gain.reshape(1, D_MODEL))


def _mm_kernel(a_ref, w_ref, o_ref):
    o_ref[...] = _dot(a_ref[...], w_ref[0]).astype(o_ref.dtype)


def matmul(a, w_all, l, tn, out_dtype):
    m, k = a.shape
    n = w_all.shape[2]
    tm = MM_TILE_M
    return pl.pallas_call(
        _mm_kernel, grid=(n // tn, m // tm),
        in_specs=[pl.BlockSpec((tm, k), lambda j, i: (i, 0)),
                  pl.BlockSpec((1, k, tn), lambda j, i: (l, 0, j))],
        out_specs=pl.BlockSpec((tm, tn), lambda j, i: (i, j)),
        out_shape=jax.ShapeDtypeStruct((m, n), out_dtype),
        compiler_params=_cparams(("arbitrary", "arbitrary")), name="mm_in",
    )(a, w_all)


def _mm_res_kernel(a_ref, w_ref, x_ref, g_ref, o_ref):
    o_ref[...] = x_ref[...] + g_ref[0] * _dot(a_ref[...], w_ref[0])


def matmul_residual(dims, a, w_all, l, x, modtab, k, m_rows):
    kk = a.shape[1]
    n = w_all.shape[2]
    tm, tn = MM_TILE_M, 1024
    mrow = _mod_row(dims, tm, k)
    return pl.pallas_call(
        _mm_res_kernel, grid=(n // tn, m_rows // tm),
        in_specs=[pl.BlockSpec((tm, kk), lambda j, i: (i, 0)),
                  pl.BlockSpec((1, kk, tn), lambda j, i: (l, 0, j)),
                  pl.BlockSpec((tm, tn), lambda j, i: (i, j)),
                  pl.BlockSpec((1, 1, tn), lambda j, i: (mrow(i), 0, j))],
        out_specs=pl.BlockSpec((tm, tn), lambda j, i: (i, j)),
        out_shape=jax.ShapeDtypeStruct((m_rows, n), F32),
        compiler_params=_cparams(("arbitrary", "arbitrary")), name="mm_out_res",
    )(a, w_all, x, modtab)


def _merge_kernel(h_ref, ya_ref, yr_ref, ys_ref, wg0_ref, wg1_ref, wg2_ref,
                  wu0_ref, wu1_ref, wu2_ref, o_ref):
    h = h_ref[...]
    acc = _sigmoid(_dot(h, wg0_ref[0])) * _dot(ya_ref[...], wu0_ref[0, 0])
    acc += _sigmoid(_dot(h, wg1_ref[0])) * _dot(yr_ref[...], wu1_ref[0, 0])
    acc += _sigmoid(_dot(h, wg2_ref[0])) * _dot(ys_ref[...], wu2_ref[0, 0])
    o_ref[...] = acc.astype(o_ref.dtype)


def merge_branches(h, ya, yr, ys, w_gate_all, w_up_all, l, m_rows):
    tm, tn = MM_TILE_M, 512
    nj = D_MODEL // tn
    y_spec = pl.BlockSpec((tm, BRANCH_WIDTH), lambda j, i: (i, 0))

    def wg_spec(n):
        return pl.BlockSpec((1, D_MODEL, tn), lambda j, i: (l, 0, n * nj + j))

    def wu_spec(n):
        return pl.BlockSpec((1, 1, BRANCH_WIDTH, tn), lambda j, i: (l, n, 0, j))

    return pl.pallas_call(
        _merge_kernel, grid=(nj, m_rows // tm),
        in_specs=[pl.BlockSpec((tm, D_MODEL), lambda j, i: (i, 0)), y_spec, y_spec, y_spec,
                  wg_spec(0), wg_spec(1), wg_spec(2), wu_spec(0), wu_spec(1), wu_spec(2)],
        out_specs=pl.BlockSpec((tm, tn), lambda j, i: (i, j)),
        out_shape=jax.ShapeDtypeStruct((m_rows, D_MODEL), BF16),
        compiler_params=_cparams(("arbitrary", "arbitrary")), name="merge",
    )(h, ya, yr, ys, w_gate_all, w_gate_all, w_gate_all, w_up_all, w_up_all, w_up_all)


def _qkv_prep_kernel(q_ref, k_ref, v_ref, c_ref, s0_ref, s1_ref, qg_ref, kg_ref,
                     qo_ref, ko_ref, vo_ref):
    c, s0, s1 = c_ref[...], s0_ref[...], s1_ref[...]

    def head(xh, gain, scale):
        y = xh * lax.rsqrt(jnp.mean(xh * xh, axis=-1, keepdims=True) + EPS) * gain
        r = y * c + pltpu.roll(y, 96, 1) * s0 + pltpu.roll(y, 32, 1) * s1
        return r * scale

    for hh in range(N_Q_HEADS):
        sl = slice(hh * HEAD_DIM, (hh + 1) * HEAD_DIM)
        qo_ref[:, sl] = head(q_ref[:, sl], qg_ref[...], HEAD_DIM ** -0.5).astype(qo_ref.dtype)
    for hh in range(N_KV_HEADS):
        sl = slice(hh * HEAD_DIM, (hh + 1) * HEAD_DIM)
        ko_ref[:, sl] = head(k_ref[:, sl], kg_ref[...], 1.0).astype(ko_ref.dtype)
    vo_ref[...] = v_ref[...].astype(vo_ref.dtype)


def qkv_prep(dims, p, rope_c, rope_s0, rope_s1, q_gain, k_gain):
    tm = ROW_TILE
    n_lat_tiles = dims.n_lat // tm
    per_batch = dims.seq // tm

    def tab(i):
        return (jnp.where(i < n_lat_tiles, i % per_batch, per_batch), 0)

    m = dims.n_tok
    tab_spec = pl.BlockSpec((tm, HEAD_DIM), tab)
    g_spec = pl.BlockSpec((1, HEAD_DIM), lambda i: (0, 0))
    return pl.pallas_call(
        _qkv_prep_kernel, grid=(m // tm,),
        in_specs=[pl.BlockSpec((tm, BRANCH_WIDTH), lambda i: (i, Q_OFF // BRANCH_WIDTH)),
                  pl.BlockSpec((tm, KV_WIDTH), lambda i: (i, K_OFF // KV_WIDTH)),
                  pl.BlockSpec((tm, KV_WIDTH), lambda i: (i, V_OFF // KV_WIDTH)),
                  tab_spec, tab_spec, tab_spec, g_spec, g_spec],
        out_specs=[pl.BlockSpec((tm, BRANCH_WIDTH), lambda i: (i, 0)),
                   pl.BlockSpec((tm, KV_WIDTH), lambda i: (i, 0)),
                   pl.BlockSpec((tm, KV_WIDTH), lambda i: (i, 0))],
        out_shape=[jax.ShapeDtypeStruct((m, BRANCH_WIDTH), BF16),
                   jax.ShapeDtypeStruct((m, KV_WIDTH), BF16),
                   jax.ShapeDtypeStruct((m, KV_WIDTH), BF16)],
        compiler_params=_cparams(("arbitrary",)), name="qkv_prep",
    )(p, p, p, rope_c, rope_s0, rope_s1, q_gain.reshape(1, HEAD_DIM), k_gain.reshape(1, HEAD_DIM))


def _attn_kernel(q_ref, kl_ref, kc_ref, vl_ref, vc_ref, o_ref, sa_ref, sb_ref, *, lat_tiles, seq, ctx, kc):
    qi = pl.program_id(2)

    chunks = [(kl_ref, vl_ref, c * kc, kc, c * kc) for c in range(seq // kc)] + [(kc_ref, vc_ref, 0, ctx, seq)]

    def scores(g, chunk, s_ref):
        k_ref, _, r0, n, col = chunk
        s = _dot_nt(q_ref[:, g * HEAD_DIM:(g + 1) * HEAD_DIM], k_ref[r0:r0 + n, :])
        s_ref[:, col:col + n] = s
        return jnp.max(s, axis=-1, keepdims=True)

    def weighted(chunk, s_ref, mx):
        _, v_ref, r0, n, col = chunk
        p = jnp.exp(s_ref[:, col:col + n] - mx)
        return jnp.sum(p, axis=-1, keepdims=True), _dot(p.astype(BF16), v_ref[r0:r0 + n, :])

    @pl.when(qi < lat_tiles)
    def _():
        bufs = (sa_ref, sb_ref)
        mx = None
        for ch in chunks:
            cm = scores(0, ch, bufs[0])
            mx = cm if mx is None else jnp.maximum(mx, cm)
        for g in range(Q_PER_KV):
            cur, nxt = bufs[g % 2], bufs[(g + 1) % 2]
            den, acc, mx_next = None, None, None
            for ch in chunks:
                if g + 1 < Q_PER_KV:
                    cm = scores(g + 1, ch, nxt)
                    mx_next = cm if mx_next is None else jnp.maximum(mx_next, cm)
                ds, pv = weighted(ch, cur, mx)
                den = ds if den is None else den + ds
                acc = pv if acc is None else acc + pv
            o_ref[:, g * HEAD_DIM:(g + 1) * HEAD_DIM] = (acc / den).astype(o_ref.dtype)
            mx = mx_next

    @pl.when(qi >= lat_tiles)
    def _():
        for g in range(Q_PER_KV):
            sl = slice(g * HEAD_DIM, (g + 1) * HEAD_DIM)
            s_c = _dot_nt(q_ref[:, sl], kc_ref[...])
            p_c = jnp.exp(s_c - jnp.max(s_c, axis=-1, keepdims=True))
            den = jnp.sum(p_c, axis=-1, keepdims=True)
            o_ref[:, sl] = (_dot(p_c.astype(BF16), vc_ref[...]) / den).astype(o_ref.dtype)


def attention(dims, qn, kn, vn):
    tq = ROW_TILE
    assert dims.ctx == tq
    lat_tiles = dims.seq // tq
    seg = _seg_block(dims, tq)
    ctx_blk = dims.n_lat // dims.ctx
    gw = Q_PER_KV * HEAD_DIM
    kc = min(ATTN_KEY_CHUNK, dims.seq)
    assert dims.seq % kc == 0
    n_keys = dims.seq + dims.ctx
    return pl.pallas_call(
        functools.partial(_attn_kernel, lat_tiles=lat_tiles, seq=dims.seq, ctx=dims.ctx, kc=kc),
        grid=(dims.batch, N_KV_HEADS, lat_tiles + 1),
        scratch_shapes=[pltpu.VMEM((tq, n_keys), F32), pltpu.VMEM((tq, n_keys), F32)],
        in_specs=[pl.BlockSpec((tq, gw), lambda b, h, r: (seg(b, r), h)),
                  pl.BlockSpec((dims.seq, HEAD_DIM), lambda b, h, r: (b, h)),
                  pl.BlockSpec((dims.ctx, HEAD_DIM), lambda b, h, r: (ctx_blk + b, h)),
                  pl.BlockSpec((dims.seq, HEAD_DIM), lambda b, h, r: (b, h)),
                  pl.BlockSpec((dims.ctx, HEAD_DIM), lambda b, h, r: (ctx_blk + b, h))],
        out_specs=pl.BlockSpec((tq, gw), lambda b, h, r: (seg(b, r), h)),
        out_shape=jax.ShapeDtypeStruct((dims.n_tok, BRANCH_WIDTH), BF16),
        compiler_params=_cparams(("arbitrary", "arbitrary", "arbitrary")), name="attention",
    )(qn, kn, kn, vn, vn)


def _conv_rows(prev8, cur, next8, w_ref, b_ref):
    rows = cur.shape[0]
    ext = jnp.concatenate([prev8, cur, next8], axis=0)
    y = b_ref[...] + w_ref[0:1, :] * ext[7:7 + rows]
    y = y + w_ref[1:2, :] * cur
    y = y + w_ref[2:3, :] * ext[9:9 + rows]
    y = y + w_ref[3:4, :] * ext[10:10 + rows]
    return y


def _conv_chunk(src_ref, r0, seg_len, rows, w_ref, b_ref):
    cur = src_ref[pl.ds(r0, rows), :]
    p0 = pl.multiple_of(jnp.maximum(r0 - SUBLANES, 0), SUBLANES)
    n0 = pl.multiple_of(jnp.minimum(r0 + rows, seg_len - SUBLANES), SUBLANES)
    prev8 = src_ref[pl.ds(p0, SUBLANES), :] * jnp.where(r0 > 0, 1.0, 0.0)
    next8 = src_ref[pl.ds(n0, SUBLANES), :] * jnp.where(r0 + rows < seg_len, 1.0, 0.0)
    return _conv_rows(prev8, cur, next8, w_ref, b_ref)


def _rglru_kernel(rxl_ref, rxc_ref, rgl_ref, rgc_ref, cw_ref, cb_ref, wg_ref, bg_ref, lam_ref,
                  o_ref, af_s, uf_s, ab_s, ub_s, hf_s, hb_s, *, seq, ctx):
    r = pl.program_id(2)
    tot = seq + ctx
    rows = ROW_TILE
    w = RNN_BLOCK_DIM

    @pl.when(r == 0)
    def _():
        sp_f = _softplus(-lam_ref[0, :, 0:w])
        sp_b = _softplus(-lam_ref[0, :, w:2 * w])

        def gates(x, base):
            z = _dot(x.astype(BF16), wg_ref[0]) + bg_ref[0]
            sg = _sigmoid(z)
            for d, (sp, a_s, u_s) in enumerate(((sp_f, af_s, uf_s), (sp_b, ab_s, ub_s))):
                rg = sg[:, (2 * d) * w:(2 * d + 1) * w]
                ig = sg[:, (2 * d + 1) * w:(2 * d + 2) * w]
                a = jnp.exp(-LRU_C * rg * sp)
                a_s[pl.ds(base, rows), :] = a
                u_s[pl.ds(base, rows), :] = jnp.sqrt(1.0 - a * a) * (ig * x)

        for c0 in range(0, ctx, rows):
            gates(_conv_chunk(rxc_ref, c0, ctx, rows, cw_ref, cb_ref), c0)

        def lat_body(c, carry):
            r0 = pl.multiple_of(c * rows, rows)
            gates(_conv_chunk(rxl_ref, r0, seq, rows, cw_ref, cb_ref), pl.multiple_of(ctx + r0, rows))
            return carry
        lax.fori_loop(0, seq // rows, lat_body, 0)

        row = lax.broadcasted_iota(jnp.int32, (SUBLANES, w), 0)
        blk = SCAN_GROUPS * SUBLANES

        def scan_block(base, carry, a_s, u_s, out_s, reverse):
            a_blk = a_s[pl.ds(base, blk), :]
            u_blk = u_s[pl.ds(base, blk), :]
            hs = [None] * SCAN_GROUPS
            for g in (range(SCAN_GROUPS - 1, -1, -1) if reverse else range(SCAN_GROUPS)):
                a = a_blk[g * SUBLANES:(g + 1) * SUBLANES]
                u = u_blk[g * SUBLANES:(g + 1) * SUBLANES]
                for k in (1, 2, 4):
                    sh = SUBLANES - k if reverse else k
                    m = (row < SUBLANES - k) if reverse else (row >= k)
                    u = u + a * jnp.where(m, pltpu.roll(u, sh, 0), 0.0)
                    a = a * jnp.where(m, pltpu.roll(a, sh, 0), 1.0)
                hs[g] = u + a * carry
                e = 0 if reverse else SUBLANES - 1
                carry = (jnp.broadcast_to(u[e:e + 1, :], (SUBLANES, w))
                         + jnp.broadcast_to(a[e:e + 1, :], (SUBLANES, w)) * carry)
            out_s[pl.ds(base, blk), :] = jnp.concatenate(hs, axis=0)
            return carry

        def both(i, carry, f0, b0):
            cf, cb = carry
            cf = scan_block(pl.multiple_of(f0 + i * blk, blk), cf, af_s, uf_s, hf_s, False)
            cb = scan_block(pl.multiple_of(b0 - (i + 1) * blk, blk), cb, ab_s, ub_s, hb_s, True)
            return cf, cb

        zero = jnp.zeros((SUBLANES, w), F32)
        carry = lax.fori_loop(0, ctx // blk, lambda i, c: both(i, c, 0, ctx), (zero, zero))
        lax.fori_loop(0, seq // blk, lambda i, c: both(i, c, ctx, tot), carry)

        for c0 in range(0, ctx, rows):
            sl = pl.ds(c0, rows)
            hf_s[sl, :] = _gelu_tanh(rgc_ref[sl, :]) * (hf_s[sl, :] + hb_s[sl, :])

        def out_body(c, carry):
            r0 = pl.multiple_of(c * rows, rows)
            sl = pl.ds(pl.multiple_of(ctx + r0, rows), rows)
            hf_s[sl, :] = _gelu_tanh(rgl_ref[pl.ds(r0, rows), :]) * (hf_s[sl, :] + hb_s[sl, :])
            return carry
        lax.fori_loop(0, seq // rows, out_body, 0)

    lat_pieces = seq // rows
    src = jnp.where(r < lat_pieces, ctx + r * rows, (r - lat_pieces) * rows)
    o_ref[...] = hf_s[pl.ds(pl.multiple_of(src, rows), rows), :].astype(o_ref.dtype)


def rglru_branch(dims, p, conv_w, conv_b, w_gates, b_gates, lam):
    rows = ROW_TILE
    w = RNN_BLOCK_DIM
    pieces = (dims.seq + dims.ctx) // rows
    seg = _seg_block(dims, rows)
    ctx_blk = dims.n_lat // dims.ctx
    rx0, rg0 = RX_OFF // w, RG_OFF // w
    tot = dims.seq + dims.ctx
    return pl.pallas_call(
        functools.partial(_rglru_kernel, seq=dims.seq, ctx=dims.ctx),
        grid=(dims.batch, RNN_BLOCKS, pieces),
        in_specs=[pl.BlockSpec((dims.seq, w), lambda b, n, r: (b, rx0 + n)),
                  pl.BlockSpec((dims.ctx, w), lambda b, n, r: (ctx_blk + b, rx0 + n)),
                  pl.BlockSpec((dims.seq, w), lambda b, n, r: (b, rg0 + n)),
                  pl.BlockSpec((dims.ctx, w), lambda b, n, r: (ctx_blk + b, rg0 + n)),
                  pl.BlockSpec((CONV_W, w), lambda b, n, r: (0, n)),
                  pl.BlockSpec((1, w), lambda b, n, r: (0, n)),
                  pl.BlockSpec((1, w, 4 * w), lambda b, n, r: (n, 0, 0)),
                  pl.BlockSpec((1, 1, 4 * w), lambda b, n, r: (n, 0, 0)),
                  pl.BlockSpec((1, 1, 2 * w), lambda b, n, r: (n, 0, 0))],
        out_specs=pl.BlockSpec((rows, w), lambda b, n, r: (seg(b, r), n)),
        out_shape=jax.ShapeDtypeStruct((dims.n_tok, RNN_WIDTH), BF16),
        scratch_shapes=[pltpu.VMEM((tot, w), F32) for _ in range(6)],
        compiler_params=_cparams(("arbitrary", "arbitrary", "arbitrary")), name="rglru",
    )(p, p, p, p, conv_w, conv_b.reshape(1, RNN_WIDTH), w_gates, b_gates, lam)


def _ssd_prep_kernel(cur_ref, prev_ref, next_ref, w_ref, b_ref, o_ref, *, n_lat_tiles, per_batch):
    i = pl.program_id(0)
    j = i % per_batch
    lat = i < n_lat_tiles
    pv = jnp.where(jnp.logical_and(lat, j != 0), 1.0, 0.0)
    nv = jnp.where(jnp.logical_and(lat, j != per_batch - 1), 1.0, 0.0)
    y = _conv_rows(prev_ref[...] * pv, cur_ref[...], next_ref[...] * nv, w_ref, b_ref)
    o_ref[...] = _silu(y)


def ssd_prep(dims, p, conv_w, conv_b):
    tm = ROW_TILE
    assert dims.ctx == tm
    m = dims.n_tok
    hb = tm // SUBLANES
    last8 = m // SUBLANES - 1
    c0 = SX_OFF // SSD_XBC
    return pl.pallas_call(
        functools.partial(_ssd_prep_kernel, n_lat_tiles=dims.n_lat // tm, per_batch=dims.seq // tm),
        grid=(m // tm,),
        in_specs=[pl.BlockSpec((tm, SSD_XBC), lambda i: (i, c0)),
                  pl.BlockSpec((SUBLANES, SSD_XBC), lambda i: (jnp.maximum(i * hb - 1, 0), c0)),
                  pl.BlockSpec((SUBLANES, SSD_XBC), lambda i: (jnp.minimum((i + 1) * hb, last8), c0)),
                  pl.BlockSpec((CONV_W, SSD_XBC), lambda i: (0, 0)),
                  pl.BlockSpec((1, SSD_XBC), lambda i: (0, 0))],
        out_specs=pl.BlockSpec((tm, SSD_XBC), lambda i: (i, 0)),
        out_shape=jax.ShapeDtypeStruct((m, SSD_XBC), F32),
        compiler_params=_cparams(("arbitrary",)), name="ssd_prep",
    )(p, p, p, conv_w, conv_b.reshape(1, SSD_XBC))


def _ssd_chunk(x_ref, b_ref, c_ref, dt_ref, dtb_ref, alog_ref, s_ref, y_ref, reverse):
    q = SSD_CHUNK
    col0 = HEADS_PER_GROUP if reverse else 0
    ri = lax.broadcasted_iota(jnp.int32, (q, q), 0)
    ci = lax.broadcasted_iota(jnp.int32, (q, q), 1)
    tri = (ri <= ci) if reverse else (ri >= ci)
    cum = jnp.where(tri, 1.0, 0.0).astype(BF16)
    ones = jnp.ones((q, q), BF16)
    left = ci < SSD_HEAD_DIM
    top = ri < SSD_HEAD_DIM

    dtc = _softplus(dt_ref[...] + dtb_ref[0])
    a = dtc * (-jnp.exp(alog_ref[0]))
    a_hi = a.astype(BF16)
    r1 = a - a_hi.astype(F32)
    a_mid = r1.astype(BF16)
    a_lo = (r1 - a_mid.astype(F32)).astype(BF16)
    acum = _dot(cum, a_hi) + _dot(cum, a_mid) + _dot(cum, a_lo)
    atot = _dot(ones, a_hi) + _dot(ones, a_mid) + _dot(ones, a_lo)
    acum_t = acum.T

    bm = b_ref[...].astype(BF16)
    cm = c_ref[...].astype(BF16)
    cb = _dot_nt(cm, bm)

    for pair in range(HEADS_PER_GROUP // 2):
        ca = col0 + 2 * pair
        lanes = slice(pair * LANES, (pair + 1) * LANES)
        col_a, col_b = acum[:, ca:ca + 1], acum[:, ca + 1:ca + 2]
        row_a, row_b = acum_t[ca:ca + 1, :], acum_t[ca + 1:ca + 2, :]
        l_a = jnp.exp(jnp.where(tri, col_a - row_a, NEG_BIG))
        l_b = jnp.exp(jnp.where(tri, col_b - row_b, NEG_BIG))
        xdt = x_ref[:, lanes] * jnp.where(left, dtc[:, ca:ca + 1], dtc[:, ca + 1:ca + 2])
        xdt_b = xdt.astype(BF16)
        y_in = jnp.where(left, _dot((cb * l_a).astype(BF16), xdt_b), _dot((cb * l_b).astype(BF16), xdt_b))
        s_old = s_ref[lanes, :]
        y_st = _dot_nt(cm, s_old.astype(BF16)) * jnp.where(left, jnp.exp(col_a), jnp.exp(col_b))
        y_ref[:, lanes] = y_in + y_st
        tot_a, tot_b = atot[:, ca:ca + 1], atot[:, ca + 1:ca + 2]
        xw = xdt * jnp.where(left, jnp.exp(tot_a - col_a), jnp.exp(tot_b - col_b))
        s_new = _dot(xw.T.astype(BF16), bm)
        s_ref[lanes, :] = s_old * jnp.where(top, jnp.exp(tot_a), jnp.exp(tot_b)) + s_new


def _ssd_scan_kernel(xf_ref, bf_ref, cf_ref, dtf_ref, xb_ref, bb_ref, cb_ref, dtb_ref_,
                     bias_ref, alog_ref, yf_ref, yb_ref, sf_s, sb_s):
    @pl.when(pl.program_id(2) == 0)
    def _():
        sf_s[...] = jnp.zeros_like(sf_s)
        sb_s[...] = jnp.zeros_like(sb_s)

    _ssd_chunk(xf_ref, bf_ref, cf_ref, dtf_ref, bias_ref, alog_ref, sf_s, yf_ref, False)
    _ssd_chunk(xb_ref, bb_ref, cb_ref, dtb_ref_, bias_ref, alog_ref, sb_s, yb_ref, True)


def ssd_scan(dims, xbc, dt, dt_bias, a_log):
    q = SSD_CHUNK
    lat_c, ctx_c = dims.seq // q, dims.ctx // q
    ctx_base = dims.n_lat // q
    steps = lat_c + ctx_c

    def cf(b, s):
        return jnp.where(s < ctx_c, ctx_base + b * ctx_c + s, b * lat_c + (s - ctx_c))

    def cbk(b, s):
        return jnp.where(s < ctx_c, ctx_base + b * ctx_c + (ctx_c - 1 - s),
                         b * lat_c + (lat_c - 1 - (s - ctx_c)))

    gw = GROUP_WIDTH
    bcol, ccol = SSD_WIDTH // SSD_STATE, (SSD_WIDTH + SSD_BC) // SSD_STATE

    def specs(cfun):
        return [pl.BlockSpec((q, gw), lambda b, g, s: (cfun(b, s), g)),
                pl.BlockSpec((q, SSD_STATE), lambda b, g, s: (cfun(b, s), bcol + g)),
                pl.BlockSpec((q, SSD_STATE), lambda b, g, s: (cfun(b, s), ccol + g)),
                pl.BlockSpec((q, LANES), lambda b, g, s: (cfun(b, s), g))]

    par_spec = pl.BlockSpec((1, 1, LANES), lambda b, g, s: (g, 0, 0))
    yshape = jax.ShapeDtypeStruct((dims.n_tok, SSD_WIDTH), F32)
    return pl.pallas_call(
        _ssd_scan_kernel, grid=(dims.batch, SSD_GROUPS, steps),
        in_specs=specs(cf) + specs(cbk) + [par_spec, par_spec],
        out_specs=[pl.BlockSpec((q, gw), lambda b, g, s: (cf(b, s), g)),
                   pl.BlockSpec((q, gw), lambda b, g, s: (cbk(b, s), g))],
        out_shape=[yshape, yshape],
        scratch_shapes=[pltpu.VMEM((gw, SSD_STATE), F32), pltpu.VMEM((gw, SSD_STATE), F32)],
        compiler_params=_cparams(("arbitrary", "arbitrary", "arbitrary")), name="ssd_scan",
    )(xbc, xbc, xbc, dt, xbc, xbc, xbc, dt, dt_bias, a_log)


def _ssd_finish_kernel(yf_ref, yb_ref, x_ref, z_ref, d_ref, g_ref, o_ref):
    y = d_ref[...] * x_ref[...] + yf_ref[...] + yb_ref[...]
    gt = y * _silu(z_ref[...])
    o_ref[...] = (gt * lax.rsqrt(jnp.mean(gt * gt, axis=-1, keepdims=True) + EPS) * g_ref[...]).astype(o_ref.dtype)


def ssd_finish(dims, yf, yb, xbc, p, d_chan, norm_g):
    tm = ROW_TILE
    gw = GROUP_WIDTH
    z0 = SZ_OFF // gw
    blk = pl.BlockSpec((tm, gw), lambda i, g: (i, g))
    vec = pl.BlockSpec((1, gw), lambda i, g: (0, g))
    return pl.pallas_call(
        _ssd_finish_kernel, grid=(dims.n_tok // tm, SSD_GROUPS),
        in_specs=[blk, blk, blk, pl.BlockSpec((tm, gw), lambda i, g: (i, z0 + g)), vec, vec],
        out_specs=blk,
        out_shape=jax.ShapeDtypeStruct((dims.n_tok, SSD_WIDTH), BF16),
        compiler_params=_cparams(("arbitrary", "arbitrary")), name="ssd_finish",
    )(yf, yb, xbc, p, d_chan, norm_g.reshape(1, SSD_WIDTH))


def _route_kernel(lg_ref, rb_ref, cw_ref, dest_ref, tab_ref, st_s, *, m, tile):
    score = _sigmoid(lg_ref[...])
    sel = score + rb_ref[...]
    v = [sel[e:e + 1, :] for e in range(N_EXPERTS)]
    sc = [score[e:e + 1, :] for e in range(N_EXPERTS)]
    best, best_g = None, None
    for g in range(N_EXPERT_GROUPS):
        vg = v[4 * g:4 * g + 4]
        gs = vg[0] + vg[1]
        for (i, j) in ((0, 2), (0, 3), (1, 2), (1, 3), (2, 3)):
            gs = jnp.maximum(gs, vg[i] + vg[j])
        if g == 0:
            best, best_g = gs, jnp.zeros_like(gs, dtype=jnp.int32)
        else:
            better = gs > best
            best_g = jnp.where(better, g, best_g)
            best = jnp.where(better, gs, best)
    wts, wsel = [], []
    for e in range(N_EXPERTS):
        g = e // EXPERTS_PER_GROUP
        rank = jnp.zeros_like(best_g)
        for k in range(4 * g, 4 * g + 4):
            if k == e:
                continue
            ahead = (v[k] >= v[e]) if k < e else (v[k] > v[e])
            rank = rank + jnp.where(ahead, 1, 0)
        chosen = jnp.logical_and(best_g == g, rank < 2)
        wts.append(jnp.where(chosen, sc[e], 0.0))
        wsel.append(jnp.where(chosen, 1.0, 0.0))
    tot = wts[0]
    for e in range(1, N_EXPERTS):
        tot = tot + wts[e]
    for j in range(EXPERTS_PER_GROUP):
        wj = wts[j]
        for g in range(1, N_EXPERT_GROUPS):
            wj = wj + wts[EXPERTS_PER_GROUP * g + j]
        cw_ref[j:j + 1, :] = wj / tot
    cw_ref[EXPERTS_PER_GROUP:SUBLANES, :] = jnp.zeros((SUBLANES - EXPERTS_PER_GROUP, m), F32)

    pos = []
    for j in range(EXPERTS_PER_GROUP):
        cj = wsel[j]
        for g in range(1, N_EXPERT_GROUPS):
            cj = cj + wsel[EXPERTS_PER_GROUP * g + j]
        pos.append(cj > 0.5)
    first = jnp.where(pos[0], 0.0, jnp.where(pos[1], 1.0, 2.0))
    second = jnp.where(pos[3], 3.0, jnp.where(pos[2], 2.0, 1.0))
    lex = 3.0 * first - 0.5 * first * (first - 1.0) + (second - first - 1.0)
    pair = jnp.where(lex == 3.0, 4.0, jnp.where(lex == 4.0, 3.0, lex))
    key = best_g.astype(F32) * float(N_PAIRS) + pair
    n_key = N_EXPERT_GROUPS * N_PAIRS
    onehot = jnp.where(lax.broadcasted_iota(jnp.int32, (n_key, m), 0).astype(F32) == key, 1.0, 0.0)
    cb = jnp.broadcast_to(jnp.sum(onehot, axis=-1, keepdims=True), (n_key, LANES))
    starts, ends_valid, ends_pad = [], [], []
    g_start = jnp.zeros((1, LANES), F32)
    for g in range(N_EXPERT_GROUPS):
        cur = g_start
        for p in range(N_PAIRS):
            starts.append(cur)
            cur = cur + cb[N_PAIRS * g + p:N_PAIRS * g + p + 1, :]
        ends_valid.append(cur)
        g_start = g_start + jnp.floor((cur - g_start + (tile - 1)) * (1.0 / tile)) * tile
        ends_pad.append(g_start)
    for k in range(n_key):
        st_s[k:k + 1, :] = starts[k]
    st = st_s[...]
    ri = lax.broadcasted_iota(jnp.int32, (LANES, LANES), 0)
    ci = lax.broadcasted_iota(jnp.int32, (LANES, LANES), 1)
    before = jnp.where(ri < ci, 1.0, 0.0).astype(BF16)
    carry = jnp.zeros((n_key, 1), F32)
    for b in range(m // LANES):
        sl = slice(b * LANES, (b + 1) * LANES)
        ob = onehot[:, sl]
        slot = _dot(ob.astype(BF16), before) + carry + st
        dest_ref[:, sl] = jnp.sum(ob * slot, axis=0, keepdims=True).astype(jnp.int32)
        carry = carry + jnp.sum(ob, axis=-1, keepdims=True)
    t0 = lax.broadcasted_iota(jnp.int32, (1, LANES), 1).astype(F32) * tile
    t1 = t0 + tile
    tile_group = jnp.zeros((1, LANES), F32)
    for g in range(N_EXPERT_GROUPS):
        tile_group = tile_group + jnp.where(t0 >= ends_pad[g], 1.0, 0.0)
    group_end = jnp.zeros((1, LANES), F32)
    for g in range(N_EXPERT_GROUPS):
        group_end = group_end + jnp.where(tile_group == float(g), ends_valid[g], 0.0)
    rows = [tile_group, g_start * (1.0 / tile), jnp.clip(group_end - t0, 0.0, float(tile))]
    for j in range(EXPERTS_PER_GROUP):
        used = jnp.zeros((1, LANES), F32)
        for k in range(n_key):
            if j in PAIRS[k % N_PAIRS]:
                cnt = cb[k:k + 1, :]
                hit = jnp.logical_and(starts[k] < t1, starts[k] + cnt > t0)
                used = jnp.maximum(used, jnp.where(hit, 1.0, 0.0))
        rows.append(used)
    rows.append(jnp.zeros((1, LANES), F32))
    for r, row in enumerate(rows):
        tab_ref[r:r + 1, :] = row.astype(jnp.int32)


def route(logits_t, router_b):
    m = logits_t.shape[1]
    return pl.pallas_call(
        functools.partial(_route_kernel, m=m, tile=MOE_TILE), grid=(1,),
        in_specs=[pl.BlockSpec((N_EXPERTS, m), lambda i: (0, 0)),
                  pl.BlockSpec((N_EXPERTS, 1), lambda i: (0, 0))],
        out_specs=[pl.BlockSpec((SUBLANES, m), lambda i: (0, 0)),
                   pl.BlockSpec((1, m), lambda i: (0, 0)),
                   pl.BlockSpec((SUBLANES, LANES), lambda i: (0, 0))],
        out_shape=[jax.ShapeDtypeStruct((SUBLANES, m), F32),
                   jax.ShapeDtypeStruct((1, m), jnp.int32),
                   jax.ShapeDtypeStruct((SUBLANES, LANES), jnp.int32)],
        scratch_shapes=[pltpu.VMEM((N_EXPERT_GROUPS * N_PAIRS, LANES), F32)],
        compiler_params=_cparams(("arbitrary",)), name="route",
    )(logits_t, router_b.reshape(N_EXPERTS, 1))


def _invert_kernel(dest_ref, src_ref):
    def clear(s, c):
        src_ref[s] = 0
        return c
    lax.fori_loop(0, src_ref.shape[0], clear, 0, unroll=8)

    def place(t, c):
        src_ref[dest_ref[t]] = t
        return c
    lax.fori_loop(0, dest_ref.shape[0], place, 0, unroll=8)


def invert_slots(dest, n_slots):
    smem = pl.BlockSpec(memory_space=pltpu.SMEM)
    return pl.pallas_call(
        _invert_kernel, in_specs=[smem], out_specs=smem,
        out_shape=jax.ShapeDtypeStruct((n_slots,), jnp.int32),
        name="moe_invert",
    )(dest)


def _moe_group_kernel(tg_ref, nu_ref, nv_ref, used_ref, src_ref, h_ref, c_ref, wg_ref, wu_ref, wd_ref, o_ref,
                      xbuf, cbuf, xb_s, sem):
    tm = o_ref.shape[0]
    i = pl.program_id(0)
    j = pl.program_id(1)
    f = pl.program_id(2)
    n_used = nu_ref[0]
    active = i < n_used
    first = jnp.logical_and(j == 0, f == 0)
    slot = i % 2

    def gather(tile, buf_slot, wait):
        def body(r, c):
            s = src_ref[tile * tm + r]
            cps = (pltpu.make_async_copy(h_ref.at[pl.ds(s, 1)], xbuf.at[buf_slot, pl.ds(r, 1)], sem.at[0, buf_slot]),
                   pltpu.make_async_copy(c_ref.at[pl.ds(s, 1)], cbuf.at[buf_slot, pl.ds(r, 1)], sem.at[1, buf_slot]))
            for cp in cps:
                if wait:
                    cp.wait()
                else:
                    cp.start()
            return c
        lax.fori_loop(0, tm, body, 0, unroll=8)

    @pl.when(jnp.logical_and(first, i == 0))
    def _():
        gather(0, 0, False)

    @pl.when(jnp.logical_and(first, i + 1 < n_used))
    def _():
        gather(i + 1, (i + 1) % 2, False)

    @pl.when(jnp.logical_and(first, active))
    def _():
        gather(i, slot, True)
        xb_s[...] = xbuf[slot].astype(BF16)

    @pl.when(first)
    def _():
        o_ref[...] = jnp.zeros_like(o_ref)

    @pl.when(jnp.logical_and(active, used_ref[j * LANES + i] > 0))
    def _():
        xb = xb_s[...]
        g = _dot(xb, wg_ref[0, 0])
        u = _dot(xb, wu_ref[0, 0])
        c = cbuf[slot]
        lane = lax.broadcasted_iota(jnp.int32, c.shape, 1)
        rowi = lax.broadcasted_iota(jnp.int32, c.shape, 0)
        keep = jnp.logical_and(lane == j, rowi < nv_ref[i])
        cw = jnp.sum(jnp.where(keep, c, 0.0), axis=-1, keepdims=True)
        hid = _silu(g) * u * cw
        o_ref[...] += _dot(hid.astype(BF16), wd_ref[0, 0])


def moe_grouped(h, cw_rows, tile_group, n_used, n_valid, used, src, w_gate_all, w_up_all, w_down_all, l, n_tiles):
    tm = MOE_TILE
    tf = EXPERT_FF // 2
    last_f = EXPERT_FF // tf - 1

    def tile(i, nu):
        return jnp.minimum(i, nu[0] - 1)

    def pick(i, j, f, tg, nu, us):
        ti = tile(i, nu)
        idle = i >= nu[0]
        j = jnp.where(idle, EXPERTS_PER_GROUP - 1, j)
        f = jnp.where(idle, last_f, f)
        prev = jnp.int32(-1)
        nxt = jnp.int32(EXPERTS_PER_GROUP - 1)
        for k in range(EXPERTS_PER_GROUP):
            prev = jnp.where(jnp.logical_and(us[k * LANES + ti] > 0, k <= j), k, prev)
        for k in reversed(range(EXPERTS_PER_GROUP)):
            nxt = jnp.where(jnp.logical_and(us[k * LANES + ti] > 0, k > j), k, nxt)
        jj = jnp.where(prev >= 0, prev, nxt)
        ff = jnp.where(prev == j, f, jnp.where(prev >= 0, last_f, 0))
        return tg[ti] * EXPERTS_PER_GROUP + jj, ff

    def expert(i, j, f, tg, nu, us):
        return pick(i, j, f, tg, nu, us)[0]

    def fblk(i, j, f, tg, nu, us):
        return pick(i, j, f, tg, nu, us)[1]

    any_spec = pl.BlockSpec(memory_space=pl.ANY)
    return pl.pallas_call(
        _moe_group_kernel,
        grid_spec=pltpu.PrefetchScalarGridSpec(
            num_scalar_prefetch=5, grid=(n_tiles, EXPERTS_PER_GROUP, EXPERT_FF // tf),
            in_specs=[any_spec, any_spec,
                      pl.BlockSpec((1, 1, D_MODEL, tf), lambda i, j, f, tg, nu, nv, us, sr:
                                   (l, expert(i, j, f, tg, nu, us), 0, fblk(i, j, f, tg, nu, us))),
                      pl.BlockSpec((1, 1, D_MODEL, tf), lambda i, j, f, tg, nu, nv, us, sr:
                                   (l, expert(i, j, f, tg, nu, us), 0, fblk(i, j, f, tg, nu, us))),
                      pl.BlockSpec((1, 1, tf, D_MODEL), lambda i, j, f, tg, nu, nv, us, sr:
                                   (l, expert(i, j, f, tg, nu, us), fblk(i, j, f, tg, nu, us), 0))],
            out_specs=pl.BlockSpec((tm, D_MODEL), lambda i, j, f, tg, nu, nv, us, sr: (i, 0)),
            scratch_shapes=[pltpu.VMEM((2, tm, D_MODEL), F32), pltpu.VMEM((2, tm, LANES), F32),
                            pltpu.VMEM((tm, D_MODEL), BF16), pltpu.SemaphoreType.DMA((2, 2))]),
        out_shape=jax.ShapeDtypeStruct((n_tiles * tm, D_MODEL), F32),
        compiler_params=_cparams(("arbitrary", "arbitrary", "arbitrary")), name="moe_grouped",
    )(tile_group, n_used, n_valid, used, src, h, cw_rows, w_gate_all, w_up_all, w_down_all)


def _combine_kernel(dest_ref, x_ref, g_ref, ys_ref, o_ref, buf, sem):
    tm = x_ref.shape[0]
    i = pl.program_id(0)
    n = pl.num_programs(0)

    def gather(tile, slot, wait):
        def body(r, c):
            d = dest_ref[tile * tm + r]
            cp = pltpu.make_async_copy(ys_ref.at[pl.ds(d, 1)], buf.at[slot, pl.ds(r, 1)], sem.at[slot])
            if wait:
                cp.wait()
            else:
                cp.start()
            return c
        lax.fori_loop(0, tm, body, 0, unroll=8)

    @pl.when(i == 0)
    def _():
        gather(0, 0, False)

    @pl.when(i + 1 < n)
    def _():
        gather(i + 1, (i + 1) % 2, False)

    slot = i % 2
    gather(i, slot, True)
    o_ref[...] = x_ref[...] + g_ref[0] * buf[slot]


def combine(dims, dest, x, ys, modtab, k, m_rows):
    tm = ROW_TILE
    mrow = _mod_row(dims, tm, k)
    return pl.pallas_call(
        _combine_kernel,
        grid_spec=pltpu.PrefetchScalarGridSpec(
            num_scalar_prefetch=1, grid=(m_rows // tm,),
            in_specs=[pl.BlockSpec((tm, D_MODEL), lambda i, d: (i, 0)),
                      pl.BlockSpec((1, 1, D_MODEL), lambda i, d: (mrow(i), 0, 0)),
                      pl.BlockSpec(memory_space=pl.ANY)],
            out_specs=pl.BlockSpec((tm, D_MODEL), lambda i, d: (i, 0)),
            scratch_shapes=[pltpu.VMEM((2, tm, D_MODEL), F32), pltpu.SemaphoreType.DMA((2,))]),
        out_shape=jax.ShapeDtypeStruct((m_rows, D_MODEL), F32),
        compiler_params=_cparams(("arbitrary",)), name="moe_combine",
    )(dest, x, modtab, ys)


def rope_tables(dims):
    s = dims.seq
    rows = s // GRID_W
    row = jnp.repeat(jnp.arange(rows, dtype=F32), GRID_W)
    col = jnp.tile(jnp.arange(GRID_W, dtype=F32), rows)
    inv = ROPE_THETA ** (-jnp.arange(N_ROPE_FREQ, dtype=F32) / N_ROPE_FREQ)
    ang = jnp.stack([row[:, None] * inv, col[:, None] * inv], axis=1)
    cos, sin = jnp.cos(ang), jnp.sin(ang)
    zero = jnp.zeros_like(sin)
    c = jnp.stack([cos, cos], axis=2).reshape(s, HEAD_DIM)
    s0 = jnp.stack([-sin, zero], axis=2).reshape(s, HEAD_DIM)
    s1 = jnp.stack([zero, sin], axis=2).reshape(s, HEAD_DIM)
    ident = jnp.ones((ROW_TILE, HEAD_DIM), F32)
    zpad = jnp.zeros((ROW_TILE, HEAD_DIM), F32)
    return (jnp.concatenate([c, ident], 0), jnp.concatenate([s0, zpad], 0), jnp.concatenate([s1, zpad], 0))


def _dt_cols():
    cols = []
    for g in range(SSD_GROUPS):
        for d in range(2):
            cols += [SDT_OFF + d * SSD_HEADS + g * HEADS_PER_GROUP + hh for hh in range(HEADS_PER_GROUP)]
    return cols


def _group_dt_param(v):
    out = []
    for g in range(SSD_GROUPS):
        hs = slice(g * HEADS_PER_GROUP, (g + 1) * HEADS_PER_GROUP)
        row = jnp.concatenate([v[0, hs], v[1, hs], jnp.zeros((LANES - 2 * HEADS_PER_GROUP,), v.dtype)])
        out.append(row.reshape(1, LANES))
    return jnp.stack(out, 0)


def _transpose_cast_kernel(w_ref, o_ref):
    o_ref[0] = w_ref[0].T.astype(o_ref.dtype)


def transpose_cast(w_t, first_row, n_rows, block, out_dtype):
    nl, _, k = w_t.shape
    assert first_row % (2 * SUBLANES) == 0 and n_rows % block == 0
    return pl.pallas_call(
        _transpose_cast_kernel, grid=(nl, n_rows // block),
        in_specs=[pl.BlockSpec((pl.Element(1), pl.Element(block), pl.Element(k)),
                               lambda l, j: (l, pl.multiple_of(first_row + j * block, 2 * SUBLANES), 0))],
        out_specs=pl.BlockSpec((1, k, block), lambda l, j: (l, 0, j)),
        out_shape=jax.ShapeDtypeStruct((nl, k, n_rows), out_dtype),
        compiler_params=_cparams(("arbitrary", "arbitrary")), name="transpose_cast",
    )(w_t)


def split_w_in(w_in):
    w_t = jnp.swapaxes(w_in, 1, 2)
    n_gate = w_in.shape[2] - MIX_COLS
    return (transpose_cast(w_t, 0, SDT_OFF, 256, BF16),
            transpose_cast(w_t, MIX_COLS, n_gate, 256, BF16),
            transpose_cast(w_t, SDT_OFF, LANES, LANES, F32))


def _dt_weight(dt_block):
    nl, d = dt_block.shape[:2]
    pad = jnp.zeros((nl, d, LANES - 2 * HEADS_PER_GROUP), dt_block.dtype)
    dtw = dt_block[:, :, :MIX_COLS - SDT_OFF].reshape(nl, d, 2, SSD_GROUPS, HEADS_PER_GROUP)
    parts = []
    for g in range(SSD_GROUPS):
        parts += [dtw[:, :, :, g, :].reshape(nl, d, 2 * HEADS_PER_GROUP), pad]
    return jnp.concatenate(parts, axis=2).astype(BF16)


def run_model(dims, x, c, ctx, c_ctx, w_mod_a, w_mod_b, b_mod, g_mix, g_ffn, w_in, w_up, w_o, q_norm, k_norm,
              rnn_conv_w, rnn_conv_b, rnn_lambda, rnn_w_r, rnn_b_r, rnn_w_i, rnn_b_i,
              ssd_conv_w, ssd_conv_b, ssd_dt_bias, ssd_a_log, ssd_d, ssd_norm,
              router_w, router_b, moe_w_gate, moe_w_up, moe_w_down, g_final):
    depth = w_in.shape[0]
    bsz = dims.batch
    xs = jnp.concatenate([x.reshape(dims.n_lat, D_MODEL), ctx.reshape(bsz * dims.ctx, D_MODEL)], axis=0)

    cond = jnp.zeros((SUBLANES, D_MODEL), F32).at[0].set(c_ctx).at[1:1 + bsz].set(c)
    mod_all = adaln_all(cond, w_mod_a, w_mod_b, b_mod).reshape(depth, SUBLANES * N_MOD, 1, D_MODEL)
    rope_c, rope_s0, rope_s1 = rope_tables(dims)
    router_wt = router_w.T.astype(BF16)

    w_mix_all, w_gate_all, dt_block = split_w_in(w_in)
    w_dt_all = _dt_weight(dt_block)
    w_up_all = w_up.astype(BF16)
    w_o_all = w_o.astype(BF16)
    moe_g_all = moe_w_gate.astype(BF16)
    moe_u_all = moe_w_up.astype(BF16)
    moe_d_all = moe_w_down.astype(BF16)

    for l in range(depth):
        last = l == depth - 1
        m_rows = dims.n_lat if last else dims.n_tok
        modtab = mod_all[l]

        h = norm_mod(dims, xs, g_mix[l], modtab, 0, dims.n_tok)
        p = matmul(h, w_mix_all, l, 768, F32)
        dt = matmul(h, w_dt_all, l, SSD_GROUPS * LANES, F32)

        qn, kn, vn = qkv_prep(dims, p, rope_c, rope_s0, rope_s1, q_norm[l], k_norm[l])
        ya = attention(dims, qn, kn, vn)

        w_gates = jnp.concatenate([rnn_w_r[l, 0], rnn_w_i[l, 0], rnn_w_r[l, 1], rnn_w_i[l, 1]], axis=-1).astype(BF16)
        b_gates = jnp.concatenate(
            [v.reshape(RNN_BLOCKS, 1, RNN_BLOCK_DIM) for v in (rnn_b_r[l, 0], rnn_b_i[l, 0], rnn_b_r[l, 1], rnn_b_i[l, 1])],
            axis=-1)
        lam = jnp.concatenate([rnn_lambda[l, d].reshape(RNN_BLOCKS, 1, RNN_BLOCK_DIM) for d in range(2)], axis=-1)
        yr = rglru_branch(dims, p, rnn_conv_w[l], rnn_conv_b[l], w_gates, b_gates, lam)

        xbc = ssd_prep(dims, p, ssd_conv_w[l], ssd_conv_b[l])
        yf, yb = ssd_scan(dims, xbc, dt, _group_dt_param(ssd_dt_bias[l]), _group_dt_param(ssd_a_log[l]))
        d_chan = jnp.repeat(ssd_d[l], SSD_HEAD_DIM).reshape(1, SSD_WIDTH)
        ys = ssd_finish(dims, yf, yb, xbc, p, d_chan, ssd_norm[l])

        merged = merge_branches(h, ya, yr, ys, w_gate_all, w_up_all, l, m_rows)
        xs = matmul_residual(dims, merged, w_o_all, l, xs, modtab, 2, m_rows)

        h2, logits_t = norm_mod(dims, xs, g_ffn[l], modtab, 3, m_rows, router_wt)
        cw_t, dest, tab = route(logits_t, router_b)
        dest = dest.reshape(m_rows)
        cw_rows = jnp.pad(cw_t.T, ((0, 0), (0, LANES - SUBLANES)))
        n_tiles = m_rows // MOE_TILE + N_EXPERT_GROUPS
        src = invert_slots(dest, n_tiles * MOE_TILE)
        used = tab[3:3 + EXPERTS_PER_GROUP].reshape(EXPERTS_PER_GROUP * LANES)
        ys2 = moe_grouped(h2, cw_rows, tab[0], tab[1, :1], tab[2], used, src, moe_g_all, moe_u_all, moe_d_all, l, n_tiles)
        xs = combine(dims, dest, xs, ys2, modtab, 5, m_rows)

    out = final_norm(xs, g_final, dims.n_lat)
    return out.reshape(bsz, dims.seq, D_MODEL)


def kernel(x, c, ctx, c_ctx, w_mod_a, w_mod_b, b_mod, g_mix, g_ffn, w_in, w_up, w_o, q_norm, k_norm, rnn_conv_w, rnn_conv_b, rnn_lambda, rnn_w_r, rnn_b_r, rnn_w_i, rnn_b_i, ssd_conv_w, ssd_conv_b, ssd_dt_bias, ssd_a_log, ssd_d, ssd_norm, router_w, router_b, moe_w_gate, moe_w_up, moe_w_down, g_final):
    dims = Dims(batch=x.shape[0], seq=x.shape[1], ctx=ctx.shape[1])
    return run_model(dims, x, c, ctx, c_ctx, w_mod_a, w_mod_b, b_mod, g_mix, g_ffn, w_in, w_up, w_o, q_norm, k_norm,
                     rnn_conv_w, rnn_conv_b, rnn_lambda, rnn_w_r, rnn_b_r, rnn_w_i, rnn_b_i,
                     ssd_conv_w, ssd_conv_b, ssd_dt_bias, ssd_a_log, ssd_d, ssd_norm,
                     router_w, router_b, moe_w_gate, moe_w_up, moe_w_down, g_final)
```

```python
import functools
from typing import NamedTuple

import jax
import jax.numpy as jnp
from jax import lax
from jax.experimental import pallas as pl
from jax.experimental.pallas import tpu as pltpu

F32 = jnp.float32
BF16 = jnp.bfloat16

D_MODEL = 4096
DEPTH = 4
GRID_W = 64
N_BRANCH = 3
BRANCH_WIDTH = 1024
HEAD_DIM = 128
N_Q_HEADS = 8
N_KV_HEADS = 2
Q_PER_KV = 4
KV_WIDTH = 256
N_ROPE_FREQ = 32
ROPE_THETA = 10000.0
RNN_WIDTH = 1024
RNN_BLOCKS = 8
RNN_BLOCK_DIM = 128
LRU_C = 8.0
CONV_W = 4
SSD_WIDTH = 1024
SSD_HEAD_DIM = 64
SSD_HEADS = 16
SSD_GROUPS = 2
SSD_STATE = 128
SSD_CHUNK = 128
SSD_BC = 256
SSD_XBC = 1536
HEADS_PER_GROUP = SSD_HEADS // SSD_GROUPS
GROUP_WIDTH = SSD_WIDTH // SSD_GROUPS
Q_OFF = 0
K_OFF = 1024
V_OFF = 1280
RX_OFF = 1536
RG_OFF = 2560
SZ_OFF = 3584
SX_OFF = 4608
SDT_OFF = 6144
MIX_COLS = 6176
MOD_RANK = 256
N_MOD = 6
N_EXPERTS = 16
N_EXPERT_GROUPS = 4
EXPERTS_PER_GROUP = 4
EXPERT_FF = 512
EPS = 1e-6

LANES = 128
SUBLANES = 8
VMEM_LIMIT_BYTES = 56 * 1024 * 1024
ROW_TILE = 256
SCAN_GROUPS = 4
MOE_TILE = 512
PAIRS = ((0, 1), (0, 2), (0, 3), (1, 3), (1, 2), (2, 3))
N_PAIRS = len(PAIRS)
ATTN_KEY_CHUNK = 1024
MM_TILE_M = 512
NEG_BIG = -1e30


class Dims(NamedTuple):
    batch: int
    seq: int
    ctx: int

    @property
    def n_lat(self):
        return self.batch * self.seq

    @property
    def n_tok(self):
        return self.batch * (self.seq + self.ctx)


def _cparams(sem):
    return pltpu.CompilerParams(dimension_semantics=sem, vmem_limit_bytes=VMEM_LIMIT_BYTES)


def _sigmoid(x):
    return 1.0 / (1.0 + jnp.exp(-x))


def _silu(x):
    return x * _sigmoid(x)


def _softplus(x):
    return jnp.maximum(x, 0.0) + jnp.log(1.0 + jnp.exp(-jnp.abs(x)))


def _gelu_tanh(x):
    return 0.5 * x * (1.0 + jnp.tanh(0.7978845608028654 * (x + 0.044715 * (x * x * x))))


def _dot(a, b):
    return jnp.dot(a, b, preferred_element_type=F32)


def _dot_nt(a, b):
    return lax.dot_general(a, b, (((1,), (1,)), ((), ())), preferred_element_type=F32)


def _mod_row(dims, tile, k):
    n_lat_tiles = dims.n_lat // tile
    per_batch = dims.seq // tile

    def f(i):
        return jnp.where(i < n_lat_tiles, 1 + i // per_batch, 0) * N_MOD + k
    return f


def _seg_block(dims, rows):
    lat_pb = dims.seq // rows
    ctx_pb = dims.ctx // rows
    ctx_base = dims.n_lat // rows

    def f(b, r):
        return jnp.where(r < lat_pb, b * lat_pb + r, ctx_base + b * ctx_pb + (r - lat_pb))
    return f


def _adaln_a_kernel(c_ref, w_ref, o_ref):
    c = c_ref[...]
    o_ref[0] = _dot(_silu(c).astype(BF16), w_ref[0].astype(BF16))


def _adaln_b_kernel(a_ref, w_ref, b_ref, o_ref):
    o_ref[0] = _dot(a_ref[0].astype(BF16), w_ref[0].astype(BF16)) + b_ref[0]


def adaln_all(cond, w_a, w_b, b):
    nl = w_a.shape[0]
    a = pl.pallas_call(
        _adaln_a_kernel,
        grid=(nl,),
        in_specs=[pl.BlockSpec((SUBLANES, D_MODEL), lambda l: (0, 0)),
                  pl.BlockSpec((1, D_MODEL, MOD_RANK), lambda l: (l, 0, 0))],
        out_specs=pl.BlockSpec((1, SUBLANES, MOD_RANK), lambda l: (l, 0, 0)),
        out_shape=jax.ShapeDtypeStruct((nl, SUBLANES, MOD_RANK), F32),
        compiler_params=_cparams(("arbitrary",)),
        name="adaln_a",
    )(cond, w_a)
    tn = 4096
    ncol = N_MOD * D_MODEL
    return pl.pallas_call(
        _adaln_b_kernel,
        grid=(nl, ncol // tn),
        in_specs=[pl.BlockSpec((1, SUBLANES, MOD_RANK), lambda l, j: (l, 0, 0)),
                  pl.BlockSpec((1, MOD_RANK, tn), lambda l, j: (l, 0, j)),
                  pl.BlockSpec((1, 1, tn), lambda l, j: (l, 0, j))],
        out_specs=pl.BlockSpec((1, SUBLANES, tn), lambda l, j: (l, 0, j)),
        out_shape=jax.ShapeDtypeStruct((nl, SUBLANES, ncol), F32),
        compiler_params=_cparams(("arbitrary", "arbitrary")),
        name="adaln_b",
    )(a, w_b, b.reshape(nl, 1, ncol))


def _norm_mod_kernel(x_ref, g_ref, sh_ref, sc_ref, o_ref):
    x = x_ref[...]
    y = x * lax.rsqrt(jnp.mean(x * x, axis=-1, keepdims=True) + EPS) * g_ref[...]
    o_ref[...] = (y * (1.0 + sc_ref[0]) + sh_ref[0]).astype(o_ref.dtype)


def _norm_mod_router_kernel(x_ref, g_ref, sh_ref, sc_ref, rw_ref, o_ref, lg_ref):
    x = x_ref[...]
    y = x * lax.rsqrt(jnp.mean(x * x, axis=-1, keepdims=True) + EPS) * g_ref[...]
    h = y * (1.0 + sc_ref[0]) + sh_ref[0]
    o_ref[...] = h
    lg_ref[...] = _dot_nt(rw_ref[...], h.astype(BF16))


def norm_mod(dims, x, gain, modtab, k, m_rows, router_wt=None):
    tm = ROW_TILE
    mrow = _mod_row(dims, tm, k)
    mrow1 = _mod_row(dims, tm, k + 1)
    in_specs = [pl.BlockSpec((tm, D_MODEL), lambda i: (i, 0)),
                pl.BlockSpec((1, D_MODEL), lambda i: (0, 0)),
                pl.BlockSpec((1, 1, D_MODEL), lambda i: (mrow(i), 0, 0)),
                pl.BlockSpec((1, 1, D_MODEL), lambda i: (mrow1(i), 0, 0))]
    h_spec = pl.BlockSpec((tm, D_MODEL), lambda i: (i, 0))
    if router_wt is None:
        return pl.pallas_call(
            _norm_mod_kernel, grid=(m_rows // tm,), in_specs=in_specs, out_specs=h_spec,
            out_shape=jax.ShapeDtypeStruct((m_rows, D_MODEL), BF16),
            compiler_params=_cparams(("arbitrary",)), name="norm_mod",
        )(x, gain.reshape(1, D_MODEL), modtab, modtab)
    return pl.pallas_call(
        _norm_mod_router_kernel, grid=(m_rows // tm,),
        in_specs=in_specs + [pl.BlockSpec((N_EXPERTS, D_MODEL), lambda i: (0, 0))],
        out_specs=[h_spec, pl.BlockSpec((N_EXPERTS, tm), lambda i: (0, i))],
        out_shape=[jax.ShapeDtypeStruct((m_rows, D_MODEL), F32),
                   jax.ShapeDtypeStruct((N_EXPERTS, m_rows), F32)],
        compiler_params=_cparams(("arbitrary",)), name="norm_mod_router",
    )(x, gain.reshape(1, D_MODEL), modtab, modtab, router_wt)


def _final_norm_kernel(x_ref, g_ref, o_ref):
    x = x_ref[...]
    o_ref[...] = x * lax.rsqrt(jnp.mean(x * x, axis=-1, keepdims=True) + EPS) * g_ref[...]


def final_norm(x, gain, m_rows):
    tm = ROW_TILE
    return pl.pallas_call(
        _final_norm_kernel, grid=(m_rows // tm,),
        in_specs=[pl.BlockSpec((tm, D_MODEL), lambda i: (i, 0)),
                  pl.BlockSpec((1, D_MODEL), lambda i: (0, 0))],
        out_specs=pl.BlockSpec((tm, D_MODEL), lambda i: (i, 0)),
        out_shape=jax.ShapeDtypeStruct((m_rows, D_MODEL), F32),
        compiler_params=_cparams(("arbitrary",)), name="final_norm",
    )(x, gain.reshape(1, D_MODEL))


def _mm_kernel(a_ref, w_ref, o_ref):
    o_ref[...] = _dot(a_ref[...], w_ref[0]).astype(o_ref.dtype)


def matmul(a, w_all, l, tn, out_dtype):
    m, k = a.shape
    n = w_all.shape[2]
    tm = MM_TILE_M
    return pl.pallas_call(
        _mm_kernel, grid=(n // tn, m // tm),
        in_specs=[pl.BlockSpec((tm, k), lambda j, i: (i, 0)),
                  pl.BlockSpec((1, k, tn), lambda j, i: (l, 0, j))],
        out_specs=pl.BlockSpec((tm, tn), lambda j, i: (i, j)),
        out_shape=jax.ShapeDtypeStruct((m, n), out_dtype),
        compiler_params=_cparams(("arbitrary", "arbitrary")), name="mm_in",
    )(a, w_all)


def _mm_res_kernel(a_ref, w_ref, x_ref, g_ref, o_ref):
    o_ref[...] = x_ref[...] + g_ref[0] * _dot(a_ref[...], w_ref[0])


def matmul_residual(dims, a, w_all, l, x, modtab, k, m_rows):
    kk = a.shape[1]
    n = w_all.shape[2]
    tm, tn = MM_TILE_M, 1024
    mrow = _mod_row(dims, tm, k)
    return pl.pallas_call(
        _mm_res_kernel, grid=(n // tn, m_rows // tm),
        in_specs=[pl.BlockSpec((tm, kk), lambda j, i: (i, 0)),
                  pl.BlockSpec((1, kk, tn), lambda j, i: (l, 0, j)),
                  pl.BlockSpec((tm, tn), lambda j, i: (i, j)),
                  pl.BlockSpec((1, 1, tn), lambda j, i: (mrow(i), 0, j))],
        out_specs=pl.BlockSpec((tm, tn), lambda j, i: (i, j)),
        out_shape=jax.ShapeDtypeStruct((m_rows, n), F32),
        compiler_params=_cparams(("arbitrary", "arbitrary")), name="mm_out_res",
    )(a, w_all, x, modtab)


def _merge_kernel(h_ref, ya_ref, yr_ref, ys_ref, wg0_ref, wg1_ref, wg2_ref,
                  wu0_ref, wu1_ref, wu2_ref, o_ref):
    h = h_ref[...]
    acc = _sigmoid(_dot(h, wg0_ref[0])) * _dot(ya_ref[...], wu0_ref[0, 0])
    acc += _sigmoid(_dot(h, wg1_ref[0])) * _dot(yr_ref[...], wu1_ref[0, 0])
    acc += _sigmoid(_dot(h, wg2_ref[0])) * _dot(ys_ref[...], wu2_ref[0, 0])
    o_ref[...] = acc.astype(o_ref.dtype)


def merge_branches(h, ya, yr, ys, w_gate_all, w_up_all, l, m_rows):
    tm, tn = MM_TILE_M, 512
    nj = D_MODEL // tn
    y_spec = pl.BlockSpec((tm, BRANCH_WIDTH), lambda j, i: (i, 0))

    def wg_spec(n):
        return pl.BlockSpec((1, D_MODEL, tn), lambda j, i: (l, 0, n * nj + j))

    def wu_spec(n):
        return pl.BlockSpec((1, 1, BRANCH_WIDTH, tn), lambda j, i: (l, n, 0, j))

    return pl.pallas_call(
        _merge_kernel, grid=(nj, m_rows // tm),
        in_specs=[pl.BlockSpec((tm, D_MODEL), lambda j, i: (i, 0)), y_spec, y_spec, y_spec,
                  wg_spec(0), wg_spec(1), wg_spec(2), wu_spec(0), wu_spec(1), wu_spec(2)],
        out_specs=pl.BlockSpec((tm, tn), lambda j, i: (i, j)),
        out_shape=jax.ShapeDtypeStruct((m_rows, D_MODEL), BF16),
        compiler_params=_cparams(("arbitrary", "arbitrary")), name="merge",
    )(h, ya, yr, ys, w_gate_all, w_gate_all, w_gate_all, w_up_all, w_up_all, w_up_all)


def _qkv_prep_kernel(q_ref, k_ref, v_ref, c_ref, s0_ref, s1_ref, qg_ref, kg_ref,
                     qo_ref, ko_ref, vo_ref):
    c, s0, s1 = c_ref[...], s0_ref[...], s1_ref[...]

    def head(xh, gain, scale):
        y = xh * lax.rsqrt(jnp.mean(xh * xh, axis=-1, keepdims=True) + EPS) * gain
        r = y * c + pltpu.roll(y, 96, 1) * s0 + pltpu.roll(y, 32, 1) * s1
        return r * scale

    for hh in range(N_Q_HEADS):
        sl = slice(hh * HEAD_DIM, (hh + 1) * HEAD_DIM)
        qo_ref[:, sl] = head(q_ref[:, sl], qg_ref[...], HEAD_DIM ** -0.5).astype(qo_ref.dtype)
    for hh in range(N_KV_HEADS):
        sl = slice(hh * HEAD_DIM, (hh + 1) * HEAD_DIM)
        ko_ref[:, sl] = head(k_ref[:, sl], kg_ref[...], 1.0).astype(ko_ref.dtype)
    vo_ref[...] = v_ref[...].astype(vo_ref.dtype)


def qkv_prep(dims, p, rope_c, rope_s0, rope_s1, q_gain, k_gain):
    tm = ROW_TILE
    n_lat_tiles = dims.n_lat // tm
    per_batch = dims.seq // tm

    def tab(i):
        return (jnp.where(i < n_lat_tiles, i % per_batch, per_batch), 0)

    m = dims.n_tok
    tab_spec = pl.BlockSpec((tm, HEAD_DIM), tab)
    g_spec = pl.BlockSpec((1, HEAD_DIM), lambda i: (0, 0))
    return pl.pallas_call(
        _qkv_prep_kernel, grid=(m // tm,),
        in_specs=[pl.BlockSpec((tm, BRANCH_WIDTH), lambda i: (i, Q_OFF // BRANCH_WIDTH)),
                  pl.BlockSpec((tm, KV_WIDTH), lambda i: (i, K_OFF // KV_WIDTH)),
                  pl.BlockSpec((tm, KV_WIDTH), lambda i: (i, V_OFF // KV_WIDTH)),
                  tab_spec, tab_spec, tab_spec, g_spec, g_spec],
        out_specs=[pl.BlockSpec((tm, BRANCH_WIDTH), lambda i: (i, 0)),
                   pl.BlockSpec((tm, KV_WIDTH), lambda i: (i, 0)),
                   pl.BlockSpec((tm, KV_WIDTH), lambda i: (i, 0))],
        out_shape=[jax.ShapeDtypeStruct((m, BRANCH_WIDTH), BF16),
                   jax.ShapeDtypeStruct((m, KV_WIDTH), BF16),
                   jax.ShapeDtypeStruct((m, KV_WIDTH), BF16)],
        compiler_params=_cparams(("arbitrary",)), name="qkv_prep",
    )(p, p, p, rope_c, rope_s0, rope_s1, q_gain.reshape(1, HEAD_DIM), k_gain.reshape(1, HEAD_DIM))


def _attn_kernel(q_ref, kl_ref, kc_ref, vl_ref, vc_ref, o_ref, sa_ref, sb_ref, *, lat_tiles, seq, ctx, kc):
    qi = pl.program_id(2)

    chunks = [(kl_ref, vl_ref, c * kc, kc, c * kc) for c in range(seq // kc)] + [(kc_ref, vc_ref, 0, ctx, seq)]

    def scores(g, chunk, s_ref):
        k_ref, _, r0, n, col = chunk
        s = _dot_nt(q_ref[:, g * HEAD_DIM:(g + 1) * HEAD_DIM], k_ref[r0:r0 + n, :])
        s_ref[:, col:col + n] = s
        return jnp.max(s, axis=-1, keepdims=True)

    def weighted(chunk, s_ref, mx):
        _, v_ref, r0, n, col = chunk
        p = jnp.exp(s_ref[:, col:col + n] - mx)
        return jnp.sum(p, axis=-1, keepdims=True), _dot(p.astype(BF16), v_ref[r0:r0 + n, :])

    @pl.when(qi < lat_tiles)
    def _():
        bufs = (sa_ref, sb_ref)
        mx = None
        for ch in chunks:
            cm = scores(0, ch, bufs[0])
            mx = cm if mx is None else jnp.maximum(mx, cm)
        for g in range(Q_PER_KV):
            cur, nxt = bufs[g % 2], bufs[(g + 1) % 2]
            den, acc, mx_next = None, None, None
            for ch in chunks:
                if g + 1 < Q_PER_KV:
                    cm = scores(g + 1, ch, nxt)
                    mx_next = cm if mx_next is None else jnp.maximum(mx_next, cm)
                ds, pv = weighted(ch, cur, mx)
                den = ds if den is None else den + ds
                acc = pv if acc is None else acc + pv
            o_ref[:, g * HEAD_DIM:(g + 1) * HEAD_DIM] = (acc / den).astype(o_ref.dtype)
            mx = mx_next

    @pl.when(qi >= lat_tiles)
    def _():
        for g in range(Q_PER_KV):
            sl = slice(g * HEAD_DIM, (g + 1) * HEAD_DIM)
            s_c = _dot_nt(q_ref[:, sl], kc_ref[...])
            p_c = jnp.exp(s_c - jnp.max(s_c, axis=-1, keepdims=True))
            den = jnp.sum(p_c, axis=-1, keepdims=True)
            o_ref[:, sl] = (_dot(p_c.astype(BF16), vc_ref[...]) / den).astype(o_ref.dtype)


def attention(dims, qn, kn, vn):
    tq = ROW_TILE
    assert dims.ctx == tq
    lat_tiles = dims.seq // tq
    seg = _seg_block(dims, tq)
    ctx_blk = dims.n_lat // dims.ctx
    gw = Q_PER_KV * HEAD_DIM
    kc = min(ATTN_KEY_CHUNK, dims.seq)
    assert dims.seq % kc == 0
    n_keys = dims.seq + dims.ctx
    return pl.pallas_call(
        functools.partial(_attn_kernel, lat_tiles=lat_tiles, seq=dims.seq, ctx=dims.ctx, kc=kc),
        grid=(dims.batch, N_KV_HEADS, lat_tiles + 1),
        scratch_shapes=[pltpu.VMEM((tq, n_keys), F32), pltpu.VMEM((tq, n_keys), F32)],
        in_specs=[pl.BlockSpec((tq, gw), lambda b, h, r: (seg(b, r), h)),
                  pl.BlockSpec((dims.seq, HEAD_DIM), lambda b, h, r: (b, h)),
                  pl.BlockSpec((dims.ctx, HEAD_DIM), lambda b, h, r: (ctx_blk + b, h)),
                  pl.BlockSpec((dims.seq, HEAD_DIM), lambda b, h, r: (b, h)),
                  pl.BlockSpec((dims.ctx, HEAD_DIM), lambda b, h, r: (ctx_blk + b, h))],
        out_specs=pl.BlockSpec((tq, gw), lambda b, h, r: (seg(b, r), h)),
        out_shape=jax.ShapeDtypeStruct((dims.n_tok, BRANCH_WIDTH), BF16),
        compiler_params=_cparams(("arbitrary", "arbitrary", "arbitrary")), name="attention",
    )(qn, kn, kn, vn, vn)


def _conv_rows(prev8, cur, next8, w_ref, b_ref):
    rows = cur.shape[0]
    ext = jnp.concatenate([prev8, cur, next8], axis=0)
    y = b_ref[...] + w_ref[0:1, :] * ext[7:7 + rows]
    y = y + w_ref[1:2, :] * cur
    y = y + w_ref[2:3, :] * ext[9:9 + rows]
    y = y + w_ref[3:4, :] * ext[10:10 + rows]
    return y


def _conv_chunk(src_ref, r0, seg_len, rows, w_ref, b_ref):
    cur = src_ref[pl.ds(r0, rows), :]
    p0 = pl.multiple_of(jnp.maximum(r0 - SUBLANES, 0), SUBLANES)
    n0 = pl.multiple_of(jnp.minimum(r0 + rows, seg_len - SUBLANES), SUBLANES)
    prev8 = src_ref[pl.ds(p0, SUBLANES), :] * jnp.where(r0 > 0, 1.0, 0.0)
    next8 = src_ref[pl.ds(n0, SUBLANES), :] * jnp.where(r0 + rows < seg_len, 1.0, 0.0)
    return _conv_rows(prev8, cur, next8, w_ref, b_ref)


def _rglru_kernel(rxl_ref, rxc_ref, rgl_ref, rgc_ref, cw_ref, cb_ref, wg_ref, bg_ref, lam_ref,
                  o_ref, af_s, uf_s, ab_s, ub_s, hf_s, hb_s, *, seq, ctx):
    r = pl.program_id(2)
    tot = seq + ctx
    rows = ROW_TILE
    w = RNN_BLOCK_DIM

    @pl.when(r == 0)
    def _():
        sp_f = _softplus(-lam_ref[0, :, 0:w])
        sp_b = _softplus(-lam_ref[0, :, w:2 * w])

        def gates(x, base):
            z = _dot(x.astype(BF16), wg_ref[0]) + bg_ref[0]
            sg = _sigmoid(z)
            for d, (sp, a_s, u_s) in enumerate(((sp_f, af_s, uf_s), (sp_b, ab_s, ub_s))):
                rg = sg[:, (2 * d) * w:(2 * d + 1) * w]
                ig = sg[:, (2 * d + 1) * w:(2 * d + 2) * w]
                a = jnp.exp(-LRU_C * rg * sp)
                a_s[pl.ds(base, rows), :] = a
                u_s[pl.ds(base, rows), :] = jnp.sqrt(1.0 - a * a) * (ig * x)

        for c0 in range(0, ctx, rows):
            gates(_conv_chunk(rxc_ref, c0, ctx, rows, cw_ref, cb_ref), c0)

        def lat_body(c, carry):
            r0 = pl.multiple_of(c * rows, rows)
            gates(_conv_chunk(rxl_ref, r0, seq, rows, cw_ref, cb_ref), pl.multiple_of(ctx + r0, rows))
            return carry
        lax.fori_loop(0, seq // rows, lat_body, 0)

        row = lax.broadcasted_iota(jnp.int32, (SUBLANES, w), 0)
        blk = SCAN_GROUPS * SUBLANES

        def scan_block(base, carry, a_s, u_s, out_s, reverse):
            a_blk = a_s[pl.ds(base, blk), :]
            u_blk = u_s[pl.ds(base, blk), :]
            hs = [None] * SCAN_GROUPS
            for g in (range(SCAN_GROUPS - 1, -1, -1) if reverse else range(SCAN_GROUPS)):
                a = a_blk[g * SUBLANES:(g + 1) * SUBLANES]
                u = u_blk[g * SUBLANES:(g + 1) * SUBLANES]
                for k in (1, 2, 4):
                    sh = SUBLANES - k if reverse else k
                    m = (row < SUBLANES - k) if reverse else (row >= k)
                    u = u + a * jnp.where(m, pltpu.roll(u, sh, 0), 0.0)
                    a = a * jnp.where(m, pltpu.roll(a, sh, 0), 1.0)
                hs[g] = u + a * carry
                e = 0 if reverse else SUBLANES - 1
                carry = (jnp.broadcast_to(u[e:e + 1, :], (SUBLANES, w))
                         + jnp.broadcast_to(a[e:e + 1, :], (SUBLANES, w)) * carry)
            out_s[pl.ds(base, blk), :] = jnp.concatenate(hs, axis=0)
            return carry

        def both(i, carry, f0, b0):
            cf, cb = carry
            cf = scan_block(pl.multiple_of(f0 + i * blk, blk), cf, af_s, uf_s, hf_s, False)
            cb = scan_block(pl.multiple_of(b0 - (i + 1) * blk, blk), cb, ab_s, ub_s, hb_s, True)
            return cf, cb

        zero = jnp.zeros((SUBLANES, w), F32)
        carry = lax.fori_loop(0, ctx // blk, lambda i, c: both(i, c, 0, ctx), (zero, zero))
        lax.fori_loop(0, seq // blk, lambda i, c: both(i, c, ctx, tot), carry)

        for c0 in range(0, ctx, rows):
            sl = pl.ds(c0, rows)
            hf_s[sl, :] = _gelu_tanh(rgc_ref[sl, :]) * (hf_s[sl, :] + hb_s[sl, :])

        def out_body(c, carry):
            r0 = pl.multiple_of(c * rows, rows)
            sl = pl.ds(pl.multiple_of(ctx + r0, rows), rows)
            hf_s[sl, :] = _gelu_tanh(rgl_ref[pl.ds(r0, rows), :]) * (hf_s[sl, :] + hb_s[sl, :])
            return carry
        lax.fori_loop(0, seq // rows, out_body, 0)

    lat_pieces = seq // rows
    src = jnp.where(r < lat_pieces, ctx + r * rows, (r - lat_pieces) * rows)
    o_ref[...] = hf_s[pl.ds(pl.multiple_of(src, rows), rows), :].astype(o_ref.dtype)


def rglru_branch(dims, p, conv_w, conv_b, w_gates, b_gates, lam):
    rows = ROW_TILE
    w = RNN_BLOCK_DIM
    pieces = (dims.seq + dims.ctx) // rows
    seg = _seg_block(dims, rows)
    ctx_blk = dims.n_lat // dims.ctx
    rx0, rg0 = RX_OFF // w, RG_OFF // w
    tot = dims.seq + dims.ctx
    return pl.pallas_call(
        functools.partial(_rglru_kernel, seq=dims.seq, ctx=dims.ctx),
        grid=(dims.batch, RNN_BLOCKS, pieces),
        in_specs=[pl.BlockSpec((dims.seq, w), lambda b, n, r: (b, rx0 + n)),
                  pl.BlockSpec((dims.ctx, w), lambda b, n, r: (ctx_blk + b, rx0 + n)),
                  pl.BlockSpec((dims.seq, w), lambda b, n, r: (b, rg0 + n)),
                  pl.BlockSpec((dims.ctx, w), lambda b, n, r: (ctx_blk + b, rg0 + n)),
                  pl.BlockSpec((CONV_W, w), lambda b, n, r: (0, n)),
                  pl.BlockSpec((1, w), lambda b, n, r: (0, n)),
                  pl.BlockSpec((1, w, 4 * w), lambda b, n, r: (n, 0, 0)),
                  pl.BlockSpec((1, 1, 4 * w), lambda b, n, r: (n, 0, 0)),
                  pl.BlockSpec((1, 1, 2 * w), lambda b, n, r: (n, 0, 0))],
        out_specs=pl.BlockSpec((rows, w), lambda b, n, r: (seg(b, r), n)),
        out_shape=jax.ShapeDtypeStruct((dims.n_tok, RNN_WIDTH), BF16),
        scratch_shapes=[pltpu.VMEM((tot, w), F32) for _ in range(6)],
        compiler_params=_cparams(("arbitrary", "arbitrary", "arbitrary")), name="rglru",
    )(p, p, p, p, conv_w, conv_b.reshape(1, RNN_WIDTH), w_gates, b_gates, lam)


def _ssd_prep_kernel(cur_ref, prev_ref, next_ref, w_ref, b_ref, o_ref, *, n_lat_tiles, per_batch):
    i = pl.program_id(0)
    j = i % per_batch
    lat = i < n_lat_tiles
    pv = jnp.where(jnp.logical_and(lat, j != 0), 1.0, 0.0)
    nv = jnp.where(jnp.logical_and(lat, j != per_batch - 1), 1.0, 0.0)
    y = _conv_rows(prev_ref[...] * pv, cur_ref[...], next_ref[...] * nv, w_ref, b_ref)
    o_ref[...] = _silu(y)


def ssd_prep(dims, p, conv_w, conv_b):
    tm = ROW_TILE
    assert dims.ctx == tm
    m = dims.n_tok
    hb = tm // SUBLANES
    last8 = m // SUBLANES - 1
    c0 = SX_OFF // SSD_XBC
    return pl.pallas_call(
        functools.partial(_ssd_prep_kernel, n_lat_tiles=dims.n_lat // tm, per_batch=dims.seq // tm),
        grid=(m // tm,),
        in_specs=[pl.BlockSpec((tm, SSD_XBC), lambda i: (i, c0)),
                  pl.BlockSpec((SUBLANES, SSD_XBC), lambda i: (jnp.maximum(i * hb - 1, 0), c0)),
                  pl.BlockSpec((SUBLANES, SSD_XBC), lambda i: (jnp.minimum((i + 1) * hb, last8), c0)),
                  pl.BlockSpec((CONV_W, SSD_XBC), lambda i: (0, 0)),
                  pl.BlockSpec((1, SSD_XBC), lambda i: (0, 0))],
        out_specs=pl.BlockSpec((tm, SSD_XBC), lambda i: (i, 0)),
        out_shape=jax.ShapeDtypeStruct((m, SSD_XBC), F32),
        compiler_params=_cparams(("arbitrary",)), name="ssd_prep",
    )(p, p, p, conv_w, conv_b.reshape(1, SSD_XBC))


def _ssd_chunk(x_ref, b_ref, c_ref, dt_ref, dtb_ref, alog_ref, s_ref, y_ref, reverse):
    q = SSD_CHUNK
    col0 = HEADS_PER_GROUP if reverse else 0
    ri = lax.broadcasted_iota(jnp.int32, (q, q), 0)
    ci = lax.broadcasted_iota(jnp.int32, (q, q), 1)
    tri = (ri <= ci) if reverse else (ri >= ci)
    cum = jnp.where(tri, 1.0, 0.0).astype(BF16)
    ones = jnp.ones((q, q), BF16)
    left = ci < SSD_HEAD_DIM
    top = ri < SSD_HEAD_DIM

    dtc = _softplus(dt_ref[...] + dtb_ref[0])
    a = dtc * (-jnp.exp(alog_ref[0]))
    a_hi = a.astype(BF16)
    r1 = a - a_hi.astype(F32)
    a_mid = r1.astype(BF16)
    a_lo = (r1 - a_mid.astype(F32)).astype(BF16)
    acum = _dot(cum, a_hi) + _dot(cum, a_mid) + _dot(cum, a_lo)
    atot = _dot(ones, a_hi) + _dot(ones, a_mid) + _dot(ones, a_lo)
    acum_t = acum.T

    bm = b_ref[...].astype(BF16)
    cm = c_ref[...].astype(BF16)
    cb = _dot_nt(cm, bm)

    for pair in range(HEADS_PER_GROUP // 2):
        ca = col0 + 2 * pair
        lanes = slice(pair * LANES, (pair + 1) * LANES)
        col_a, col_b = acum[:, ca:ca + 1], acum[:, ca + 1:ca + 2]
        row_a, row_b = acum_t[ca:ca + 1, :], acum_t[ca + 1:ca + 2, :]
        l_a = jnp.exp(jnp.where(tri, col_a - row_a, NEG_BIG))
        l_b = jnp.exp(jnp.where(tri, col_b - row_b, NEG_BIG))
        xdt = x_ref[:, lanes] * jnp.where(left, dtc[:, ca:ca + 1], dtc[:, ca + 1:ca + 2])
        xdt_b = xdt.astype(BF16)
        y_in = jnp.where(left, _dot((cb * l_a).astype(BF16), xdt_b), _dot((cb * l_b).astype(BF16), xdt_b))
        s_old = s_ref[lanes, :]
        y_st = _dot_nt(cm, s_old.astype(BF16)) * jnp.where(left, jnp.exp(col_a), jnp.exp(col_b))
        y_ref[:, lanes] = y_in + y_st
        tot_a, tot_b = atot[:, ca:ca + 1], atot[:, ca + 1:ca + 2]
        xw = xdt * jnp.where(left, jnp.exp(tot_a - col_a), jnp.exp(tot_b - col_b))
        s_new = _dot(xw.T.astype(BF16), bm)
        s_ref[lanes, :] = s_old * jnp.where(top, jnp.exp(tot_a), jnp.exp(tot_b)) + s_new


def _ssd_scan_kernel(xf_ref, bf_ref, cf_ref, dtf_ref, xb_ref, bb_ref, cb_ref, dtb_ref_,
                     bias_ref, alog_ref, yf_ref, yb_ref, sf_s, sb_s):
    @pl.when(pl.program_id(2) == 0)
    def _():
        sf_s[...] = jnp.zeros_like(sf_s)
        sb_s[...] = jnp.zeros_like(sb_s)

    _ssd_chunk(xf_ref, bf_ref, cf_ref, dtf_ref, bias_ref, alog_ref, sf_s, yf_ref, False)
    _ssd_chunk(xb_ref, bb_ref, cb_ref, dtb_ref_, bias_ref, alog_ref, sb_s, yb_ref, True)


def ssd_scan(dims, xbc, dt, dt_bias, a_log):
    q = SSD_CHUNK
    lat_c, ctx_c = dims.seq // q, dims.ctx // q
    ctx_base = dims.n_lat // q
    steps = lat_c + ctx_c

    def cf(b, s):
        return jnp.where(s < ctx_c, ctx_base + b * ctx_c + s, b * lat_c + (s - ctx_c))

    def cbk(b, s):
        return jnp.where(s < ctx_c, ctx_base + b * ctx_c + (ctx_c - 1 - s),
                         b * lat_c + (lat_c - 1 - (s - ctx_c)))

    gw = GROUP_WIDTH
    bcol, ccol = SSD_WIDTH // SSD_STATE, (SSD_WIDTH + SSD_BC) // SSD_STATE

    def specs(cfun):
        return [pl.BlockSpec((q, gw), lambda b, g, s: (cfun(b, s), g)),
                pl.BlockSpec((q, SSD_STATE), lambda b, g, s: (cfun(b, s), bcol + g)),
                pl.BlockSpec((q, SSD_STATE), lambda b, g, s: (cfun(b, s), ccol + g)),
                pl.BlockSpec((q, LANES), lambda b, g, s: (cfun(b, s), g))]

    par_spec = pl.BlockSpec((1, 1, LANES), lambda b, g, s: (g, 0, 0))
    yshape = jax.ShapeDtypeStruct((dims.n_tok, SSD_WIDTH), F32)
    return pl.pallas_call(
        _ssd_scan_kernel, grid=(dims.batch, SSD_GROUPS, steps),
        in_specs=specs(cf) + specs(cbk) + [par_spec, par_spec],
        out_specs=[pl.BlockSpec((q, gw), lambda b, g, s: (cf(b, s), g)),
                   pl.BlockSpec((q, gw), lambda b, g, s: (cbk(b, s), g))],
        out_shape=[yshape, yshape],
        scratch_shapes=[pltpu.VMEM((gw, SSD_STATE), F32), pltpu.VMEM((gw, SSD_STATE), F32)],
        compiler_params=_cparams(("arbitrary", "arbitrary", "arbitrary")), name="ssd_scan",
    )(xbc, xbc, xbc, dt, xbc, xbc, xbc, dt, dt_bias, a_log)


def _ssd_finish_kernel(yf_ref, yb_ref, x_ref, z_ref, d_ref, g_ref, o_ref):
    y = d_ref[...] * x_ref[...] + yf_ref[...] + yb_ref[...]
    gt = y * _silu(z_ref[...])
    o_ref[...] = (gt * lax.rsqrt(jnp.mean(gt * gt, axis=-1, keepdims=True) + EPS) * g_ref[...]).astype(o_ref.dtype)


def ssd_finish(dims, yf, yb, xbc, p, d_chan, norm_g):
    tm = ROW_TILE
    gw = GROUP_WIDTH
    z0 = SZ_OFF // gw
    blk = pl.BlockSpec((tm, gw), lambda i, g: (i, g))
    vec = pl.BlockSpec((1, gw), lambda i, g: (0, g))
    return pl.pallas_call(
        _ssd_finish_kernel, grid=(dims.n_tok // tm, SSD_GROUPS),
        in_specs=[blk, blk, blk, pl.BlockSpec((tm, gw), lambda i, g: (i, z0 + g)), vec, vec],
        out_specs=blk,
        out_shape=jax.ShapeDtypeStruct((dims.n_tok, SSD_WIDTH), BF16),
        compiler_params=_cparams(("arbitrary", "arbitrary")), name="ssd_finish",
    )(yf, yb, xbc, p, d_chan, norm_g.reshape(1, SSD_WIDTH))


def _route_kernel(lg_ref, rb_ref, cw_ref, dest_ref, tab_ref, st_s, *, m, tile):
    score = _sigmoid(lg_ref[...])
    sel = score + rb_ref[...]
    v = [sel[e:e + 1, :] for e in range(N_EXPERTS)]
    sc = [score[e:e + 1, :] for e in range(N_EXPERTS)]
    best, best_g = None, None
    for g in range(N_EXPERT_GROUPS):
        vg = v[4 * g:4 * g + 4]
        gs = vg[0] + vg[1]
        for (i, j) in ((0, 2), (0, 3), (1, 2), (1, 3), (2, 3)):
            gs = jnp.maximum(gs, vg[i] + vg[j])
        if g == 0:
            best, best_g = gs, jnp.zeros_like(gs, dtype=jnp.int32)
        else:
            better = gs > best
            best_g = jnp.where(better, g, best_g)
            best = jnp.where(better, gs, best)
    wts, wsel = [], []
    for e in range(N_EXPERTS):
        g = e // EXPERTS_PER_GROUP
        rank = jnp.zeros_like(best_g)
        for k in range(4 * g, 4 * g + 4):
            if k == e:
                continue
            ahead = (v[k] >= v[e]) if k < e else (v[k] > v[e])
            rank = rank + jnp.where(ahead, 1, 0)
        chosen = jnp.logical_and(best_g == g, rank < 2)
        wts.append(jnp.where(chosen, sc[e], 0.0))
        wsel.append(jnp.where(chosen, 1.0, 0.0))
    tot = wts[0]
    for e in range(1, N_EXPERTS):
        tot = tot + wts[e]
    for j in range(EXPERTS_PER_GROUP):
        wj = wts[j]
        for g in range(1, N_EXPERT_GROUPS):
            wj = wj + wts[EXPERTS_PER_GROUP * g + j]
        cw_ref[j:j + 1, :] = wj / tot
    cw_ref[EXPERTS_PER_GROUP:SUBLANES, :] = jnp.zeros((SUBLANES - EXPERTS_PER_GROUP, m), F32)

    pos = []
    for j in range(EXPERTS_PER_GROUP):
        cj = wsel[j]
        for g in range(1, N_EXPERT_GROUPS):
            cj = cj + wsel[EXPERTS_PER_GROUP * g + j]
        pos.append(cj > 0.5)
    first = jnp.where(pos[0], 0.0, jnp.where(pos[1], 1.0, 2.0))
    second = jnp.where(pos[3], 3.0, jnp.where(pos[2], 2.0, 1.0))
    lex = 3.0 * first - 0.5 * first * (first - 1.0) + (second - first - 1.0)
    pair = jnp.where(lex == 3.0, 4.0, jnp.where(lex == 4.0, 3.0, lex))
    key = best_g.astype(F32) * float(N_PAIRS) + pair
    n_key = N_EXPERT_GROUPS * N_PAIRS
    onehot = jnp.where(lax.broadcasted_iota(jnp.int32, (n_key, m), 0).astype(F32) == key, 1.0, 0.0)
    cb = jnp.broadcast_to(jnp.sum(onehot, axis=-1, keepdims=True), (n_key, LANES))
    starts, ends_valid, ends_pad = [], [], []
    g_start = jnp.zeros((1, LANES), F32)
    for g in range(N_EXPERT_GROUPS):
        cur = g_start
        for p in range(N_PAIRS):
            starts.append(cur)
            cur = cur + cb[N_PAIRS * g + p:N_PAIRS * g + p + 1, :]
        ends_valid.append(cur)
        g_start = g_start + jnp.floor((cur - g_start + (tile - 1)) * (1.0 / tile)) * tile
        ends_pad.append(g_start)
    for k in range(n_key):
        st_s[k:k + 1, :] = starts[k]
    st = st_s[...]
    ri = lax.broadcasted_iota(jnp.int32, (LANES, LANES), 0)
    ci = lax.broadcasted_iota(jnp.int32, (LANES, LANES), 1)
    before = jnp.where(ri < ci, 1.0, 0.0).astype(BF16)
    carry = jnp.zeros((n_key, 1), F32)
    for b in range(m // LANES):
        sl = slice(b * LANES, (b + 1) * LANES)
        ob = onehot[:, sl]
        slot = _dot(ob.astype(BF16), before) + carry + st
        dest_ref[:, sl] = jnp.sum(ob * slot, axis=0, keepdims=True).astype(jnp.int32)
        carry = carry + jnp.sum(ob, axis=-1, keepdims=True)
    t0 = lax.broadcasted_iota(jnp.int32, (1, LANES), 1).astype(F32) * tile
    t1 = t0 + tile
    tile_group = jnp.zeros((1, LANES), F32)
    for g in range(N_EXPERT_GROUPS):
        tile_group = tile_group + jnp.where(t0 >= ends_pad[g], 1.0, 0.0)
    group_end = jnp.zeros((1, LANES), F32)
    for g in range(N_EXPERT_GROUPS):
        group_end = group_end + jnp.where(tile_group == float(g), ends_valid[g], 0.0)
    rows = [tile_group, g_start * (1.0 / tile), jnp.clip(group_end - t0, 0.0, float(tile))]
    for j in range(EXPERTS_PER_GROUP):
        used = jnp.zeros((1, LANES), F32)
        for k in range(n_key):
            if j in PAIRS[k % N_PAIRS]:
                cnt = cb[k:k + 1, :]
                hit = jnp.logical_and(starts[k] < t1, starts[k] + cnt > t0)
                used = jnp.maximum(used, jnp.where(hit, 1.0, 0.0))
        rows.append(used)
    rows.append(jnp.zeros((1, LANES), F32))
    for r, row in enumerate(rows):
        tab_ref[r:r + 1, :] = row.astype(jnp.int32)


def route(logits_t, router_b):
    m = logits_t.shape[1]
    return pl.pallas_call(
        functools.partial(_route_kernel, m=m, tile=MOE_TILE), grid=(1,),
        in_specs=[pl.BlockSpec((N_EXPERTS, m), lambda i: (0, 0)),
                  pl.BlockSpec((N_EXPERTS, 1), lambda i: (0, 0))],
        out_specs=[pl.BlockSpec((SUBLANES, m), lambda i: (0, 0)),
                   pl.BlockSpec((1, m), lambda i: (0, 0)),
                   pl.BlockSpec((SUBLANES, LANES), lambda i: (0, 0))],
        out_shape=[jax.ShapeDtypeStruct((SUBLANES, m), F32),
                   jax.ShapeDtypeStruct((1, m), jnp.int32),
                   jax.ShapeDtypeStruct((SUBLANES, LANES), jnp.int32)],
        scratch_shapes=[pltpu.VMEM((N_EXPERT_GROUPS * N_PAIRS, LANES), F32)],
        compiler_params=_cparams(("arbitrary",)), name="route",
    )(logits_t, router_b.reshape(N_EXPERTS, 1))


def _invert_kernel(dest_ref, src_ref):
    def clear(s, c):
        src_ref[s] = 0
        return c
    lax.fori_loop(0, src_ref.shape[0], clear, 0, unroll=8)

    def place(t, c):
        src_ref[dest_ref[t]] = t
        return c
    lax.fori_loop(0, dest_ref.shape[0], place, 0, unroll=8)


def invert_slots(dest, n_slots):
    smem = pl.BlockSpec(memory_space=pltpu.SMEM)
    return pl.pallas_call(
        _invert_kernel, in_specs=[smem], out_specs=smem,
        out_shape=jax.ShapeDtypeStruct((n_slots,), jnp.int32),
        name="moe_invert",
    )(dest)


def _moe_group_kernel(tg_ref, nu_ref, nv_ref, used_ref, src_ref, h_ref, c_ref, wg_ref, wu_ref, wd_ref, o_ref,
                      xbuf, cbuf, xb_s, sem):
    tm = o_ref.shape[0]
    i = pl.program_id(0)
    j = pl.program_id(1)
    f = pl.program_id(2)
    n_used = nu_ref[0]
    active = i < n_used
    first = jnp.logical_and(j == 0, f == 0)
    slot = i % 2

    def gather(tile, buf_slot, wait):
        def body(r, c):
            s = src_ref[tile * tm + r]
            cps = (pltpu.make_async_copy(h_ref.at[pl.ds(s, 1)], xbuf.at[buf_slot, pl.ds(r, 1)], sem.at[0, buf_slot]),
                   pltpu.make_async_copy(c_ref.at[pl.ds(s, 1)], cbuf.at[buf_slot, pl.ds(r, 1)], sem.at[1, buf_slot]))
            for cp in cps:
                if wait:
                    cp.wait()
                else:
                    cp.start()
            return c
        lax.fori_loop(0, tm, body, 0, unroll=8)

    @pl.when(jnp.logical_and(first, i == 0))
    def _():
        gather(0, 0, False)

    @pl.when(jnp.logical_and(first, i + 1 < n_used))
    def _():
        gather(i + 1, (i + 1) % 2, False)

    @pl.when(jnp.logical_and(first, active))
    def _():
        gather(i, slot, True)
        xb_s[...] = xbuf[slot].astype(BF16)

    @pl.when(first)
    def _():
        o_ref[...] = jnp.zeros_like(o_ref)

    @pl.when(jnp.logical_and(active, used_ref[j * LANES + i] > 0))
    def _():
        xb = xb_s[...]
        g = _dot(xb, wg_ref[0, 0])
        u = _dot(xb, wu_ref[0, 0])
        c = cbuf[slot]
        lane = lax.broadcasted_iota(jnp.int32, c.shape, 1)
        rowi = lax.broadcasted_iota(jnp.int32, c.shape, 0)
        keep = jnp.logical_and(lane == j, rowi < nv_ref[i])
        cw = jnp.sum(jnp.where(keep, c, 0.0), axis=-1, keepdims=True)
        hid = _silu(g) * u * cw
        o_ref[...] += _dot(hid.astype(BF16), wd_ref[0, 0])


def moe_grouped(h, cw_rows, tile_group, n_used, n_valid, used, src, w_gate_all, w_up_all, w_down_all, l, n_tiles):
    tm = MOE_TILE
    tf = EXPERT_FF // 2
    last_f = EXPERT_FF // tf - 1

    def tile(i, nu):
        return jnp.minimum(i, nu[0] - 1)

    def pick(i, j, f, tg, nu, us):
        ti = tile(i, nu)
        idle = i >= nu[0]
        j = jnp.where(idle, EXPERTS_PER_GROUP - 1, j)
        f = jnp.where(idle, last_f, f)
        prev = jnp.int32(-1)
        nxt = jnp.int32(EXPERTS_PER_GROUP - 1)
        for k in range(EXPERTS_PER_GROUP):
            prev = jnp.where(jnp.logical_and(us[k * LANES + ti] > 0, k <= j), k, prev)
        for k in reversed(range(EXPERTS_PER_GROUP)):
            nxt = jnp.where(jnp.logical_and(us[k * LANES + ti] > 0, k > j), k, nxt)
        jj = jnp.where(prev >= 0, prev, nxt)
        ff = jnp.where(prev == j, f, jnp.where(prev >= 0, last_f, 0))
        return tg[ti] * EXPERTS_PER_GROUP + jj, ff

    def expert(i, j, f, tg, nu, us):
        return pick(i, j, f, tg, nu, us)[0]

    def fblk(i, j, f, tg, nu, us):
        return pick(i, j, f, tg, nu, us)[1]

    any_spec = pl.BlockSpec(memory_space=pl.ANY)
    return pl.pallas_call(
        _moe_group_kernel,
        grid_spec=pltpu.PrefetchScalarGridSpec(
            num_scalar_prefetch=5, grid=(n_tiles, EXPERTS_PER_GROUP, EXPERT_FF // tf),
            in_specs=[any_spec, any_spec,
                      pl.BlockSpec((1, 1, D_MODEL, tf), lambda i, j, f, tg, nu, nv, us, sr:
                                   (l, expert(i, j, f, tg, nu, us), 0, fblk(i, j, f, tg, nu, us))),
                      pl.BlockSpec((1, 1, D_MODEL, tf), lambda i, j, f, tg, nu, nv, us, sr:
                                   (l, expert(i, j, f, tg, nu, us), 0, fblk(i, j, f, tg, nu, us))),
                      pl.BlockSpec((1, 1, tf, D_MODEL), lambda i, j, f, tg, nu, nv, us, sr:
                                   (l, expert(i, j, f, tg, nu, us), fblk(i, j, f, tg, nu, us), 0))],
            out_specs=pl.BlockSpec((tm, D_MODEL), lambda i, j, f, tg, nu, nv, us, sr: (i, 0)),
            scratch_shapes=[pltpu.VMEM((2, tm, D_MODEL), F32), pltpu.VMEM((2, tm, LANES), F32),
                            pltpu.VMEM((tm, D_MODEL), BF16), pltpu.SemaphoreType.DMA((2, 2))]),
        out_shape=jax.ShapeDtypeStruct((n_tiles * tm, D_MODEL), F32),
        compiler_params=_cparams(("arbitrary", "arbitrary", "arbitrary")), name="moe_grouped",
    )(tile_group, n_used, n_valid, used, src, h, cw_rows, w_gate_all, w_up_all, w_down_all)


def _combine_kernel(dest_ref, x_ref, g_ref, ys_ref, o_ref, buf, sem):
    tm = x_ref.shape[0]
    i = pl.program_id(0)
    n = pl.num_programs(0)

    def gather(tile, slot, wait):
        def body(r, c):
            d = dest_ref[tile * tm + r]
            cp = pltpu.make_async_copy(ys_ref.at[pl.ds(d, 1)], buf.at[slot, pl.ds(r, 1)], sem.at[slot])
            if wait:
                cp.wait()
            else:
                cp.start()
            return c
        lax.fori_loop(0, tm, body, 0, unroll=8)

    @pl.when(i == 0)
    def _():
        gather(0, 0, False)

    @pl.when(i + 1 < n)
    def _():
        gather(i + 1, (i + 1) % 2, False)

    slot = i % 2
    gather(i, slot, True)
    o_ref[...] = x_ref[...] + g_ref[0] * buf[slot]


def combine(dims, dest, x, ys, modtab, k, m_rows):
    tm = ROW_TILE
    mrow = _mod_row(dims, tm, k)
    return pl.pallas_call(
        _combine_kernel,
        grid_spec=pltpu.PrefetchScalarGridSpec(
            num_scalar_prefetch=1, grid=(m_rows // tm,),
            in_specs=[pl.BlockSpec((tm, D_MODEL), lambda i, d: (i, 0)),
                      pl.BlockSpec((1, 1, D_MODEL), lambda i, d: (mrow(i), 0, 0)),
                      pl.BlockSpec(memory_space=pl.ANY)],
            out_specs=pl.BlockSpec((tm, D_MODEL), lambda i, d: (i, 0)),
            scratch_shapes=[pltpu.VMEM((2, tm, D_MODEL), F32), pltpu.SemaphoreType.DMA((2,))]),
        out_shape=jax.ShapeDtypeStruct((m_rows, D_MODEL), F32),
        compiler_params=_cparams(("arbitrary",)), name="moe_combine",
    )(dest, x, modtab, ys)


def rope_tables(dims):
    s = dims.seq
    rows = s // GRID_W
    row = jnp.repeat(jnp.arange(rows, dtype=F32), GRID_W)
    col = jnp.tile(jnp.arange(GRID_W, dtype=F32), rows)
    inv = ROPE_THETA ** (-jnp.arange(N_ROPE_FREQ, dtype=F32) / N_ROPE_FREQ)
    ang = jnp.stack([row[:, None] * inv, col[:, None] * inv], axis=1)
    cos, sin = jnp.cos(ang), jnp.sin(ang)
    zero = jnp.zeros_like(sin)
    c = jnp.stack([cos, cos], axis=2).reshape(s, HEAD_DIM)
    s0 = jnp.stack([-sin, zero], axis=2).reshape(s, HEAD_DIM)
    s1 = jnp.stack([zero, sin], axis=2).reshape(s, HEAD_DIM)
    ident = jnp.ones((ROW_TILE, HEAD_DIM), F32)
    zpad = jnp.zeros((ROW_TILE, HEAD_DIM), F32)
    return (jnp.concatenate([c, ident], 0), jnp.concatenate([s0, zpad], 0), jnp.concatenate([s1, zpad], 0))


def _dt_cols():
    cols = []
    for g in range(SSD_GROUPS):
        for d in range(2):
            cols += [SDT_OFF + d * SSD_HEADS + g * HEADS_PER_GROUP + hh for hh in range(HEADS_PER_GROUP)]
    return cols


def _group_dt_param(v):
    out = []
    for g in range(SSD_GROUPS):
        hs = slice(g * HEADS_PER_GROUP, (g + 1) * HEADS_PER_GROUP)
        row = jnp.concatenate([v[0, hs], v[1, hs], jnp.zeros((LANES - 2 * HEADS_PER_GROUP,), v.dtype)])
        out.append(row.reshape(1, LANES))
    return jnp.stack(out, 0)


def _transpose_cast_kernel(w_ref, o_ref):
    o_ref[0] = w_ref[0].T.astype(o_ref.dtype)


def transpose_cast(w_t, first_row, n_rows, block, out_dtype):
    nl, _, k = w_t.shape
    assert first_row % (2 * SUBLANES) == 0 and n_rows % block == 0
    return pl.pallas_call(
        _transpose_cast_kernel, grid=(nl, n_rows // block),
        in_specs=[pl.BlockSpec((pl.Element(1), pl.Element(block), pl.Element(k)),
                               lambda l, j: (l, pl.multiple_of(first_row + j * block, 2 * SUBLANES), 0))],
        out_specs=pl.BlockSpec((1, k, block), lambda l, j: (l, 0, j)),
        out_shape=jax.ShapeDtypeStruct((nl, k, n_rows), out_dtype),
        compiler_params=_cparams(("arbitrary", "arbitrary")), name="transpose_cast",
    )(w_t)


def split_w_in(w_in):
    w_t = jnp.swapaxes(w_in, 1, 2)
    n_gate = w_in.shape[2] - MIX_COLS
    return (transpose_cast(w_t, 0, SDT_OFF, 256, BF16),
            transpose_cast(w_t, MIX_COLS, n_gate, 256, BF16),
            transpose_cast(w_t, SDT_OFF, LANES, LANES, F32))


def _dt_weight(dt_block):
    nl, d = dt_block.shape[:2]
    pad = jnp.zeros((nl, d, LANES - 2 * HEADS_PER_GROUP), dt_block.dtype)
    dtw = dt_block[:, :, :MIX_COLS - SDT_OFF].reshape(nl, d, 2, SSD_GROUPS, HEADS_PER_GROUP)
    parts = []
    for g in range(SSD_GROUPS):
        parts += [dtw[:, :, :, g, :].reshape(nl, d, 2 * HEADS_PER_GROUP), pad]
    return jnp.concatenate(parts, axis=2).astype(BF16)


def run_model(dims, x, c, ctx, c_ctx, w_mod_a, w_mod_b, b_mod, g_mix, g_ffn, w_in, w_up, w_o, q_norm, k_norm,
              rnn_conv_w, rnn_conv_b, rnn_lambda, rnn_w_r, rnn_b_r, rnn_w_i, rnn_b_i,
              ssd_conv_w, ssd_conv_b, ssd_dt_bias, ssd_a_log, ssd_d, ssd_norm,
              router_w, router_b, moe_w_gate, moe_w_up, moe_w_down, g_final):
    depth = w_in.shape[0]
    bsz = dims.batch
    xs = jnp.concatenate([x.reshape(dims.n_lat, D_MODEL), ctx.reshape(bsz * dims.ctx, D_MODEL)], axis=0)

    cond = jnp.zeros((SUBLANES, D_MODEL), F32).at[0].set(c_ctx).at[1:1 + bsz].set(c)
    mod_all = adaln_all(cond, w_mod_a, w_mod_b, b_mod).reshape(depth, SUBLANES * N_MOD, 1, D_MODEL)
    rope_c, rope_s0, rope_s1 = rope_tables(dims)
    router_wt = router_w.T.astype(BF16)

    w_mix_all, w_gate_all, dt_block = split_w_in(w_in)
    w_dt_all = _dt_weight(dt_block)
    w_up_all = w_up.astype(BF16)
    w_o_all = w_o.astype(BF16)
    moe_g_all = moe_w_gate.astype(BF16)
    moe_u_all = moe_w_up.astype(BF16)
    moe_d_all = moe_w_down.astype(BF16)

    for l in range(depth):
        last = l == depth - 1
        m_rows = dims.n_lat if last else dims.n_tok
        modtab = mod_all[l]

        h = norm_mod(dims, xs, g_mix[l], modtab, 0, dims.n_tok)
        p = matmul(h, w_mix_all, l, 1536, F32)
        dt = matmul(h, w_dt_all, l, SSD_GROUPS * LANES, F32)

        qn, kn, vn = qkv_prep(dims, p, rope_c, rope_s0, rope_s1, q_norm[l], k_norm[l])
        ya = attention(dims, qn, kn, vn)

        w_gates = jnp.concatenate([rnn_w_r[l, 0], rnn_w_i[l, 0], rnn_w_r[l, 1], rnn_w_i[l, 1]], axis=-1).astype(BF16)
        b_gates = jnp.concatenate(
            [v.reshape(RNN_BLOCKS, 1, RNN_BLOCK_DIM) for v in (rnn_b_r[l, 0], rnn_b_i[l, 0], rnn_b_r[l, 1], rnn_b_i[l, 1])],
            axis=-1)
        lam = jnp.concatenate([rnn_lambda[l, d].reshape(RNN_BLOCKS, 1, RNN_BLOCK_DIM) for d in range(2)], axis=-1)
        yr = rglru_branch(dims, p, rnn_conv_w[l], rnn_conv_b[l], w_gates, b_gates, lam)

        xbc = ssd_prep(dims, p, ssd_conv_w[l], ssd_conv_b[l])
        yf, yb = ssd_scan(dims, xbc, dt, _group_dt_param(ssd_dt_bias[l]), _group_dt_param(ssd_a_log[l]))
        d_chan = jnp.repeat(ssd_d[l], SSD_HEAD_DIM).reshape(1, SSD_WIDTH)
        ys = ssd_finish(dims, yf, yb, xbc, p, d_chan, ssd_norm[l])

        merged = merge_branches(h, ya, yr, ys, w_gate_all, w_up_all, l, m_rows)
        xs = matmul_residual(dims, merged, w_o_all, l, xs, modtab, 2, m_rows)

        h2, logits_t = norm_mod(dims, xs, g_ffn[l], modtab, 3, m_rows, router_wt)
        cw_t, dest, tab = route(logits_t, router_b)
        dest = dest.reshape(m_rows)
        cw_rows = jnp.pad(cw_t.T, ((0, 0), (0, LANES - SUBLANES)))
        n_tiles = m_rows // MOE_TILE + N_EXPERT_GROUPS
        src = invert_slots(dest, n_tiles * MOE_TILE)
        used = tab[3:3 + EXPERTS_PER_GROUP].reshape(EXPERTS_PER_GROUP * LANES)
        ys2 = moe_grouped(h2, cw_rows, tab[0], tab[1, :1], tab[2], used, src, moe_g_all, moe_u_all, moe_d_all, l, n_tiles)
        xs = combine(dims, dest, xs, ys2, modtab, 5, m_rows)

    out = final_norm(xs, g_final, dims.n_lat)
    return out.reshape(bsz, dims.seq, D_MODEL)


def kernel(x, c, ctx, c_ctx, w_mod_a, w_mod_b, b_mod, g_mix, g_ffn, w_in, w_up, w_o, q_norm, k_norm, rnn_conv_w, rnn_conv_b, rnn_lambda, rnn_w_r, rnn_b_r, rnn_w_i, rnn_b_i, ssd_conv_w, ssd_conv_b, ssd_dt_bias, ssd_a_log, ssd_d, ssd_norm, router_w, router_b, moe_w_gate, moe_w_up, moe_w_down, g_final):
    dims = Dims(batch=x.shape[0], seq=x.shape[1], ctx=ctx.shape[1])
    return run_model(dims, x, c, ctx, c_ctx, w_mod_a, w_mod_b, b_mod, g_mix, g_ffn, w_in, w_up, w_o, q_norm, k_norm,
                     rnn_conv_w, rnn_conv_b, rnn_lambda, rnn_w_r, rnn_b_r, rnn_w_i, rnn_b_i,
                     ssd_conv_w, ssd_conv_b, ssd_dt_bias, ssd_a_log, ssd_d, ssd_norm,
                     router_w, router_b, moe_w_gate, moe_w_up, moe_w_down, g_final)
```
